```python
import math
import jax, jax.numpy as jnp
from jax import lax
import numpy as np

D_MODEL = 1024
BATCH = 8
SEQ = 4096
DEPTH = 4

CTX_LEN = 256
GRID_W = 64
F32 = jnp.float32

GLA_HEADS = 4
GLA_DK = 64
GLA_DV = 64
GLA_GATE_RANK = 16
GLA_GATE_NORM = 16.0
GLA_CHUNK = 64
DIFF_HEADS = 4
DIFF_DH = 32
DIFF_DV = 2 * DIFF_DH
SWA_HEADS = 8
SWA_KV_HEADS = 2
SWA_Q_PER_KV = SWA_HEADS // SWA_KV_HEADS
SWA_DH = 64
WINDOW = 128
BLOCK = 128
ROPE_BASE = 10000.0

GLA_QK_W = GLA_HEADS * GLA_DK
GLA_V_W = GLA_HEADS * GLA_DV
DIFF_QK_W = DIFF_HEADS * 2 * DIFF_DH
DIFF_V_W = DIFF_HEADS * DIFF_DV
SWA_Q_W = SWA_HEADS * SWA_DH
SWA_KV_W = SWA_KV_HEADS * SWA_DH
IN_SPLITS = (GLA_QK_W, GLA_QK_W, GLA_V_W, GLA_V_W, GLA_GATE_RANK, GLA_GATE_RANK,
             DIFF_QK_W, DIFF_QK_W, DIFF_V_W, SWA_Q_W, SWA_KV_W, SWA_KV_W)
IN_WIDTH = 2 * GLA_QK_W + 2 * GLA_V_W + 2 * GLA_GATE_RANK + 2 * DIFF_QK_W + DIFF_V_W + SWA_Q_W + 2 * SWA_KV_W
MIX_WIDTH = GLA_V_W + DIFF_V_W + SWA_Q_W

FFN_HIDDEN = ((8 * D_MODEL // 3 + 255) // 256) * 256
DN_ALPHA = (2.0 * DEPTH) ** 0.25
DN_BETA = (8.0 * DEPTH) ** -0.25

kernel_name = 'hybrid_gla_diff_swa_dit_block'


def _ln_plain(x, eps=1e-6):
    xf = x.astype(F32)
    mu = jnp.mean(xf, -1, keepdims=True)
    var = jnp.mean(jnp.square(xf - mu), -1, keepdims=True)
    return (xf - mu) * lax.rsqrt(var + eps)


def layer_norm(x, g, b):
    return (_ln_plain(x) * g.astype(F32) + b.astype(F32)).astype(x.dtype)


def rms_norm(x, g, eps=1e-6):
    xf = x.astype(F32)
    y = xf * lax.rsqrt(jnp.mean(jnp.square(xf), -1, keepdims=True) + eps)
    return (y * g.astype(F32)).astype(x.dtype)


def modulate(x, shift, scale):
    return (_ln_plain(x) * (1.0 + scale.astype(F32)) + shift.astype(F32)).astype(x.dtype)


def axial_rope(rows, dim):
    row = jnp.repeat(jnp.arange(rows, dtype=F32), GRID_W)
    col = jnp.tile(jnp.arange(GRID_W, dtype=F32), rows)
    n_freq = dim // 4
    inv = jnp.power(ROPE_BASE, -jnp.arange(n_freq, dtype=F32) / n_freq)
    ang = jnp.concatenate([row[:, None] * inv, col[:, None] * inv], axis=-1)
    return jnp.cos(ang), jnp.sin(ang)


def apply_rope(x, cos, sin):
    half = x.shape[-1] // 2
    x1 = x[..., :half].astype(F32)
    x2 = x[..., half:].astype(F32)
    return jnp.concatenate([x1 * cos - x2 * sin, x1 * sin + x2 * cos], -1).astype(x.dtype)


def project_heads(p, w_gate2, b_gate2, rope_d, rope_s):
    bsz, n, _ = p.shape
    offs = [int(o) for o in np.cumsum(IN_SPLITS)[:-1]]
    g_q, g_k, g_v, g_o, g_zf, g_zb, d_q, d_k, d_v, s_q, s_k, s_v = jnp.split(p, offs, axis=-1)

    def heads(a, h):
        return jnp.moveaxis(a.reshape(bsz, n, h, -1), 2, 1)

    def gla_gate(z, w, b):
        return jax.nn.log_sigmoid((z @ w + b).astype(F32)) / GLA_GATE_NORM

    out = {
        'gq': heads(g_q, GLA_HEADS) * GLA_DK ** -0.5,
        'gk': heads(g_k, GLA_HEADS),
        'gv': heads(g_v, GLA_HEADS),
        'go': g_o,
        'gf': heads(gla_gate(g_zf, w_gate2[0], b_gate2[0]), GLA_HEADS),
        'gb': heads(gla_gate(g_zb, w_gate2[1], b_gate2[1]), GLA_HEADS),
        'dq': d_q.reshape(bsz, n, DIFF_HEADS, 2, DIFF_DH).transpose(0, 2, 3, 1, 4),
        'dk': d_k.reshape(bsz, n, DIFF_HEADS, 2, DIFF_DH).transpose(0, 2, 3, 1, 4),
        'dv': heads(d_v, DIFF_HEADS),
        'sq': s_q.reshape(bsz, n, SWA_KV_HEADS, SWA_Q_PER_KV, SWA_DH).transpose(0, 2, 3, 1, 4),
        'sk': heads(s_k, SWA_KV_HEADS),
        'sv': heads(s_v, SWA_KV_HEADS),
    }
    if rope_d is not None:
        for name in ('dq', 'dk'):
            out[name] = apply_rope(out[name], *rope_d)
        for name in ('sq', 'sk'):
            out[name] = apply_rope(out[name], *rope_s)
    return out


def gla_chunked(q, k, v, g, s0):
    bsz, nh, n, _ = q.shape
    nc = n // GLA_CHUNK

    def to_chunks(a):
        a = a.astype(F32).reshape(bsz, nh, nc, GLA_CHUNK, a.shape[-1])
        return jnp.moveaxis(a, 2, 0)

    lower = jnp.tril(jnp.ones((GLA_CHUNK, GLA_CHUNK), dtype=bool))[:, :, None]

    def step(state, inp):
        qc, kc, vc, gc = inp
        b = jnp.cumsum(gc, axis=2)
        rel = jnp.exp(jnp.where(lower, b[:, :, :, None, :] - b[:, :, None, :, :], -jnp.inf))
        att = jnp.einsum('bhtk,bhsk,bhtsk->bhts', qc, kc, rel)
        o = (jnp.einsum('bhtk,bhkv->bhtv', qc * jnp.exp(b), state)
             + jnp.einsum('bhts,bhsv->bhtv', att, vc))
        b_end = b[:, :, -1:, :]
        new_state = (jnp.exp(b_end[:, :, 0, :])[..., None] * state
                     + jnp.einsum('bhsk,bhsv->bhkv', kc * jnp.exp(b_end - b), vc))
        return new_state, o

    s_fin, o = lax.scan(step, s0.astype(F32), (to_chunks(q), to_chunks(k), to_chunks(v), to_chunks(g)))
    o = jnp.moveaxis(o, 0, 2).reshape(bsz, nh, n, -1)
    return o.astype(v.dtype), s_fin


def gla_reverse(q, k, v, g, s0):
    o, s = gla_chunked(jnp.flip(q, 2), jnp.flip(k, 2), jnp.flip(v, 2), jnp.flip(g, 2), s0)
    return jnp.flip(o, 2), s


def gla_output(o, og, g):
    bsz, nh, n, dv = o.shape
    on = rms_norm(jnp.moveaxis(o, 1, 2), g)
    return (on * jax.nn.silu(og.reshape(bsz, n, nh, dv))).reshape(bsz, n, nh * dv)


def diff_attend(q, k, v, lam):
    s = jnp.einsum('bhctd,bhcmd->bhctm', q, k).astype(F32) * (DIFF_DH ** -0.5)
    p = jax.nn.softmax(s, axis=-1)
    w = p[:, :, 0] - lam * p[:, :, 1]
    return jnp.einsum('bhtm,bhmv->bhtv', w.astype(v.dtype), v)


def diff_latent(q, k_all, v_all, lam):
    bsz, nh, _, n, dh = q.shape
    nb = n // BLOCK
    qb = jnp.moveaxis(q.reshape(bsz, nh, 2, nb, BLOCK, dh), 3, 0)
    ob = lax.map(lambda qi: diff_attend(qi, k_all, v_all, lam), qb)
    return jnp.moveaxis(ob, 0, 2).reshape(bsz, nh, n, -1)


def diff_output(o, g, lam_init):
    bsz, nh, n, dv = o.shape
    on = rms_norm(o, g) * (1.0 - lam_init)
    return jnp.moveaxis(on, 1, 2).reshape(bsz, n, nh * dv)


def swa_latent(q, k, v, kc, vc, sink):
    bsz, ng, nr, n, dh = q.shape
    nb = n // BLOCK
    n_ctx = kc.shape[2]
    scale = dh ** -0.5
    pad = ((0, 0), (0, 0), (BLOCK, BLOCK), (0, 0))
    k_pad = jnp.pad(k, pad)
    v_pad = jnp.pad(v, pad)

    def one_block(i):
        start = i * BLOCK
        qi = lax.dynamic_slice_in_dim(q, start, BLOCK, axis=3)
        ki = lax.dynamic_slice_in_dim(k_pad, start, 3 * BLOCK, axis=2)
        vi = lax.dynamic_slice_in_dim(v_pad, start, 3 * BLOCK, axis=2)
        pos_q = start + jnp.arange(BLOCK)
        pos_k = start - BLOCK + jnp.arange(3 * BLOCK)
        valid = ((jnp.abs(pos_k[None, :] - pos_q[:, None]) <= WINDOW)
                 & (pos_k >= 0)[None, :] & (pos_k < n)[None, :])
        s_loc = jnp.einsum('bgrtd,bgjd->bgrtj', qi, ki).astype(F32) * scale
        s_loc = jnp.where(valid, s_loc, -jnp.inf)
        s_ctx = jnp.einsum('bgrtd,bgmd->bgrtm', qi, kc).astype(F32) * scale
        s_snk = jnp.broadcast_to(sink.astype(F32)[None, :, :, None, None], s_ctx.shape[:-1] + (1,))
        p = jax.nn.softmax(jnp.concatenate([s_ctx, s_loc, s_snk], axis=-1), axis=-1).astype(v.dtype)
        return (jnp.einsum('bgrtm,bgmd->bgrtd', p[..., :n_ctx], vc)
                + jnp.einsum('bgrtj,bgjd->bgrtd', p[..., n_ctx:n_ctx + 3 * BLOCK], vi))

    ob = lax.map(one_block, jnp.arange(nb))
    return jnp.moveaxis(ob, 0, 3).reshape(bsz, ng, nr, n, dh)


def swa_context(q, kc, vc, sink):
    s = jnp.einsum('bgrtd,bgmd->bgrtm', q, kc).astype(F32) * (q.shape[-1] ** -0.5)
    s_snk = jnp.broadcast_to(sink.astype(F32)[None, :, :, None, None], s.shape[:-1] + (1,))
    p = jax.nn.softmax(jnp.concatenate([s, s_snk], axis=-1), axis=-1)[..., :-1].astype(vc.dtype)
    return jnp.einsum('bgrtm,bgmd->bgrtd', p, vc)


def swa_output(o):
    bsz, ng, nr, n, dh = o.shape
    return o.transpose(0, 3, 1, 2, 4).reshape(bsz, n, ng * nr * dh)


def post_norm(h, y, gate, g, b):
    return layer_norm(DN_ALPHA * h + gate * y, g, b)


def swiglu_sublayer(h, shift, scale, gate, w_g, w_u, w_d, g, b):
    u = modulate(h, shift, scale)
    f = (jax.nn.silu(u @ w_g) * (u @ w_u)) @ w_d
    return post_norm(h, f, gate, g, b)


def setup_inputs(seed: int = 0) -> dict:
    key = jax.random.key(seed)
    ks = jax.random.split(key, 20)
    D = D_MODEL

    def nrm(k, shape, s):
        return jax.random.normal(k, shape, F32) * s

    return {
        'x': nrm(ks[0], (BATCH, SEQ, D), 1.0),
        'c': nrm(ks[1], (BATCH, D), 1.0),
        'ctx': nrm(ks[2], (BATCH, CTX_LEN, D), 1.0),
        'c_ctx': nrm(ks[3], (D,), 1.0),
        'w_ada': nrm(ks[4], (DEPTH, D, 6 * D), 0.5 * D ** -0.5),
        'b_ada': nrm(ks[5], (DEPTH, 6 * D), 0.02),
        'w_in': nrm(ks[6], (DEPTH, D, IN_WIDTH), D ** -0.5),
        'w_gla_gate': nrm(ks[7], (DEPTH, 2, GLA_GATE_RANK, GLA_QK_W), GLA_GATE_RANK ** -0.5),
        'b_gla_gate': nrm(ks[8], (DEPTH, 2, GLA_QK_W), 0.1),
        'gla_norm_g': 1.0 + nrm(ks[9], (DEPTH, GLA_DV), 0.02),
        'diff_lambda': nrm(ks[10], (DEPTH, 4, DIFF_DH), 0.1),
        'diff_norm_g': 1.0 + nrm(ks[11], (DEPTH, DIFF_DV), 0.02),
        'swa_sink': nrm(ks[12], (DEPTH, SWA_HEADS), 0.5),
        'w_out': nrm(ks[13], (DEPTH, MIX_WIDTH, D), MIX_WIDTH ** -0.5 * DN_BETA),
        'ln_g': 1.0 + nrm(ks[14], (DEPTH, 2, D), 0.02),
        'ln_b': nrm(ks[15], (DEPTH, 2, D), 0.02),
        'w_ffn_gate': nrm(ks[16], (DEPTH, D, FFN_HIDDEN), D ** -0.5),
        'w_ffn_up': nrm(ks[17], (DEPTH, D, FFN_HIDDEN), D ** -0.5),
        'w_ffn_down': nrm(ks[18], (DEPTH, FFN_HIDDEN, D), FFN_HIDDEN ** -0.5 * DN_BETA),
    }


def reference(x, c, ctx, c_ctx, w_ada, b_ada, w_in, w_gla_gate, b_gla_gate, gla_norm_g,
              diff_lambda, diff_norm_g, swa_sink, w_out, ln_g, ln_b, w_ffn_gate, w_ffn_up,
              w_ffn_down):
    bsz, n_lat, _ = x.shape
    rows = n_lat // GRID_W
    rope_d = axial_rope(rows, DIFF_DH)
    rope_s = axial_rope(rows, SWA_DH)
    silu_c = jax.nn.silu(c)
    silu_cc = jax.nn.silu(c_ctx)
    h_lat, h_ctx = x, ctx
    for layer in range(DEPTH):
        last = layer == DEPTH - 1
        lam_init = 0.8 - 0.6 * math.exp(-0.3 * layer)
        m_lat = jnp.split((silu_c @ w_ada[layer] + b_ada[layer])[:, None, :], 6, axis=-1)
        m_ctx = jnp.split(silu_cc @ w_ada[layer] + b_ada[layer], 6, axis=-1)

        pl = project_heads(modulate(h_lat, m_lat[0], m_lat[1]) @ w_in[layer],
                           w_gla_gate[layer], b_gla_gate[layer], rope_d, rope_s)
        pc = project_heads(modulate(h_ctx, m_ctx[0], m_ctx[1]) @ w_in[layer],
                           w_gla_gate[layer], b_gla_gate[layer], None, None)

        zero_state = jnp.zeros((bsz, GLA_HEADS, GLA_DK, GLA_DV), F32)
        oc_f, sc_f = gla_chunked(pc['gq'], pc['gk'], pc['gv'], pc['gf'], zero_state)
        oc_b, sc_b = gla_reverse(pc['gq'], pc['gk'], pc['gv'], pc['gb'], zero_state)
        ol_f, _ = gla_chunked(pl['gq'], pl['gk'], pl['gv'], pl['gf'], sc_f)
        ol_b, _ = gla_reverse(pl['gq'], pl['gk'], pl['gv'], pl['gb'], sc_b)
        gla_lat = gla_output(ol_f + ol_b, pl['go'], gla_norm_g[layer])

        lq1 = diff_lambda[layer, 0]
        lk1 = diff_lambda[layer, 1]
        lq2 = diff_lambda[layer, 2]
        lk2 = diff_lambda[layer, 3]
        lam = (jnp.exp(jnp.sum(lq1.astype(F32) * lk1.astype(F32)))
               - jnp.exp(jnp.sum(lq2.astype(F32) * lk2.astype(F32))) + lam_init)
        dk_all = jnp.concatenate([pc['dk'], pl['dk']], axis=3)
        dv_all = jnp.concatenate([pc['dv'], pl['dv']], axis=2)
        diff_lat = diff_output(diff_latent(pl['dq'], dk_all, dv_all, lam), diff_norm_g[layer], lam_init)

        sink = swa_sink[layer].reshape(SWA_KV_HEADS, SWA_Q_PER_KV)
        swa_lat = swa_output(swa_latent(pl['sq'], pl['sk'], pl['sv'], pc['sk'], pc['sv'], sink))

        y_lat = jnp.concatenate([gla_lat, diff_lat, swa_lat], axis=-1) @ w_out[layer]
        h_lat_mid = post_norm(h_lat, y_lat, m_lat[2], ln_g[layer, 0], ln_b[layer, 0])
        h_lat_next = swiglu_sublayer(h_lat_mid, m_lat[3], m_lat[4], m_lat[5], w_ffn_gate[layer],
                                     w_ffn_up[layer], w_ffn_down[layer], ln_g[layer, 1], ln_b[layer, 1])

        if not last:
            gla_ctx = gla_output(oc_f + oc_b, pc['go'], gla_norm_g[layer])
            diff_ctx = diff_output(diff_attend(pc['dq'], pc['dk'], pc['dv'], lam), diff_norm_g[layer], lam_init)
            swa_ctx = swa_output(swa_context(pc['sq'], pc['sk'], pc['sv'], sink))
            y_ctx = jnp.concatenate([gla_ctx, diff_ctx, swa_ctx], axis=-1) @ w_out[layer]
            h_ctx_mid = post_norm(h_ctx, y_ctx, m_ctx[2], ln_g[layer, 0], ln_b[layer, 0])
            h_ctx = swiglu_sublayer(h_ctx_mid, m_ctx[3], m_ctx[4], m_ctx[5], w_ffn_gate[layer],
                                    w_ffn_up[layer], w_ffn_down[layer], ln_g[layer, 1], ln_b[layer, 1])
        h_lat = h_lat_next
    return h_lat
```

```python
import functools
import math

import jax
import jax.numpy as jnp
from jax import lax
from jax.experimental import pallas as pl
from jax.experimental.pallas import tpu as pltpu

F32 = jnp.float32
MXU_DTYPE = jnp.bfloat16

GRID_W = 64
GLA_HEADS, GLA_DK, GLA_DV = 4, 64, 64
GLA_GATE_RANK = 16
GLA_GATE_NORM = 16.0
GLA_CHUNK = 64
GLA_SUB = 16
DIFF_HEADS, DIFF_DH = 4, 32
DIFF_DV = 2 * DIFF_DH
SWA_HEADS, SWA_KV_HEADS, SWA_DH = 8, 2, 64
WINDOW = 128
ROPE_BASE = 10000.0
LN_EPS = 1e-6

GLA_W = GLA_HEADS * GLA_DK
DIFF_W = DIFF_HEADS * 2 * DIFF_DH
SWA_Q_W = SWA_HEADS * SWA_DH
SWA_KV_W = SWA_KV_HEADS * SWA_DH
LANES = 128
TOK = 256
SWA_TQ = 128
VMEM_LIMIT = 52 * 1024 * 1024


def _mm(a, b):
    return jnp.dot(a.astype(MXU_DTYPE), b.astype(MXU_DTYPE), preferred_element_type=F32)


def _mm_nt(a, b):
    return lax.dot_general(a.astype(MXU_DTYPE), b.astype(MXU_DTYPE),
                           (((1,), (1,)), ((), ())), preferred_element_type=F32)


def _mm_tn(a, b):
    return lax.dot_general(a.astype(MXU_DTYPE), b.astype(MXU_DTYPE),
                           (((0,), (0,)), ((), ())), preferred_element_type=F32)


def _split_mm(a, b_exact):
    if MXU_DTYPE == F32:
        return jnp.dot(a, b_exact.astype(F32), preferred_element_type=F32)
    hi = a.astype(jnp.bfloat16)
    r1 = a - hi.astype(F32)
    mid = r1.astype(jnp.bfloat16)
    lo = (r1 - mid.astype(F32)).astype(jnp.bfloat16)
    b = b_exact.astype(jnp.bfloat16)
    return (jnp.dot(hi, b, preferred_element_type=F32)
            + jnp.dot(mid, b, preferred_element_type=F32)
            + jnp.dot(lo, b, preferred_element_type=F32))


def _split_mm_t(b_exact, a):
    if MXU_DTYPE == F32:
        return jnp.dot(b_exact.astype(F32), a, preferred_element_type=F32)
    hi = a.astype(jnp.bfloat16)
    r1 = a - hi.astype(F32)
    mid = r1.astype(jnp.bfloat16)
    lo = (r1 - mid.astype(F32)).astype(jnp.bfloat16)
    b = b_exact.astype(jnp.bfloat16)
    return (jnp.dot(b, hi, preferred_element_type=F32)
            + jnp.dot(b, mid, preferred_element_type=F32)
            + jnp.dot(b, lo, preferred_element_type=F32))


def _ln_plain(x):
    mu = jnp.mean(x, axis=-1, keepdims=True)
    xc = x - mu
    var = jnp.mean(xc * xc, axis=-1, keepdims=True)
    return xc * lax.rsqrt(var + LN_EPS)


def _silu(x):
    return x / (1.0 + jnp.exp(-x))


def _group_mean_matrix(width, group):
    r = lax.broadcasted_iota(jnp.int32, (width, width), 0) // group
    c = lax.broadcasted_iota(jnp.int32, (width, width), 1) // group
    return jnp.where(r == c, 1.0 / group, 0.0).astype(F32)


def _mods_kernel(a_ref, w_ref, b_ref, o_ref):
    a = a_ref[...]
    o_ref[...] = jnp.dot(_silu(a), w_ref[...], preferred_element_type=F32,
                         precision=lax.Precision.HIGHEST) + b_ref[...]


def _mods_call(cond, w_ada, b_ada):
    depth, d, width = w_ada.shape
    rows = cond.shape[0]
    bn = 1536
    return pl.pallas_call(
        _mods_kernel,
        grid=(depth, width // bn),
        in_specs=[
            pl.BlockSpec((rows, d), lambda l, n: (0, 0)),
            pl.BlockSpec((None, d, bn), lambda l, n: (l, 0, n)),
            pl.BlockSpec((None, 1, bn), lambda l, n: (l, 0, n)),
        ],
        out_specs=pl.BlockSpec((None, rows, bn), lambda l, n: (l, 0, n)),
        out_shape=jax.ShapeDtypeStruct((depth, rows, width), F32),
        compiler_params=pltpu.CompilerParams(vmem_limit_bytes=VMEM_LIMIT),
        name="adaln_mods",
    )(cond, w_ada, b_ada.reshape(depth, 1, width))


C_GQ, C_GK, C_GV, C_GO = 0, 256, 512, 768
C_DQ, C_DK, C_DV = 1024, 1280, 1536
C_SQ, C_SK, C_SV = 1792, 2304, 2432
C_Z = 2560
IN_PAD_W = 2688


def _rope(x, cos, sin_signed, half):
    outs = []
    lane = lax.broadcasted_iota(jnp.int32, (1, LANES), 1)
    first = (lane % (2 * half)) < half
    for s in range(x.shape[1] // LANES):
        xs = x[:, s * LANES:(s + 1) * LANES]
        up = pltpu.roll(xs, LANES - half, axis=1)
        dn = pltpu.roll(xs, half, axis=1)
        outs.append(xs * cos + jnp.where(first, up, dn) * sin_signed)
    return outs[0] if len(outs) == 1 else jnp.concatenate(outs, axis=1)


def _inproj_kernel(h_ref, mod_ref, w_ref, wg_ref, bg_ref, cd_ref, sd_ref, cs_ref, ss_ref,
                   gla_ref, gate_ref, dqkv_ref, sq_ref, skv_ref):
    x = h_ref[...]
    u = _ln_plain(x) * (1.0 + mod_ref[1:2, :]) + mod_ref[0:1, :]
    p = _mm(u, w_ref[...])

    gla_ref[:, 0:GLA_W] = p[:, C_GQ:C_GQ + GLA_W] * (GLA_DK ** -0.5)
    gla_ref[:, GLA_W:4 * GLA_W] = p[:, C_GK:C_GO + GLA_W]

    z = p[:, C_Z:C_Z + LANES]
    gpre = _mm(z, wg_ref[...]) + bg_ref[...]
    logsig = jnp.minimum(gpre, 0.0) - jnp.log(1.0 + jnp.exp(-jnp.abs(gpre)))
    gate_ref[...] = logsig * (1.0 / GLA_GATE_NORM)

    cd, sd = cd_ref[...], sd_ref[...]
    dq = _rope(p[:, C_DQ:C_DQ + DIFF_W], cd, sd, DIFF_DH // 2) * (DIFF_DH ** -0.5)
    dk = _rope(p[:, C_DK:C_DK + DIFF_W], cd, sd, DIFF_DH // 2)
    dqkv_ref[:, 0:DIFF_W] = dq.astype(dqkv_ref.dtype)
    dqkv_ref[:, DIFF_W:2 * DIFF_W] = dk.astype(dqkv_ref.dtype)
    dqkv_ref[:, 2 * DIFF_W:3 * DIFF_W] = p[:, C_DV:C_DV + DIFF_W].astype(dqkv_ref.dtype)

    cs, ss = cs_ref[...], ss_ref[...]
    sq = _rope(p[:, C_SQ:C_SQ + SWA_Q_W], cs, ss, SWA_DH // 2) * (SWA_DH ** -0.5)
    sq_ref[...] = sq.astype(sq_ref.dtype)
    sk = _rope(p[:, C_SK:C_SK + SWA_KV_W], cs, ss, SWA_DH // 2)
    sv = p[:, C_SV:C_SV + SWA_KV_W]
    lane = lax.broadcasted_iota(jnp.int32, (1, LANES), 1)
    low = lane < SWA_DH
    for idx, a in enumerate((sk, sv)):
        sw = pltpu.roll(a, SWA_DH, axis=1)
        base = idx * 2 * LANES
        skv_ref[:, base:base + LANES] = jnp.where(low, a, sw).astype(skv_ref.dtype)
        skv_ref[:, base + LANES:base + 2 * LANES] = jnp.where(low, sw, a).astype(skv_ref.dtype)


def _mod_index(b, j, *, layer, ctx_row):
    return (layer, jnp.where(j == 0, ctx_row, b), 0, 0)


def _inproj_call(h, mods, layer, ctx_row, w_in_p, wg, bg, tabs):
    bsz, s, d = h.shape
    nt = s // TOK
    act = MXU_DTYPE
    tab_spec = pl.BlockSpec((TOK, LANES), lambda b, j: (j, 0))
    return pl.pallas_call(
        _inproj_kernel,
        grid=(bsz, nt),
        in_specs=[
            pl.BlockSpec((None, TOK, d), lambda b, j: (b, j, 0)),
            pl.BlockSpec((None, None, 6, d), functools.partial(_mod_index, layer=layer, ctx_row=ctx_row)),
            pl.BlockSpec((d, IN_PAD_W), lambda b, j: (0, 0)),
            pl.BlockSpec((LANES, 2 * GLA_W), lambda b, j: (0, 0)),
            pl.BlockSpec((1, 2 * GLA_W), lambda b, j: (0, 0)),
            tab_spec, tab_spec, tab_spec, tab_spec,
        ],
        out_specs=[
            pl.BlockSpec((None, TOK, 4 * GLA_W), lambda b, j: (b, j, 0)),
            pl.BlockSpec((None, TOK, 2 * GLA_W), lambda b, j: (b, j, 0)),
            pl.BlockSpec((None, TOK, 3 * DIFF_W), lambda b, j: (b, j, 0)),
            pl.BlockSpec((None, TOK, SWA_Q_W), lambda b, j: (b, j, 0)),
            pl.BlockSpec((None, TOK, 4 * LANES), lambda b, j: (b, j, 0)),
        ],
        out_shape=[
            jax.ShapeDtypeStruct((bsz, s, 4 * GLA_W), F32),
            jax.ShapeDtypeStruct((bsz, s, 2 * GLA_W), F32),
            jax.ShapeDtypeStruct((bsz, s, 3 * DIFF_W), act),
            jax.ShapeDtypeStruct((bsz, s, SWA_Q_W), act),
            jax.ShapeDtypeStruct((bsz, s, 4 * LANES), act),
        ],
        compiler_params=pltpu.CompilerParams(
            dimension_semantics=("arbitrary", "arbitrary"), vmem_limit_bytes=VMEM_LIMIT),
        name="in_projection",
    )(h, mods, w_in_p, wg, bg, *tabs)


def _gla_direction(gla_ref, gate_ref, gate_col, o_ref, st_ref, reverse, consts):
    tri, head_rows, head_lane, blockdiag, sub_masks = consts
    c_, sub = GLA_CHUNK, GLA_SUB
    nsub = c_ // sub
    nchunk = TOK // c_
    g_all = gate_ref[:, gate_col:gate_col + GLA_W]
    b_all = _split_mm_t(tri, g_all)
    order = range(nchunk - 1, -1, -1) if reverse else range(nchunk)
    for c in order:
        r0 = c * c_
        q = gla_ref[r0:r0 + c_, 0:GLA_W]
        k = gla_ref[r0:r0 + c_, GLA_W:2 * GLA_W]
        v = gla_ref[r0:r0 + c_, 2 * GLA_W:3 * GLA_W]
        b = b_all[r0:r0 + c_, :]
        b_end = b[0:1, :] if reverse else b[c_ - 1:c_, :]
        o_parts = []
        for i in range(nsub):
            t0, t1 = i * sub, (i + 1) * sub
            if reverse:
                ref_b = b[t1:t1 + 1, :] if i < nsub - 1 else jnp.zeros((1, GLA_W), F32)
                k0, k1 = t0, c_
            else:
                ref_b = b[t0 - 1:t0, :] if i > 0 else jnp.zeros((1, GLA_W), F32)
                k0, k1 = 0, t1
            qd = q[t0:t1, :] * jnp.exp(b[t0:t1, :] - ref_b)
            lhs = jnp.concatenate([qd * head_lane[hh] for hh in range(GLA_HEADS)], axis=0)
            kk = k[k0:k1, :] * jnp.exp(ref_b - b[k0:k1, :])
            att = _mm_nt(lhs, kk)
            att = jnp.where(sub_masks[(reverse, i)], att, 0.0)
            res = _mm(att, v[k0:k1, :])
            o_i = res[0:sub, :] * head_lane[0]
            for hh in range(1, GLA_HEADS):
                o_i = o_i + res[hh * sub:(hh + 1) * sub, :] * head_lane[hh]
            o_parts.append(o_i)
        o_intra = jnp.concatenate(o_parts, axis=0)
        st = st_ref[...]
        o_cross = _mm_nt(q * jnp.exp(b), st)
        o_ref[r0:r0 + c_, :] = o_intra + o_cross
        kend = k * jnp.exp(b_end - b)
        upd = _mm_tn(v, kend)
        st_ref[...] = st * jnp.exp(b_end) + jnp.where(blockdiag, upd, 0.0)


def _gla_consts():
    c_, sub = GLA_CHUNK, GLA_SUB
    nsub = c_ // sub
    r = lax.broadcasted_iota(jnp.int32, (TOK, TOK), 0)
    cc = lax.broadcasted_iota(jnp.int32, (TOK, TOK), 1)
    same_chunk = (r // c_) == (cc // c_)
    tri_f = jnp.where(same_chunk & (cc <= r), 1.0, 0.0).astype(F32)
    tri_b = jnp.where(same_chunk & (cc >= r), 1.0, 0.0).astype(F32)
    lane = lax.broadcasted_iota(jnp.int32, (1, GLA_W), 1)
    head_lane = [jnp.where((lane // GLA_DK) == hh, 1.0, 0.0).astype(F32) for hh in range(GLA_HEADS)]
    blockdiag = (r // GLA_DV) == (cc // GLA_DK)
    sub_masks = {}
    for reverse in (False, True):
        for i in range(nsub):
            nk = (nsub - i) * sub if reverse else (i + 1) * sub
            rr = lax.broadcasted_iota(jnp.int32, (GLA_HEADS * sub, nk), 0) % sub
            kc = lax.broadcasted_iota(jnp.int32, (GLA_HEADS * sub, nk), 1)
            if reverse:
                sub_masks[(reverse, i)] = kc >= rr
            else:
                sub_masks[(reverse, i)] = (kc - (nk - sub)) <= rr
    return tri_f, tri_b, head_lane, blockdiag, sub_masks


def _gla_kernel(gla_f_ref, gate_f_ref, gla_b_ref, gate_b_ref, of_ref, ob_ref, stf_ref, stb_ref):
    @pl.when(pl.program_id(1) == 0)
    def _():
        stf_ref[...] = jnp.zeros_like(stf_ref)
        stb_ref[...] = jnp.zeros_like(stb_ref)

    tri_f, tri_b, head_lane, blockdiag, sub_masks = _gla_consts()
    _gla_direction(gla_f_ref, gate_f_ref, 0, of_ref, stf_ref, False,
                   (tri_f, None, head_lane, blockdiag, sub_masks))
    _gla_direction(gla_b_ref, gate_b_ref, GLA_W, ob_ref, stb_ref, True,
                   (tri_b, None, head_lane, blockdiag, sub_masks))


def _gla_call(gla_in, gates):
    bsz, s, _ = gla_in.shape
    nt = s // TOK

    def fwd(b, j):
        return (b, j, 0)

    def bwd(b, j):
        return (b, jnp.where(j == 0, 0, nt - j), 0)

    return pl.pallas_call(
        _gla_kernel,
        grid=(bsz, nt),
        in_specs=[
            pl.BlockSpec((None, TOK, 4 * GLA_W), fwd),
            pl.BlockSpec((None, TOK, 2 * GLA_W), fwd),
            pl.BlockSpec((None, TOK, 4 * GLA_W), bwd),
            pl.BlockSpec((None, TOK, 2 * GLA_W), bwd),
        ],
        out_specs=[
            pl.BlockSpec((None, TOK, GLA_W), fwd),
            pl.BlockSpec((None, TOK, GLA_W), bwd),
        ],
        out_shape=[jax.ShapeDtypeStruct((bsz, s, GLA_W), F32)] * 2,
        scratch_shapes=[pltpu.VMEM((GLA_W, GLA_W), F32), pltpu.VMEM((GLA_W, GLA_W), F32)],
        compiler_params=pltpu.CompilerParams(
            dimension_semantics=("arbitrary", "arbitrary"), vmem_limit_bytes=VMEM_LIMIT),
        name="gla_scan",
    )(gla_in, gates, gla_in, gates)


def _diff_kernel(q_ref, k_ref, v_ref, lam_ref, lam0_ref, g_ref, o_ref, *, n_ctx):
    lam_p = lam_ref[...]
    lam = (jnp.exp(jnp.sum(lam_p[0:1, :] * lam_p[1:2, :], axis=-1, keepdims=True))
           - jnp.exp(jnp.sum(lam_p[2:3, :] * lam_p[3:4, :], axis=-1, keepdims=True))
           + lam0_ref[...])
    lane = lax.broadcasted_iota(jnp.int32, (1, DIFF_W), 1)
    gm = _group_mean_matrix(DIFF_W, DIFF_DV)

    def attend(nk):
        q = q_ref[...]
        k = k_ref[0:nk, :]
        v = v_ref[0:nk, :]
        out = jnp.zeros((TOK, DIFF_W), F32)
        for hh in range(DIFF_HEADS):
            comps = []
            for cc in range(2):
                lo = hh * 2 * DIFF_DH + cc * DIFF_DH
                qm = jnp.where((lane >= lo) & (lane < lo + DIFF_DH), q, jnp.zeros_like(q))
                s = _mm_nt(qm, k)
                m = jnp.max(s, axis=-1, keepdims=True)
                p = jnp.exp(s - m)
                l = jnp.sum(p, axis=-1, keepdims=True)
                comps.append(_mm(p, v) / l)
            o_h = comps[0] - lam * comps[1]
            out = jnp.where((lane // DIFF_DV) == hh, o_h, out)
        ms = _split_mm(out * out, gm)
        o_ref[...] = (out * lax.rsqrt(ms + LN_EPS) * g_ref[...]).astype(o_ref.dtype)

    j = pl.program_id(1)

    @pl.when(j == 0)
    def _():
        attend(n_ctx)

    @pl.when(j > 0)
    def _():
        attend(k_ref.shape[0])


def _diff_call(dqkv, lam_p, lam0, g_eff):
    bsz, s, _ = dqkv.shape
    nt = s // TOK
    return pl.pallas_call(
        functools.partial(_diff_kernel, n_ctx=TOK),
        grid=(bsz, nt),
        in_specs=[
            pl.BlockSpec((None, TOK, DIFF_W), lambda b, j: (b, j, 0)),
            pl.BlockSpec((None, s, DIFF_W), lambda b, j: (b, 0, 1)),
            pl.BlockSpec((None, s, DIFF_W), lambda b, j: (b, 0, 2)),
            pl.BlockSpec((4, DIFF_DH), lambda b, j: (0, 0)),
            pl.BlockSpec((1, 1), lambda b, j: (0, 0)),
            pl.BlockSpec((1, DIFF_W), lambda b, j: (0, 0)),
        ],
        out_specs=pl.BlockSpec((None, TOK, DIFF_W), lambda b, j: (b, j, 0)),
        out_shape=jax.ShapeDtypeStruct((bsz, s, DIFF_W), MXU_DTYPE),
        compiler_params=pltpu.CompilerParams(
            dimension_semantics=("arbitrary", "arbitrary"), vmem_limit_bytes=VMEM_LIMIT),
        name="diff_attention",
    )(dqkv, dqkv, dqkv, lam_p, lam0, g_eff)


def _swa_kernel(q_ref, kv_ref, sink_ref, o_ref, *, n_ctx, n_lat):
    j = pl.program_id(1)
    tq = SWA_TQ
    wk = 3 * tq
    ctx_tiles = n_ctx // tq
    p0 = (j - ctx_tiles) * tq
    s_rows = n_ctx + n_lat
    start = jnp.clip(n_ctx + p0 - tq, 0, s_rows - wk)
    start = pl.multiple_of(start, tq)
    pos_q = p0 + lax.broadcasted_iota(jnp.int32, (tq, 1), 0)
    pos_k = start - n_ctx + lax.broadcasted_iota(jnp.int32, (1, wk), 1)
    valid = (jnp.abs(pos_k - pos_q) <= WINDOW) & (pos_k >= 0) & (j >= ctx_tiles)
    lane = lax.broadcasted_iota(jnp.int32, (1, LANES), 1)
    low = lane < SWA_DH
    q_per_kv = SWA_HEADS // SWA_KV_HEADS
    for pair in range(SWA_HEADS // 2):
        qp = q_ref[:, pair * LANES:(pair + 1) * LANES]
        grp = (2 * pair) // q_per_kv
        kc = kv_ref[0:n_ctx, grp * LANES:(grp + 1) * LANES]
        vc = kv_ref[0:n_ctx, 2 * LANES + grp * LANES:2 * LANES + (grp + 1) * LANES]
        kl = kv_ref[pl.ds(start, wk), grp * LANES:(grp + 1) * LANES]
        vl = kv_ref[pl.ds(start, wk), 2 * LANES + grp * LANES:2 * LANES + (grp + 1) * LANES]
        halves = []
        for half in range(2):
            hh = 2 * pair + half
            qh = jnp.where(low if half == 0 else ~low, qp, jnp.zeros_like(qp))
            s_c = _mm_nt(qh, kc)
            s_l = jnp.where(valid, _mm_nt(qh, kl), -jnp.inf)
            snk = sink_ref[hh:hh + 1, 0:1]
            m = jnp.maximum(jnp.maximum(jnp.max(s_c, axis=-1, keepdims=True),
                                        jnp.max(s_l, axis=-1, keepdims=True)), snk)
            p_c = jnp.exp(s_c - m)
            p_l = jnp.exp(s_l - m)
            l = (jnp.sum(p_c, axis=-1, keepdims=True) + jnp.sum(p_l, axis=-1, keepdims=True)
                 + jnp.exp(snk - m))
            halves.append((_mm(p_c, vc) + _mm(p_l, vl)) / l)
        o_ref[:, pair * LANES:(pair + 1) * LANES] = jnp.where(low, halves[0], halves[1]).astype(o_ref.dtype)


def _swa_call(sq, skv, sink_tile, n_ctx):
    bsz, s, _ = sq.shape
    return pl.pallas_call(
        functools.partial(_swa_kernel, n_ctx=n_ctx, n_lat=s - n_ctx),
        grid=(bsz, s // SWA_TQ),
        in_specs=[
            pl.BlockSpec((None, SWA_TQ, SWA_Q_W), lambda b, j: (b, j, 0)),
            pl.BlockSpec((None, s, 4 * LANES), lambda b, j: (b, 0, 0)),
            pl.BlockSpec((SWA_HEADS, LANES), lambda b, j: (0, 0)),
        ],
        out_specs=pl.BlockSpec((None, SWA_TQ, SWA_Q_W), lambda b, j: (b, j, 0)),
        out_shape=jax.ShapeDtypeStruct((bsz, s, SWA_Q_W), MXU_DTYPE),
        compiler_params=pltpu.CompilerParams(
            dimension_semantics=("arbitrary", "arbitrary"), vmem_limit_bytes=VMEM_LIMIT),
        name="windowed_gqa",
    )(sq, skv, sink_tile)


def _outproj_kernel(of_ref, ob_ref, og_ref, diff_ref, swa_ref, h_ref, mod_ref, w_ref,
                    gng_ref, lng_ref, lnb_ref, o_ref, *, alpha):
    o = of_ref[...] + ob_ref[...]
    ms = _split_mm(o * o, _group_mean_matrix(GLA_W, GLA_DV))
    gla = o * lax.rsqrt(ms + LN_EPS) * gng_ref[...] * _silu(og_ref[...])
    y = (_mm(gla, w_ref[0:GLA_W, :])
         + _mm(diff_ref[...], w_ref[GLA_W:GLA_W + DIFF_W, :])
         + _mm(swa_ref[...], w_ref[GLA_W + DIFF_W:, :]))
    z = alpha * h_ref[...] + mod_ref[2:3, :] * y
    o_ref[...] = _ln_plain(z) * lng_ref[...] + lnb_ref[...]


def _outproj_call(o_f, o_b, gla_in, diff_o, swa_o, h, mods, layer, ctx_row, w_out, gng, lng, lnb, alpha):
    bsz, s, d = h.shape
    nt = s // TOK
    tile = lambda w: pl.BlockSpec((None, TOK, w), lambda b, j: (b, j, 0))
    const = lambda shape: pl.BlockSpec(shape, lambda b, j: (0,) * len(shape))
    return pl.pallas_call(
        functools.partial(_outproj_kernel, alpha=alpha),
        grid=(bsz, nt),
        in_specs=[
            tile(GLA_W), tile(GLA_W),
            pl.BlockSpec((None, TOK, GLA_W), lambda b, j: (b, j, 3)),
            tile(DIFF_W), tile(SWA_Q_W), tile(d),
            pl.BlockSpec((None, None, 6, d), functools.partial(_mod_index, layer=layer, ctx_row=ctx_row)),
            const((d, d)), const((1, GLA_W)), const((1, d)), const((1, d)),
        ],
        out_specs=tile(d),
        out_shape=jax.ShapeDtypeStruct((bsz, s, d), F32),
        compiler_params=pltpu.CompilerParams(
            dimension_semantics=("arbitrary", "arbitrary"), vmem_limit_bytes=VMEM_LIMIT),
        name="out_projection",
    )(o_f, o_b, gla_in, diff_o, swa_o, h, mods, w_out, gng, lng, lnb)


def _ffn_kernel(h_ref, mod_ref, wg_ref, wu_ref, wd_ref, lng_ref, lnb_ref, o_ref, *, alpha, skip_first):
    def body():
        x = h_ref[...]
        u = (_ln_plain(x) * (1.0 + mod_ref[4:5, :]) + mod_ref[3:4, :]).astype(MXU_DTYPE)
        a = _silu(_mm(u, wg_ref[...])) * _mm(u, wu_ref[...])
        f = _mm(a, wd_ref[...])
        z = alpha * x + mod_ref[5:6, :] * f
        o_ref[...] = _ln_plain(z) * lng_ref[...] + lnb_ref[...]

    if skip_first:
        pl.when(pl.program_id(1) > 0)(body)
    else:
        body()


def _ffn_call(h, mods, layer, ctx_row, w_g, w_u, w_d, lng, lnb, alpha, latent_only):
    bsz, s, d = h.shape
    nt = s // TOK
    hid = w_g.shape[1]
    single = pl.Buffered(1)
    const = lambda shape: pl.BlockSpec(shape, lambda b, j: (0,) * len(shape), pipeline_mode=single)
    if latent_only:
        out_spec = pl.BlockSpec((None, TOK, d), lambda b, j: (b, jnp.maximum(j - 1, 0), 0))
        out_rows = s - TOK
    else:
        out_spec = pl.BlockSpec((None, TOK, d), lambda b, j: (b, j, 0))
        out_rows = s
    return pl.pallas_call(
        functools.partial(_ffn_kernel, alpha=alpha, skip_first=latent_only),
        grid=(bsz, nt),
        in_specs=[
            pl.BlockSpec((None, TOK, d), lambda b, j: (b, j, 0)),
            pl.BlockSpec((None, None, 6, d), functools.partial(_mod_index, layer=layer, ctx_row=ctx_row)),
            const((d, hid)), const((d, hid)), const((hid, d)),
            pl.BlockSpec((1, d), lambda b, j: (0, 0)), pl.BlockSpec((1, d), lambda b, j: (0, 0)),
        ],
        out_specs=out_spec,
        out_shape=jax.ShapeDtypeStruct((bsz, out_rows, d), F32),
        compiler_params=pltpu.CompilerParams(
            dimension_semantics=("arbitrary", "arbitrary"), vmem_limit_bytes=VMEM_LIMIT),
        name="swiglu_ffn",
    )(h, mods, w_g, w_u, w_d, lng, lnb)


def _rope_tables(rows, n_ctx, dim):
    row = jnp.repeat(jnp.arange(rows, dtype=F32), GRID_W)
    col = jnp.tile(jnp.arange(GRID_W, dtype=F32), rows)
    n_freq = dim // 4
    inv = jnp.power(ROPE_BASE, -jnp.arange(n_freq, dtype=F32) / n_freq)
    ang = jnp.concatenate([row[:, None] * inv, col[:, None] * inv], axis=-1)
    cos, sin = jnp.cos(ang), jnp.sin(ang)
    reps = LANES // dim
    cos_t = jnp.tile(jnp.concatenate([cos, cos], axis=-1), (1, reps))
    sin_t = jnp.tile(jnp.concatenate([-sin, sin], axis=-1), (1, reps))
    cos_t = jnp.concatenate([jnp.ones((n_ctx, LANES), F32), cos_t], axis=0)
    sin_t = jnp.concatenate([jnp.zeros((n_ctx, LANES), F32), sin_t], axis=0)
    return cos_t, sin_t


def _reorder_w_in(w_in):
    o = 0
    parts = {}
    for name, width in (("gq", GLA_W), ("gk", GLA_W), ("gv", GLA_W), ("go", GLA_W),
                        ("zf", GLA_GATE_RANK), ("zb", GLA_GATE_RANK),
                        ("dq", DIFF_W), ("dk", DIFF_W), ("dv", DIFF_W),
                        ("sq", SWA_Q_W), ("sk", SWA_KV_W), ("sv", SWA_KV_W)):
        parts[name] = w_in[:, o:o + width]
        o += width
    cols = [parts[n] for n in ("gq", "gk", "gv", "go", "dq", "dk", "dv", "sq", "sk", "sv", "zf", "zb")]
    pad = IN_PAD_W - sum(c.shape[1] for c in cols)
    cols.append(jnp.zeros((w_in.shape[0], pad), w_in.dtype))
    return jnp.concatenate(cols, axis=1)


def kernel(x, c, ctx, c_ctx, w_ada, b_ada, w_in, w_gla_gate, b_gla_gate, gla_norm_g, diff_lambda,
           diff_norm_g, swa_sink, w_out, ln_g, ln_b, w_ffn_gate, w_ffn_up, w_ffn_down):
    bsz, n_lat, d = x.shape
    n_ctx = ctx.shape[1]
    depth = w_ada.shape[0]
    assert n_ctx == TOK and n_lat % TOK == 0 and n_lat % GRID_W == 0
    alpha = (2.0 * depth) ** 0.25
    wdt = MXU_DTYPE

    cond_rows = 16
    cond = jnp.concatenate([c, c_ctx[None, :], jnp.zeros((cond_rows - bsz - 1, d), F32)], axis=0)
    mods = _mods_call(cond, w_ada, b_ada).reshape(depth, cond_rows, 6, d)
    ctx_row = bsz

    rows = n_lat // GRID_W
    tabs = _rope_tables(rows, n_ctx, DIFF_DH) + _rope_tables(rows, n_ctx, SWA_DH)

    h = jnp.concatenate([ctx, x], axis=1)
    for layer in range(depth):
        last = layer == depth - 1
        lam_init = 0.8 - 0.6 * math.exp(-0.3 * layer)
        w_in_p = _reorder_w_in(w_in[layer]).astype(wdt)
        wg = jnp.zeros((LANES, 2 * GLA_W), F32)
        wg = wg.at[0:GLA_GATE_RANK, 0:GLA_W].set(w_gla_gate[layer, 0])
        wg = wg.at[GLA_GATE_RANK:2 * GLA_GATE_RANK, GLA_W:].set(w_gla_gate[layer, 1])
        bg = b_gla_gate[layer].reshape(1, 2 * GLA_W)

        gla_in, gates, dqkv, sq, skv = _inproj_call(h, mods, layer, ctx_row, w_in_p, wg.astype(wdt), bg, tabs)
        o_f, o_b = _gla_call(gla_in, gates)
        diff_o = _diff_call(dqkv, diff_lambda[layer], jnp.full((1, 1), lam_init, F32),
                            (jnp.tile(diff_norm_g[layer], DIFF_HEADS) * (1.0 - lam_init)).reshape(1, DIFF_W))
        swa_o = _swa_call(sq, skv, jnp.broadcast_to(swa_sink[layer][:, None], (SWA_HEADS, LANES)), n_ctx)
        h_mid = _outproj_call(o_f, o_b, gla_in, diff_o, swa_o, h, mods, layer, ctx_row,
                              w_out[layer].astype(wdt),
                              jnp.tile(gla_norm_g[layer], GLA_HEADS).reshape(1, GLA_W),
                              ln_g[layer, 0].reshape(1, d), ln_b[layer, 0].reshape(1, d), alpha)
        h = _ffn_call(h_mid, mods, layer, ctx_row, w_ffn_gate[layer].astype(wdt),
                      w_ffn_up[layer].astype(wdt), w_ffn_down[layer].astype(wdt),
                      ln_g[layer, 1].reshape(1, d), ln_b[layer, 1].reshape(1, d), alpha, last)
    return h
```

```python
import functools
import math

import jax
import jax.numpy as jnp
from jax import lax
from jax.experimental import pallas as pl
from jax.experimental.pallas import tpu as pltpu

F32 = jnp.float32
MXU_DTYPE = jnp.bfloat16

GRID_W = 64
GLA_HEADS, GLA_DK, GLA_DV = 4, 64, 64
GLA_GATE_RANK = 16
GLA_GATE_NORM = 16.0
GLA_CHUNK = 64
GLA_SUB = 16
DIFF_HEADS, DIFF_DH = 4, 32
DIFF_DV = 2 * DIFF_DH
SWA_HEADS, SWA_KV_HEADS, SWA_DH = 8, 2, 64
WINDOW = 128
ROPE_BASE = 10000.0
LN_EPS = 1e-6

GLA_W = GLA_HEADS * GLA_DK
DIFF_W = DIFF_HEADS * 2 * DIFF_DH
SWA_Q_W = SWA_HEADS * SWA_DH
SWA_KV_W = SWA_KV_HEADS * SWA_DH
LANES = 128
TOK = 256
SWA_TQ = 128
DIFF_VROWS = DIFF_DV + 16
SWA_VROWS = SWA_DH + 16
LOG2_E = math.log2(math.e)
VMEM_LIMIT = 52 * 1024 * 1024


def _mm(a, b):
    return jnp.dot(a.astype(MXU_DTYPE), b.astype(MXU_DTYPE), preferred_element_type=F32)


def _mm_nt(a, b):
    return lax.dot_general(a.astype(MXU_DTYPE), b.astype(MXU_DTYPE),
                           (((1,), (1,)), ((), ())), preferred_element_type=F32)


def _mm_tn(a, b):
    return lax.dot_general(a.astype(MXU_DTYPE), b.astype(MXU_DTYPE),
                           (((0,), (0,)), ((), ())), preferred_element_type=F32)


def _split_mm(a, b_exact):
    if MXU_DTYPE == F32:
        return jnp.dot(a, b_exact.astype(F32), preferred_element_type=F32)
    hi = a.astype(jnp.bfloat16)
    r1 = a - hi.astype(F32)
    mid = r1.astype(jnp.bfloat16)
    lo = (r1 - mid.astype(F32)).astype(jnp.bfloat16)
    b = b_exact.astype(jnp.bfloat16)
    return (jnp.dot(hi, b, preferred_element_type=F32)
            + jnp.dot(mid, b, preferred_element_type=F32)
            + jnp.dot(lo, b, preferred_element_type=F32))


def _split_mm_t(b_exact, a):
    if MXU_DTYPE == F32:
        return jnp.dot(b_exact.astype(F32), a, preferred_element_type=F32)
    hi = a.astype(jnp.bfloat16)
    r1 = a - hi.astype(F32)
    mid = r1.astype(jnp.bfloat16)
    lo = (r1 - mid.astype(F32)).astype(jnp.bfloat16)
    b = b_exact.astype(jnp.bfloat16)
    return (jnp.dot(b, hi, preferred_element_type=F32)
            + jnp.dot(b, mid, preferred_element_type=F32)
            + jnp.dot(b, lo, preferred_element_type=F32))


def _ln_plain(x):
    mu = jnp.mean(x, axis=-1, keepdims=True)
    xc = x - mu
    var = jnp.mean(xc * xc, axis=-1, keepdims=True)
    return xc * lax.rsqrt(var + LN_EPS)


def _silu(x):
    return x / (1.0 + jnp.exp(-x))


def _group_mean_matrix(width, group):
    r = lax.broadcasted_iota(jnp.int32, (width, width), 0) // group
    c = lax.broadcasted_iota(jnp.int32, (width, width), 1) // group
    return jnp.where(r == c, 1.0 / group, 0.0).astype(F32)


def _mods_kernel(a_ref, w_ref, b_ref, o_ref):
    a = a_ref[...]
    o_ref[...] = jnp.dot(_silu(a), w_ref[...], preferred_element_type=F32,
                         precision=lax.Precision.HIGHEST) + b_ref[...]


def _mods_call(cond, w_ada, b_ada):
    depth, d, width = w_ada.shape
    rows = cond.shape[0]
    bn = 1536
    return pl.pallas_call(
        _mods_kernel,
        grid=(depth, width // bn),
        in_specs=[
            pl.BlockSpec((rows, d), lambda l, n: (0, 0)),
            pl.BlockSpec((None, d, bn), lambda l, n: (l, 0, n)),
            pl.BlockSpec((None, 1, bn), lambda l, n: (l, 0, n)),
        ],
        out_specs=pl.BlockSpec((None, rows, bn), lambda l, n: (l, 0, n)),
        out_shape=jax.ShapeDtypeStruct((depth, rows, width), F32),
        compiler_params=pltpu.CompilerParams(vmem_limit_bytes=VMEM_LIMIT),
        name="adaln_mods",
    )(cond, w_ada, b_ada.reshape(depth, 1, width))


C_GQ, C_GK, C_GV, C_GO = 0, 256, 512, 768
C_DQ, C_DK, C_DV = 1024, 1280, 1536
C_SQ, C_SK, C_SV = 1792, 2304, 2432
C_Z = 2560
IN_PAD_W = 2688


def _rope(x, cos, sin_signed, half):
    outs = []
    lane = lax.broadcasted_iota(jnp.int32, (1, LANES), 1)
    first = (lane % (2 * half)) < half
    for s in range(x.shape[1] // LANES):
        xs = x[:, s * LANES:(s + 1) * LANES]
        up = pltpu.roll(xs, LANES - half, axis=1)
        dn = pltpu.roll(xs, half, axis=1)
        outs.append(xs * cos + jnp.where(first, up, dn) * sin_signed)
    return outs[0] if len(outs) == 1 else jnp.concatenate(outs, axis=1)


def _inproj_kernel(h_ref, mod_ref, w_ref, wg_ref, bg_ref, cd_ref, sd_ref, cs_ref, ss_ref,
                   gla_ref, gate_ref, dqk_ref, dvt_ref, sq_ref, sk_ref, svt_ref):
    x = h_ref[...]
    u = _ln_plain(x) * (1.0 + mod_ref[1:2, :]) + mod_ref[0:1, :]
    p = _mm(u, w_ref[...])

    gla_ref[:, 0:GLA_W] = p[:, C_GQ:C_GQ + GLA_W] * (GLA_DK ** -0.5)
    gla_ref[:, GLA_W:4 * GLA_W] = p[:, C_GK:C_GO + GLA_W]

    z = p[:, C_Z:C_Z + LANES]
    gpre = _mm(z, wg_ref[...]) + bg_ref[...]
    logsig = jnp.minimum(gpre, 0.0) - jnp.log(1.0 + jnp.exp(-jnp.abs(gpre)))
    gate_ref[...] = logsig * (1.0 / GLA_GATE_NORM)

    cd, sd = cd_ref[...], sd_ref[...]
    dq = _rope(p[:, C_DQ:C_DQ + DIFF_W], cd, sd, DIFF_DH // 2) * (DIFF_DH ** -0.5 * LOG2_E)
    dk = _rope(p[:, C_DK:C_DK + DIFF_W], cd, sd, DIFF_DH // 2)
    dqk_ref[:, 0:DIFF_W] = dq.astype(dqk_ref.dtype)
    dqk_ref[:, DIFF_W:2 * DIFF_W] = dk.astype(dqk_ref.dtype)
    vt = p[:, C_DV:C_DV + DIFF_W].T
    ones_row = jnp.where(lax.broadcasted_iota(jnp.int32, (DIFF_VROWS - DIFF_DV, TOK), 0) == 0, 1.0, 0.0)
    for hh in range(DIFF_HEADS):
        dvt_ref[hh, 0:DIFF_DV, :] = vt[hh * DIFF_DV:(hh + 1) * DIFF_DV, :].astype(dvt_ref.dtype)
        dvt_ref[hh, DIFF_DV:DIFF_VROWS, :] = ones_row.astype(dvt_ref.dtype)

    cs, ss = cs_ref[...], ss_ref[...]
    sq = _rope(p[:, C_SQ:C_SQ + SWA_Q_W], cs, ss, SWA_DH // 2) * (SWA_DH ** -0.5 * LOG2_E)
    sq_ref[...] = sq.astype(sq_ref.dtype)
    sk = _rope(p[:, C_SK:C_SK + SWA_KV_W], cs, ss, SWA_DH // 2)
    lane = lax.broadcasted_iota(jnp.int32, (1, LANES), 1)
    low = lane < SWA_DH
    sw = pltpu.roll(sk, SWA_DH, axis=1)
    sk_ref[:, 0:LANES] = jnp.where(low, sk, sw).astype(sk_ref.dtype)
    sk_ref[:, LANES:2 * LANES] = jnp.where(low, sw, sk).astype(sk_ref.dtype)
    svt = p[:, C_SV:C_SV + SWA_KV_W].T
    ones_s = jnp.where(lax.broadcasted_iota(jnp.int32, (SWA_VROWS - SWA_DH, SWA_TQ), 0) == 0, 1.0, 0.0)
    for grp in range(SWA_KV_HEADS):
        for i in range(TOK // SWA_TQ):
            svt_ref[grp, i, 0:SWA_DH, :] = svt[grp * SWA_DH:(grp + 1) * SWA_DH,
                                               i * SWA_TQ:(i + 1) * SWA_TQ].astype(svt_ref.dtype)
            svt_ref[grp, i, SWA_DH:SWA_VROWS, :] = ones_s.astype(svt_ref.dtype)


def _mod_index(b, j, *, layer, ctx_row):
    return (layer, jnp.where(j == 0, ctx_row, b), 0, 0)


def _inproj_call(h, mods, layer, ctx_row, w_in_p, wg, bg, tabs):
    bsz, s, d = h.shape
    nt = s // TOK
    act = MXU_DTYPE
    tab_spec = pl.BlockSpec((TOK, LANES), lambda b, j: (j, 0))
    return pl.pallas_call(
        _inproj_kernel,
        grid=(bsz, nt),
        in_specs=[
            pl.BlockSpec((None, TOK, d), lambda b, j: (b, j, 0)),
            pl.BlockSpec((None, None, 6, d), functools.partial(_mod_index, layer=layer, ctx_row=ctx_row)),
            pl.BlockSpec((d, IN_PAD_W), lambda b, j: (0, 0)),
            pl.BlockSpec((LANES, 2 * GLA_W), lambda b, j: (0, 0)),
            pl.BlockSpec((1, 2 * GLA_W), lambda b, j: (0, 0)),
            tab_spec, tab_spec, tab_spec, tab_spec,
        ],
        out_specs=[
            pl.BlockSpec((None, TOK, 4 * GLA_W), lambda b, j: (b, j, 0)),
            pl.BlockSpec((None, TOK, 2 * GLA_W), lambda b, j: (b, j, 0)),
            pl.BlockSpec((None, TOK, 2 * DIFF_W), lambda b, j: (b, j, 0)),
            pl.BlockSpec((None, DIFF_HEADS, DIFF_VROWS, TOK), lambda b, j: (b, 0, 0, j)),
            pl.BlockSpec((None, TOK, SWA_Q_W), lambda b, j: (b, j, 0)),
            pl.BlockSpec((None, TOK, 2 * LANES), lambda b, j: (b, j, 0)),
            pl.BlockSpec((None, SWA_KV_HEADS, TOK // SWA_TQ, SWA_VROWS, SWA_TQ), lambda b, j: (b, 0, j, 0, 0)),
        ],
        out_shape=[
            jax.ShapeDtypeStruct((bsz, s, 4 * GLA_W), F32),
            jax.ShapeDtypeStruct((bsz, s, 2 * GLA_W), F32),
            jax.ShapeDtypeStruct((bsz, s, 2 * DIFF_W), act),
            jax.ShapeDtypeStruct((bsz, DIFF_HEADS, DIFF_VROWS, s), act),
            jax.ShapeDtypeStruct((bsz, s, SWA_Q_W), act),
            jax.ShapeDtypeStruct((bsz, s, 2 * LANES), act),
            jax.ShapeDtypeStruct((bsz, SWA_KV_HEADS, s // SWA_TQ, SWA_VROWS, SWA_TQ), act),
        ],
        compiler_params=pltpu.CompilerParams(
            dimension_semantics=("arbitrary", "arbitrary"), vmem_limit_bytes=VMEM_LIMIT),
        name="in_projection",
    )(h, mods, w_in_p, wg, bg, *tabs)


def _gla_direction(gla_ref, gate_ref, gate_col, o_ref, st_ref, reverse, consts):
    tri, head_rows, head_lane, blockdiag, sub_masks = consts
    c_, sub = GLA_CHUNK, GLA_SUB
    nsub = c_ // sub
    nchunk = TOK // c_
    g_all = gate_ref[:, gate_col:gate_col + GLA_W]
    b_all = _split_mm_t(tri, g_all)
    order = range(nchunk - 1, -1, -1) if reverse else range(nchunk)
    for c in order:
        r0 = c * c_
        q = gla_ref[r0:r0 + c_, 0:GLA_W]
        k = gla_ref[r0:r0 + c_, GLA_W:2 * GLA_W]
        v = gla_ref[r0:r0 + c_, 2 * GLA_W:3 * GLA_W]
        b = b_all[r0:r0 + c_, :]
        b_end = b[0:1, :] if reverse else b[c_ - 1:c_, :]
        o_parts = []
        for i in range(nsub):
            t0, t1 = i * sub, (i + 1) * sub
            if reverse:
                ref_b = b[t1:t1 + 1, :] if i < nsub - 1 else jnp.zeros((1, GLA_W), F32)
                k0, k1 = t0, c_
            else:
                ref_b = b[t0 - 1:t0, :] if i > 0 else jnp.zeros((1, GLA_W), F32)
                k0, k1 = 0, t1
            qd = q[t0:t1, :] * jnp.exp(b[t0:t1, :] - ref_b)
            lhs = jnp.concatenate([qd * head_lane[hh] for hh in range(GLA_HEADS)], axis=0)
            kk = k[k0:k1, :] * jnp.exp(ref_b - b[k0:k1, :])
            att = _mm_nt(lhs, kk)
            att = jnp.where(sub_masks[(reverse, i)], att, 0.0)
            res = _mm(att, v[k0:k1, :])
            o_i = res[0:sub, :] * head_lane[0]
            for hh in range(1, GLA_HEADS):
                o_i = o_i + res[hh * sub:(hh + 1) * sub, :] * head_lane[hh]
            o_parts.append(o_i)
        o_intra = jnp.concatenate(o_parts, axis=0)
        st = st_ref[...]
        o_cross = _mm_nt(q * jnp.exp(b), st)
        o_ref[r0:r0 + c_, :] = o_intra + o_cross
        kend = k * jnp.exp(b_end - b)
        upd = _mm_tn(v, kend)
        st_ref[...] = st * jnp.exp(b_end) + jnp.where(blockdiag, upd, 0.0)


def _gla_consts():
    c_, sub = GLA_CHUNK, GLA_SUB
    nsub = c_ // sub
    r = lax.broadcasted_iota(jnp.int32, (TOK, TOK), 0)
    cc = lax.broadcasted_iota(jnp.int32, (TOK, TOK), 1)
    same_chunk = (r // c_) == (cc // c_)
    tri_f = jnp.where(same_chunk & (cc <= r), 1.0, 0.0).astype(F32)
    tri_b = jnp.where(same_chunk & (cc >= r), 1.0, 0.0).astype(F32)
    lane = lax.broadcasted_iota(jnp.int32, (1, GLA_W), 1)
    head_lane = [jnp.where((lane // GLA_DK) == hh, 1.0, 0.0).astype(F32) for hh in range(GLA_HEADS)]
    blockdiag = (r // GLA_DV) == (cc // GLA_DK)
    sub_masks = {}
    for reverse in (False, True):
        for i in range(nsub):
            nk = (nsub - i) * sub if reverse else (i + 1) * sub
            rr = lax.broadcasted_iota(jnp.int32, (GLA_HEADS * sub, nk), 0) % sub
            kc = lax.broadcasted_iota(jnp.int32, (GLA_HEADS * sub, nk), 1)
            if reverse:
                sub_masks[(reverse, i)] = kc >= rr
            else:
                sub_masks[(reverse, i)] = (kc - (nk - sub)) <= rr
    return tri_f, tri_b, head_lane, blockdiag, sub_masks


def _gla_kernel(gla_f_ref, gate_f_ref, gla_b_ref, gate_b_ref, of_ref, ob_ref, stf_ref, stb_ref):
    @pl.when(pl.program_id(1) == 0)
    def _():
        stf_ref[...] = jnp.zeros_like(stf_ref)
        stb_ref[...] = jnp.zeros_like(stb_ref)

    tri_f, tri_b, head_lane, blockdiag, sub_masks = _gla_consts()
    _gla_direction(gla_f_ref, gate_f_ref, 0, of_ref, stf_ref, False,
                   (tri_f, None, head_lane, blockdiag, sub_masks))
    _gla_direction(gla_b_ref, gate_b_ref, GLA_W, ob_ref, stb_ref, True,
                   (tri_b, None, head_lane, blockdiag, sub_masks))


def _gla_call(gla_in, gates):
    bsz, s, _ = gla_in.shape
    nt = s // TOK

    def fwd(b, j):
        return (b, j, 0)

    def bwd(b, j):
        return (b, jnp.where(j == 0, 0, nt - j), 0)

    return pl.pallas_call(
        _gla_kernel,
        grid=(bsz, nt),
        in_specs=[
            pl.BlockSpec((None, TOK, 4 * GLA_W), fwd),
            pl.BlockSpec((None, TOK, 2 * GLA_W), fwd),
            pl.BlockSpec((None, TOK, 4 * GLA_W), bwd),
            pl.BlockSpec((None, TOK, 2 * GLA_W), bwd),
        ],
        out_specs=[
            pl.BlockSpec((None, TOK, GLA_W), fwd),
            pl.BlockSpec((None, TOK, GLA_W), bwd),
        ],
        out_shape=[jax.ShapeDtypeStruct((bsz, s, GLA_W), F32)] * 2,
        scratch_shapes=[pltpu.VMEM((GLA_W, GLA_W), F32), pltpu.VMEM((GLA_W, GLA_W), F32)],
        compiler_params=pltpu.CompilerParams(
            dimension_semantics=("arbitrary", "arbitrary"), vmem_limit_bytes=VMEM_LIMIT),
        name="gla_scan",
    )(gla_in, gates, gla_in, gates)


def _diff_kernel(q_ref, k_ref, vt_ref, lam_ref, lam0_ref, g_ref, o_ref, *, n_ctx):
    lam_p = lam_ref[...]
    lam = (jnp.exp(jnp.sum(lam_p[0:1, :] * lam_p[1:2, :], axis=-1, keepdims=True))
           - jnp.exp(jnp.sum(lam_p[2:3, :] * lam_p[3:4, :], axis=-1, keepdims=True))
           + lam0_ref[...])
    lane = lax.broadcasted_iota(jnp.int32, (1, DIFF_W), 1)

    def attend(nk, parts):
        q = q_ref[...]
        step = nk // parts

        def scores(idx):
            lo = idx * DIFF_DH
            qm = jnp.where((lane >= lo) & (lane < lo + DIFF_DH), q, jnp.zeros_like(q))
            return [_mm_nt(k_ref[i * step:(i + 1) * step, :], qm).astype(MXU_DTYPE)
                    for i in range(parts)]

        heads = []
        nxt = scores(0)
        for hh in range(DIFF_HEADS):
            comps = []
            for cc in range(2):
                sts = nxt
                if 2 * hh + cc + 1 < 2 * DIFF_HEADS:
                    nxt = scores(2 * hh + cc + 1)
                m = _col_max(sts[0])
                for st in sts[1:]:
                    m = jnp.maximum(m, _col_max(st))
                acc = None
                for i, st in enumerate(sts):
                    pv = _mm(vt_ref[hh, :, i * step:(i + 1) * step], jnp.exp2(st - m))
                    acc = pv if acc is None else acc + pv
                comps.append(acc[0:DIFF_DV, :] / acc[DIFF_DV:DIFF_DV + 1, :])
            o_h = comps[0] - lam * comps[1]
            ms = jnp.mean(o_h * o_h, axis=0, keepdims=True)
            heads.append(o_h * lax.rsqrt(ms + LN_EPS))
        out = jnp.concatenate(heads, axis=0).T
        o_ref[...] = (out * g_ref[...]).astype(o_ref.dtype)

    j = pl.program_id(1)

    @pl.when(j == 0)
    def _():
        attend(n_ctx, 1)

    @pl.when(j > 0)
    def _():
        attend(k_ref.shape[0], 2)


def _col_max(x):
    r = x.shape[0]
    slab = 16
    while r % (2 * slab) == 0 and r // slab > 32:
        slab *= 2
    acc = x[0:slab, :]
    for i in range(1, r // slab):
        acc = jnp.maximum(acc, x[i * slab:(i + 1) * slab, :])
    return jnp.max(acc, axis=0, keepdims=True)


def _diff_call(dqk, dvt, lam_p, lam0, g_eff):
    bsz, s, _ = dqk.shape
    nt = s // TOK
    return pl.pallas_call(
        functools.partial(_diff_kernel, n_ctx=TOK),
        grid=(bsz, nt),
        in_specs=[
            pl.BlockSpec((None, TOK, DIFF_W), lambda b, j: (b, j, 0)),
            pl.BlockSpec((None, s, DIFF_W), lambda b, j: (b, 0, 1)),
            pl.BlockSpec((None, DIFF_HEADS, DIFF_VROWS, s), lambda b, j: (b, 0, 0, 0)),
            pl.BlockSpec((4, DIFF_DH), lambda b, j: (0, 0)),
            pl.BlockSpec((1, 1), lambda b, j: (0, 0)),
            pl.BlockSpec((1, DIFF_W), lambda b, j: (0, 0)),
        ],
        out_specs=pl.BlockSpec((None, TOK, DIFF_W), lambda b, j: (b, j, 0)),
        out_shape=jax.ShapeDtypeStruct((bsz, s, DIFF_W), MXU_DTYPE),
        compiler_params=pltpu.CompilerParams(
            dimension_semantics=("arbitrary", "arbitrary"), vmem_limit_bytes=VMEM_LIMIT),
        name="diff_attention",
    )(dqk, dqk, dvt, lam_p, lam0, g_eff)


def _swa_kernel(q_ref, k_ref, vt_ref, sink_ref, o_ref, *, n_ctx, n_lat):
    j = pl.program_id(1)
    tq = SWA_TQ
    n_win = 3
    n_ctx_t = n_ctx // tq
    s_rows = n_ctx + n_lat
    lane = lax.broadcasted_iota(jnp.int32, (1, LANES), 1)
    low = lane < SWA_DH
    q_per_kv = SWA_HEADS // SWA_KV_HEADS
    chains = [(sub, grp) for sub in range(TOK // tq) for grp in range(SWA_KV_HEADS)]

    def window(sub):
        p0 = j * TOK + sub * tq - n_ctx
        start = pl.multiple_of(jnp.clip(n_ctx + p0 - tq, 0, s_rows - n_win * tq), tq)
        return p0, start

    def scores(idx):
        sub, grp = chains[idx]
        _, start = window(sub)
        blocks = []
        for r in range(q_per_kv):
            hh = grp * q_per_kv + r
            qp = q_ref[sub * tq:(sub + 1) * tq, (hh // 2) * LANES:(hh // 2 + 1) * LANES]
            blocks.append(jnp.where(low if hh % 2 == 0 else ~low, qp, jnp.zeros_like(qp)))
        lhs = jnp.concatenate(blocks, axis=0)
        st_c = _mm_nt(k_ref[0:n_ctx, grp * LANES:(grp + 1) * LANES], lhs)
        st_l = _mm_nt(k_ref[pl.ds(start, n_win * tq), grp * LANES:(grp + 1) * LANES], lhs)
        return st_c, st_l

    nxt = scores(0)
    for idx, (sub, grp) in enumerate(chains):
        st_c, st_l = nxt
        if idx + 1 < len(chains):
            nxt = scores(idx + 1)
        p0, start = window(sub)
        pos_q = p0 + lax.broadcasted_iota(jnp.int32, (1, tq), 1)
        pos_k = start - n_ctx + lax.broadcasted_iota(jnp.int32, (n_win * tq, 1), 0)
        valid = (jnp.abs(pos_k - pos_q) <= WINDOW) & (pos_k >= 0) & (p0 >= 0)
        bias = jnp.where(valid, 0.0, -jnp.inf).astype(F32)
        st_l = st_l + jnp.concatenate([bias] * q_per_kv, axis=1)
        st_c = st_c.astype(MXU_DTYPE)
        st_l = st_l.astype(MXU_DTYPE)
        snk = jnp.concatenate([sink_ref[grp * q_per_kv + r:grp * q_per_kv + r + 1, :]
                               for r in range(q_per_kv)], axis=1)
        m = jnp.maximum(jnp.maximum(_col_max(st_c), _col_max(st_l)).astype(F32), snk)
        mb = m.astype(MXU_DTYPE)
        p_c = jnp.exp2(st_c - mb)
        p_l = jnp.exp2(st_l - mb)
        t0 = start // tq
        acc = None
        for i in range(n_ctx_t):
            pv = _mm(vt_ref[grp, i], p_c[i * tq:(i + 1) * tq, :])
            acc = pv if acc is None else acc + pv
        for i in range(n_win):
            acc = acc + _mm(vt_ref[grp, t0 + i], p_l[i * tq:(i + 1) * tq, :])
        l = acc[SWA_DH:SWA_DH + 1, :] + jnp.exp2(snk - mb.astype(F32))
        o = acc[0:SWA_DH, :] / l
        for pr in range(q_per_kv // 2):
            pair = grp * (q_per_kv // 2) + pr
            both = jnp.concatenate([o[:, (2 * pr) * tq:(2 * pr + 1) * tq],
                                    o[:, (2 * pr + 1) * tq:(2 * pr + 2) * tq]], axis=0)
            o_ref[sub * tq:(sub + 1) * tq, pair * LANES:(pair + 1) * LANES] = both.T.astype(o_ref.dtype)


def _swa_call(sq, sk, svt, sink_tile, n_ctx):
    bsz, s, _ = sq.shape
    return pl.pallas_call(
        functools.partial(_swa_kernel, n_ctx=n_ctx, n_lat=s - n_ctx),
        grid=(bsz, s // TOK),
        in_specs=[
            pl.BlockSpec((None, TOK, SWA_Q_W), lambda b, j: (b, j, 0)),
            pl.BlockSpec((None, s, 2 * LANES), lambda b, j: (b, 0, 0)),
            pl.BlockSpec((None, SWA_KV_HEADS, s // SWA_TQ, SWA_VROWS, SWA_TQ), lambda b, j: (b, 0, 0, 0, 0)),
            pl.BlockSpec((SWA_HEADS, LANES), lambda b, j: (0, 0)),
        ],
        out_specs=pl.BlockSpec((None, TOK, SWA_Q_W), lambda b, j: (b, j, 0)),
        out_shape=jax.ShapeDtypeStruct((bsz, s, SWA_Q_W), MXU_DTYPE),
        compiler_params=pltpu.CompilerParams(
            dimension_semantics=("arbitrary", "arbitrary"), vmem_limit_bytes=VMEM_LIMIT),
        name="windowed_gqa",
    )(sq, sk, svt, sink_tile)


def _outproj_kernel(of_ref, ob_ref, og_ref, diff_ref, swa_ref, h_ref, mod_ref, w_ref,
                    gng_ref, lng_ref, lnb_ref, o_ref, *, alpha):
    o = of_ref[...] + ob_ref[...]
    ms = _split_mm(o * o, _group_mean_matrix(GLA_W, GLA_DV))
    gla = o * lax.rsqrt(ms + LN_EPS) * gng_ref[...] * _silu(og_ref[...])
    y = (_mm(gla, w_ref[0:GLA_W, :])
         + _mm(diff_ref[...], w_ref[GLA_W:GLA_W + DIFF_W, :])
         + _mm(swa_ref[...], w_ref[GLA_W + DIFF_W:, :]))
    z = alpha * h_ref[...] + mod_ref[2:3, :] * y
    o_ref[...] = _ln_plain(z) * lng_ref[...] + lnb_ref[...]


def _outproj_call(o_f, o_b, gla_in, diff_o, swa_o, h, mods, layer, ctx_row, w_out, gng, lng, lnb, alpha):
    bsz, s, d = h.shape
    nt = s // TOK
    tile = lambda w: pl.BlockSpec((None, TOK, w), lambda b, j: (b, j, 0))
    const = lambda shape: pl.BlockSpec(shape, lambda b, j: (0,) * len(shape))
    return pl.pallas_call(
        functools.partial(_outproj_kernel, alpha=alpha),
        grid=(bsz, nt),
        in_specs=[
            tile(GLA_W), tile(GLA_W),
            pl.BlockSpec((None, TOK, GLA_W), lambda b, j: (b, j, 3)),
            tile(DIFF_W), tile(SWA_Q_W), tile(d),
            pl.BlockSpec((None, None, 6, d), functools.partial(_mod_index, layer=layer, ctx_row=ctx_row)),
            const((d, d)), const((1, GLA_W)), const((1, d)), const((1, d)),
        ],
        out_specs=tile(d),
        out_shape=jax.ShapeDtypeStruct((bsz, s, d), F32),
        compiler_params=pltpu.CompilerParams(
            dimension_semantics=("arbitrary", "arbitrary"), vmem_limit_bytes=VMEM_LIMIT),
        name="out_projection",
    )(o_f, o_b, gla_in, diff_o, swa_o, h, mods, w_out, gng, lng, lnb)


def _ffn_kernel(h_ref, mod_ref, wg_ref, wu_ref, wd_ref, lng_ref, lnb_ref, o_ref, *, alpha, skip_first):
    def body():
        x = h_ref[...]
        u = (_ln_plain(x) * (1.0 + mod_ref[4:5, :]) + mod_ref[3:4, :]).astype(MXU_DTYPE)
        a = _silu(_mm(u, wg_ref[...])) * _mm(u, wu_ref[...])
        f = _mm(a, wd_ref[...])
        z = alpha * x + mod_ref[5:6, :] * f
        o_ref[...] = _ln_plain(z) * lng_ref[...] + lnb_ref[...]

    if skip_first:
        pl.when(pl.program_id(1) > 0)(body)
    else:
        body()


def _ffn_call(h, mods, layer, ctx_row, w_g, w_u, w_d, lng, lnb, alpha, latent_only):
    bsz, s, d = h.shape
    nt = s // TOK
    hid = w_g.shape[1]
    single = pl.Buffered(1)
    const = lambda shape: pl.BlockSpec(shape, lambda b, j: (0,) * len(shape), pipeline_mode=single)
    if latent_only:
        out_spec = pl.BlockSpec((None, TOK, d), lambda b, j: (b, jnp.maximum(j - 1, 0), 0))
        out_rows = s - TOK
    else:
        out_spec = pl.BlockSpec((None, TOK, d), lambda b, j: (b, j, 0))
        out_rows = s
    return pl.pallas_call(
        functools.partial(_ffn_kernel, alpha=alpha, skip_first=latent_only),
        grid=(bsz, nt),
        in_specs=[
            pl.BlockSpec((None, TOK, d), lambda b, j: (b, j, 0)),
            pl.BlockSpec((None, None, 6, d), functools.partial(_mod_index, layer=layer, ctx_row=ctx_row)),
            const((d, hid)), const((d, hid)), const((hid, d)),
            pl.BlockSpec((1, d), lambda b, j: (0, 0)), pl.BlockSpec((1, d), lambda b, j: (0, 0)),
        ],
        out_specs=out_spec,
        out_shape=jax.ShapeDtypeStruct((bsz, out_rows, d), F32),
        compiler_params=pltpu.CompilerParams(
            dimension_semantics=("arbitrary", "arbitrary"), vmem_limit_bytes=VMEM_LIMIT),
        name="swiglu_ffn",
    )(h, mods, w_g, w_u, w_d, lng, lnb)


def _rope_tables(rows, n_ctx, dim):
    row = jnp.repeat(jnp.arange(rows, dtype=F32), GRID_W)
    col = jnp.tile(jnp.arange(GRID_W, dtype=F32), rows)
    n_freq = dim // 4
    inv = jnp.power(ROPE_BASE, -jnp.arange(n_freq, dtype=F32) / n_freq)
    ang = jnp.concatenate([row[:, None] * inv, col[:, None] * inv], axis=-1)
    cos, sin = jnp.cos(ang), jnp.sin(ang)
    reps = LANES // dim
    cos_t = jnp.tile(jnp.concatenate([cos, cos], axis=-1), (1, reps))
    sin_t = jnp.tile(jnp.concatenate([-sin, sin], axis=-1), (1, reps))
    cos_t = jnp.concatenate([jnp.ones((n_ctx, LANES), F32), cos_t], axis=0)
    sin_t = jnp.concatenate([jnp.zeros((n_ctx, LANES), F32), sin_t], axis=0)
    return cos_t, sin_t


def _reorder_w_in(w_in):
    o = 0
    parts = {}
    for name, width in (("gq", GLA_W), ("gk", GLA_W), ("gv", GLA_W), ("go", GLA_W),
                        ("zf", GLA_GATE_RANK), ("zb", GLA_GATE_RANK),
                        ("dq", DIFF_W), ("dk", DIFF_W), ("dv", DIFF_W),
                        ("sq", SWA_Q_W), ("sk", SWA_KV_W), ("sv", SWA_KV_W)):
        parts[name] = w_in[:, o:o + width]
        o += width
    cols = [parts[n] for n in ("gq", "gk", "gv", "go", "dq", "dk", "dv", "sq", "sk", "sv", "zf", "zb")]
    pad = IN_PAD_W - sum(c.shape[1] for c in cols)
    cols.append(jnp.zeros((w_in.shape[0], pad), w_in.dtype))
    return jnp.concatenate(cols, axis=1)


def kernel(x, c, ctx, c_ctx, w_ada, b_ada, w_in, w_gla_gate, b_gla_gate, gla_norm_g, diff_lambda,
           diff_norm_g, swa_sink, w_out, ln_g, ln_b, w_ffn_gate, w_ffn_up, w_ffn_down):
    bsz, n_lat, d = x.shape
    n_ctx = ctx.shape[1]
    depth = w_ada.shape[0]
    assert n_ctx == TOK and n_lat % TOK == 0 and n_lat % GRID_W == 0
    alpha = (2.0 * depth) ** 0.25
    wdt = MXU_DTYPE

    cond_rows = -(-(bsz + 1) // 8) * 8
    cond =jnp.concatenate([c, c_ctx[None, :], jnp.zeros((cond_rows - bsz - 1, d), F32)], axis=0)
    mods = _mods_call(cond, w_ada, b_ada).reshape(depth, cond_rows, 6, d)
    ctx_row = bsz

    rows = n_lat // GRID_W
    tabs = _rope_tables(rows, n_ctx, DIFF_DH) + _rope_tables(rows, n_ctx, SWA_DH)

    h = jnp.concatenate([ctx, x], axis=1)
    for layer in range(depth):
        last = layer == depth - 1
        lam_init = 0.8 - 0.6 * math.exp(-0.3 * layer)
        w_in_p = _reorder_w_in(w_in[layer]).astype(wdt)
        wg = jnp.zeros((LANES, 2 * GLA_W), F32)
        wg = wg.at[0:GLA_GATE_RANK, 0:GLA_W].set(w_gla_gate[layer, 0])
        wg = wg.at[GLA_GATE_RANK:2 * GLA_GATE_RANK, GLA_W:].set(w_gla_gate[layer, 1])
        bg = b_gla_gate[layer].reshape(1, 2 * GLA_W)

        gla_in, gates, dqk, dvt, sq, sk, svt = _inproj_call(h, mods, layer, ctx_row, w_in_p, wg.astype(wdt), bg, tabs)
        o_f, o_b = _gla_call(gla_in, gates)
        diff_o = _diff_call(dqk, dvt, diff_lambda[layer], jnp.full((1, 1), lam_init, F32),
                            (jnp.tile(diff_norm_g[layer], DIFF_HEADS) * (1.0 - lam_init)).reshape(1, DIFF_W))
        swa_o = _swa_call(sq, sk, svt,
                          jnp.broadcast_to(swa_sink[layer][:, None] * LOG2_E, (SWA_HEADS, LANES)), n_ctx)
        h_mid = _outproj_call(o_f, o_b, gla_in, diff_o, swa_o, h, mods, layer, ctx_row,
                              w_out[layer].astype(wdt),
                              jnp.tile(gla_norm_g[layer], GLA_HEADS).reshape(1, GLA_W),
                              ln_g[layer, 0].reshape(1, d), ln_b[layer, 0].reshape(1, d), alpha)
        h = _ffn_call(h_mid, mods, layer, ctx_row, w_ffn_gate[layer].astype(wdt),
                      w_ffn_up[layer].astype(wdt), w_ffn_down[layer].astype(wdt),
                      ln_g[layer, 1].reshape(1, d), ln_b[layer, 1].reshape(1, d), alpha, last)
    return h
```

```python
import functools
import math

import jax
import jax.numpy as jnp
from jax import lax
from jax.experimental import pallas as pl
from jax.experimental.pallas import tpu as pltpu

F32 = jnp.float32
MXU_DTYPE = jnp.bfloat16

GRID_W = 64
GLA_HEADS, GLA_DK, GLA_DV = 4, 64, 64
GLA_GATE_RANK = 16
GLA_GATE_NORM = 16.0
GLA_CHUNK = 64
GLA_SUB = 16
DIFF_HEADS, DIFF_DH = 4, 32
DIFF_DV = 2 * DIFF_DH
SWA_HEADS, SWA_KV_HEADS, SWA_DH = 8, 2, 64
WINDOW = 128
ROPE_BASE = 10000.0
LN_EPS = 1e-6

GLA_W = GLA_HEADS * GLA_DK
DIFF_W = DIFF_HEADS * 2 * DIFF_DH
SWA_Q_W = SWA_HEADS * SWA_DH
SWA_KV_W = SWA_KV_HEADS * SWA_DH
LANES = 128
TOK = 256
SWA_TQ = 128
DIFF_VROWS = DIFF_DV + 16
SWA_VROWS = SWA_DH + 16
LOG2_E = math.log2(math.e)
VMEM_LIMIT = 52 * 1024 * 1024


def _mm(a, b):
    return jnp.dot(a.astype(MXU_DTYPE), b.astype(MXU_DTYPE), preferred_element_type=F32)


def _mm_nt(a, b):
    return lax.dot_general(a.astype(MXU_DTYPE), b.astype(MXU_DTYPE),
                           (((1,), (1,)), ((), ())), preferred_element_type=F32)


def _mm_tn(a, b):
    return lax.dot_general(a.astype(MXU_DTYPE), b.astype(MXU_DTYPE),
                           (((0,), (0,)), ((), ())), preferred_element_type=F32)


def _split_mm(a, b_exact):
    if MXU_DTYPE == F32:
        return jnp.dot(a, b_exact.astype(F32), preferred_element_type=F32)
    hi = a.astype(jnp.bfloat16)
    r1 = a - hi.astype(F32)
    mid = r1.astype(jnp.bfloat16)
    lo = (r1 - mid.astype(F32)).astype(jnp.bfloat16)
    b = b_exact.astype(jnp.bfloat16)
    return (jnp.dot(hi, b, preferred_element_type=F32)
            + jnp.dot(mid, b, preferred_element_type=F32)
            + jnp.dot(lo, b, preferred_element_type=F32))


def _split_mm_t(b_exact, a):
    if MXU_DTYPE == F32:
        return jnp.dot(b_exact.astype(F32), a, preferred_element_type=F32)
    hi = a.astype(jnp.bfloat16)
    r1 = a - hi.astype(F32)
    mid = r1.astype(jnp.bfloat16)
    lo = (r1 - mid.astype(F32)).astype(jnp.bfloat16)
    b = b_exact.astype(jnp.bfloat16)
    return (jnp.dot(b, hi, preferred_element_type=F32)
            + jnp.dot(b, mid, preferred_element_type=F32)
            + jnp.dot(b, lo, preferred_element_type=F32))


def _ln_plain(x):
    mu = jnp.mean(x, axis=-1, keepdims=True)
    xc = x - mu
    var = jnp.mean(xc * xc, axis=-1, keepdims=True)
    return xc * lax.rsqrt(var + LN_EPS)


def _silu(x):
    return x / (1.0 + jnp.exp(-x))


def _group_mean_matrix(width, group):
    r = lax.broadcasted_iota(jnp.int32, (width, width), 0) // group
    c = lax.broadcasted_iota(jnp.int32, (width, width), 1) // group
    return jnp.where(r == c, 1.0 / group, 0.0).astype(F32)


def _mods_kernel(a_ref, w_ref, b_ref, o_ref):
    a = a_ref[...]
    o_ref[...] = jnp.dot(_silu(a), w_ref[...], preferred_element_type=F32,
                         precision=lax.Precision.HIGHEST) + b_ref[...]


def _mods_call(cond, w_ada, b_ada):
    depth, d, width = w_ada.shape
    rows = cond.shape[0]
    bn = 1536
    return pl.pallas_call(
        _mods_kernel,
        grid=(depth, width // bn),
        in_specs=[
            pl.BlockSpec((rows, d), lambda l, n: (0, 0)),
            pl.BlockSpec((None, d, bn), lambda l, n: (l, 0, n)),
            pl.BlockSpec((None, 1, bn), lambda l, n: (l, 0, n)),
        ],
        out_specs=pl.BlockSpec((None, rows, bn), lambda l, n: (l, 0, n)),
        out_shape=jax.ShapeDtypeStruct((depth, rows, width), F32),
        compiler_params=pltpu.CompilerParams(vmem_limit_bytes=VMEM_LIMIT),
        name="adaln_mods",
    )(cond, w_ada, b_ada.reshape(depth, 1, width))


C_GQ, C_GK, C_GV, C_GO = 0, 256, 512, 768
C_DQ, C_DK, C_DV = 1024, 1280, 1536
C_SQ, C_SK, C_SV = 1792, 2304, 2432
C_Z = 2560
IN_PAD_W = 2688


def _rope(x, cos, sin_signed, half):
    outs = []
    lane = lax.broadcasted_iota(jnp.int32, (1, LANES), 1)
    first = (lane % (2 * half)) < half
    for s in range(x.shape[1] // LANES):
        xs = x[:, s * LANES:(s + 1) * LANES]
        up = pltpu.roll(xs, LANES - half, axis=1)
        dn = pltpu.roll(xs, half, axis=1)
        outs.append(xs * cos + jnp.where(first, up, dn) * sin_signed)
    return outs[0] if len(outs) == 1 else jnp.concatenate(outs, axis=1)


def _inproj_kernel(h_ref, mod_ref, w_ref, wg_ref, bg_ref, cd_ref, sd_ref, cs_ref, ss_ref,
                   gla_ref, gate_ref, dqk_ref, dvt_ref, sq_ref, sk_ref, svt_ref):
    x = h_ref[...]
    u = _ln_plain(x) * (1.0 + mod_ref[1:2, :]) + mod_ref[0:1, :]
    ub = u.astype(MXU_DTYPE)

    def project(c0, c1):
        return jnp.dot(ub, w_ref[:, c0:c1], preferred_element_type=F32)

    z = project(C_Z, C_Z + LANES)
    pd = project(C_DQ, C_DV + DIFF_W)

    gpre = _mm(z, wg_ref[...]) + bg_ref[...]
    logsig = jnp.minimum(gpre, 0.0) - jnp.log(1.0 + jnp.exp(-jnp.abs(gpre)))
    gate_ref[...] = logsig * (1.0 / GLA_GATE_NORM)

    ps = project(C_SQ, C_SV + SWA_KV_W)

    cd, sd = cd_ref[...], sd_ref[...]
    dq = _rope(pd[:, 0:DIFF_W], cd, sd, DIFF_DH // 2) * (DIFF_DH ** -0.5 * LOG2_E)
    dk = _rope(pd[:, DIFF_W:2 * DIFF_W], cd, sd, DIFF_DH // 2)
    dqk_ref[:, 0:DIFF_W] = dq.astype(dqk_ref.dtype)
    dqk_ref[:, DIFF_W:2 * DIFF_W] = dk.astype(dqk_ref.dtype)
    vt = pd[:, 2 * DIFF_W:3 * DIFF_W].T
    ones_row = jnp.where(lax.broadcasted_iota(jnp.int32, (DIFF_VROWS - DIFF_DV, TOK), 0) == 0, 1.0, 0.0)
    for hh in range(DIFF_HEADS):
        dvt_ref[hh, 0:DIFF_DV, :] = vt[hh * DIFF_DV:(hh + 1) * DIFF_DV, :].astype(dvt_ref.dtype)
        dvt_ref[hh, DIFF_DV:DIFF_VROWS, :] = ones_row.astype(dvt_ref.dtype)

    pg = project(C_GQ, C_GO + GLA_W)

    cs, ss = cs_ref[...], ss_ref[...]
    sq = _rope(ps[:, 0:SWA_Q_W], cs, ss, SWA_DH // 2) * (SWA_DH ** -0.5 * LOG2_E)
    sq_ref[...] = sq.astype(sq_ref.dtype)
    sk = _rope(ps[:, SWA_Q_W:SWA_Q_W + SWA_KV_W], cs, ss, SWA_DH // 2)
    lane = lax.broadcasted_iota(jnp.int32, (1, LANES), 1)
    low = lane < SWA_DH
    sw = pltpu.roll(sk, SWA_DH, axis=1)
    sk_ref[:, 0:LANES] = jnp.where(low, sk, sw).astype(sk_ref.dtype)
    sk_ref[:, LANES:2 * LANES] = jnp.where(low, sw, sk).astype(sk_ref.dtype)
    svt = ps[:, SWA_Q_W + SWA_KV_W:SWA_Q_W + 2 * SWA_KV_W].T
    ones_s = jnp.where(lax.broadcasted_iota(jnp.int32, (SWA_VROWS - SWA_DH, SWA_TQ), 0) == 0, 1.0, 0.0)
    for grp in range(SWA_KV_HEADS):
        for i in range(TOK // SWA_TQ):
            svt_ref[grp, i, 0:SWA_DH, :] = svt[grp * SWA_DH:(grp + 1) * SWA_DH,
                                               i * SWA_TQ:(i + 1) * SWA_TQ].astype(svt_ref.dtype)
            svt_ref[grp, i, SWA_DH:SWA_VROWS, :] = ones_s.astype(svt_ref.dtype)

    gla_ref[:, 0:GLA_W] = pg[:, 0:GLA_W] * (GLA_DK ** -0.5)
    gla_ref[:, GLA_W:4 * GLA_W] = pg[:, GLA_W:4 * GLA_W]


def _mod_index(b, j, *, layer, ctx_row):
    return (layer, jnp.where(j == 0, ctx_row, b), 0, 0)


def _inproj_call(h, mods, layer, ctx_row, w_in_p, wg, bg, tabs):
    bsz, s, d = h.shape
    nt = s // TOK
    act = MXU_DTYPE
    tab_spec = pl.BlockSpec((TOK, LANES), lambda b, j: (j, 0))
    return pl.pallas_call(
        _inproj_kernel,
        grid=(bsz, nt),
        in_specs=[
            pl.BlockSpec((None, TOK, d), lambda b, j: (b, j, 0)),
            pl.BlockSpec((None, None, 6, d), functools.partial(_mod_index, layer=layer, ctx_row=ctx_row)),
            pl.BlockSpec((d, IN_PAD_W), lambda b, j: (0, 0)),
            pl.BlockSpec((LANES, 2 * GLA_W), lambda b, j: (0, 0)),
            pl.BlockSpec((1, 2 * GLA_W), lambda b, j: (0, 0)),
            tab_spec, tab_spec, tab_spec, tab_spec,
        ],
        out_specs=[
            pl.BlockSpec((None, TOK, 4 * GLA_W), lambda b, j: (b, j, 0)),
            pl.BlockSpec((None, TOK, 2 * GLA_W), lambda b, j: (b, j, 0)),
            pl.BlockSpec((None, TOK, 2 * DIFF_W), lambda b, j: (b, j, 0)),
            pl.BlockSpec((None, DIFF_HEADS, DIFF_VROWS, TOK), lambda b, j: (b, 0, 0, j)),
            pl.BlockSpec((None, TOK, SWA_Q_W), lambda b, j: (b, j, 0)),
            pl.BlockSpec((None, TOK, 2 * LANES), lambda b, j: (b, j, 0)),
            pl.BlockSpec((None, SWA_KV_HEADS, TOK // SWA_TQ, SWA_VROWS, SWA_TQ), lambda b, j: (b, 0, j, 0, 0)),
        ],
        out_shape=[
            jax.ShapeDtypeStruct((bsz, s, 4 * GLA_W), F32),
            jax.ShapeDtypeStruct((bsz, s, 2 * GLA_W), F32),
            jax.ShapeDtypeStruct((bsz, s, 2 * DIFF_W), act),
            jax.ShapeDtypeStruct((bsz, DIFF_HEADS, DIFF_VROWS, s), act),
            jax.ShapeDtypeStruct((bsz, s, SWA_Q_W), act),
            jax.ShapeDtypeStruct((bsz, s, 2 * LANES), act),
            jax.ShapeDtypeStruct((bsz, SWA_KV_HEADS, s // SWA_TQ, SWA_VROWS, SWA_TQ), act),
        ],
        compiler_params=pltpu.CompilerParams(
            dimension_semantics=("arbitrary", "arbitrary"), vmem_limit_bytes=VMEM_LIMIT),
        name="in_projection",
    )(h, mods, w_in_p, wg, bg, *tabs)


def _gla_direction(gla_ref, gate_ref, gate_col, o_ref, st_ref, reverse, consts):
    tri, head_lane, blockdiag, sub_masks = consts
    c_, sub = GLA_CHUNK, GLA_SUB
    nsub = c_ // sub
    nchunk = TOK // c_
    g_all = gate_ref[:, gate_col:gate_col + GLA_W]
    b_all = _split_mm_t(tri, g_all)
    order = range(nchunk - 1, -1, -1) if reverse else range(nchunk)
    zero = jnp.zeros((), F32)
    chunks = []
    for c in order:
        r0 = c * c_
        q = gla_ref[r0:r0 + c_, 0:GLA_W]
        k = gla_ref[r0:r0 + c_, GLA_W:2 * GLA_W]
        v = gla_ref[r0:r0 + c_, 2 * GLA_W:3 * GLA_W]
        b = b_all[r0:r0 + c_, :]
        b_end = b[0:1, :] if reverse else b[c_ - 1:c_, :]
        atts = []
        for i in range(nsub):
            t0, t1 = i * sub, (i + 1) * sub
            if reverse:
                ref_b = b[t1:t1 + 1, :] if i < nsub - 1 else jnp.zeros((1, GLA_W), F32)
                k0, k1 = t0, c_
            else:
                ref_b = b[t0 - 1:t0, :] if i > 0 else jnp.zeros((1, GLA_W), F32)
                k0, k1 = 0, t1
            qd = q[t0:t1, :] * jnp.exp(b[t0:t1, :] - ref_b)
            lhs = jnp.concatenate([jnp.where(head_lane[hh], qd, zero) for hh in range(GLA_HEADS)], axis=0)
            kk = k[k0:k1, :] * jnp.exp(ref_b - b[k0:k1, :])
            atts.append((_mm_nt(lhs, kk), k0, k1))
        upd = _mm_tn(v, k * jnp.exp(b_end - b))
        chunks.append(dict(r0=r0, v=v, atts=atts, upd=upd, qs=q * jnp.exp(b), decay=jnp.exp(b_end)))
    for ch in chunks:
        o_parts = []
        for i, (att, k0, k1) in enumerate(ch["atts"]):
            res = _mm(jnp.where(sub_masks[(reverse, i)], att, zero), ch["v"][k0:k1, :])
            o_i = res[(GLA_HEADS - 1) * sub:GLA_HEADS * sub, :]
            for hh in range(GLA_HEADS - 2, -1, -1):
                o_i = jnp.where(head_lane[hh], res[hh * sub:(hh + 1) * sub, :], o_i)
            o_parts.append(o_i)
        ch["o_intra"] = jnp.concatenate(o_parts, axis=0)
    st = st_ref[...]
    for ch in chunks:
        o_ref[ch["r0"]:ch["r0"] + c_, :] = ch["o_intra"] + _mm_nt(ch["qs"], st)
        st = st * ch["decay"] + jnp.where(blockdiag, ch["upd"], zero)
    st_ref[...] = st


def _gla_consts():
    c_, sub = GLA_CHUNK, GLA_SUB
    nsub = c_ // sub
    r = lax.broadcasted_iota(jnp.int32, (TOK, TOK), 0)
    cc = lax.broadcasted_iota(jnp.int32, (TOK, TOK), 1)
    same_chunk = (r // c_) == (cc // c_)
    tri_f = jnp.where(same_chunk & (cc <= r), 1.0, 0.0).astype(F32)
    tri_b = jnp.where(same_chunk & (cc >= r), 1.0, 0.0).astype(F32)
    lane = lax.broadcasted_iota(jnp.int32, (1, GLA_W), 1)
    head_lane = [(lane // GLA_DK) == hh for hh in range(GLA_HEADS)]
    blockdiag = (r // GLA_DV) == (cc // GLA_DK)
    sub_masks = {}
    for reverse in (False, True):
        for i in range(nsub):
            nk = (nsub - i) * sub if reverse else (i + 1) * sub
            rr = lax.broadcasted_iota(jnp.int32, (GLA_HEADS * sub, nk), 0) % sub
            kc = lax.broadcasted_iota(jnp.int32, (GLA_HEADS * sub, nk), 1)
            if reverse:
                sub_masks[(reverse, i)] = kc >= rr
            else:
                sub_masks[(reverse, i)] = (kc - (nk - sub)) <= rr
    return tri_f, tri_b, head_lane, blockdiag, sub_masks


def _gla_kernel(gla_f_ref, gate_f_ref, gla_b_ref, gate_b_ref, of_ref, ob_ref, stf_ref, stb_ref):
    @pl.when(pl.program_id(1) == 0)
    def _():
        stf_ref[...] = jnp.zeros_like(stf_ref)
        stb_ref[...] = jnp.zeros_like(stb_ref)

    tri_f, tri_b, head_lane, blockdiag, sub_masks = _gla_consts()
    _gla_direction(gla_f_ref, gate_f_ref, 0, of_ref, stf_ref, False, (tri_f, head_lane, blockdiag, sub_masks))
    _gla_direction(gla_b_ref, gate_b_ref, GLA_W, ob_ref, stb_ref, True, (tri_b, head_lane, blockdiag, sub_masks))


def _gla_call(gla_in, gates):
    bsz, s, _ = gla_in.shape
    nt = s // TOK

    def fwd(b, j):
        return (b, j, 0)

    def bwd(b, j):
        return (b, jnp.where(j == 0, 0, nt - j), 0)

    return pl.pallas_call(
        _gla_kernel,
        grid=(bsz, nt),
        in_specs=[
            pl.BlockSpec((None, TOK, 4 * GLA_W), fwd),
            pl.BlockSpec((None, TOK, 2 * GLA_W), fwd),
            pl.BlockSpec((None, TOK, 4 * GLA_W), bwd),
            pl.BlockSpec((None, TOK, 2 * GLA_W), bwd),
        ],
        out_specs=[
            pl.BlockSpec((None, TOK, GLA_W), fwd),
            pl.BlockSpec((None, TOK, GLA_W), bwd),
        ],
        out_shape=[jax.ShapeDtypeStruct((bsz, s, GLA_W), F32)] * 2,
        scratch_shapes=[pltpu.VMEM((GLA_W, GLA_W), F32), pltpu.VMEM((GLA_W, GLA_W), F32)],
        compiler_params=pltpu.CompilerParams(
            dimension_semantics=("arbitrary", "arbitrary"), vmem_limit_bytes=VMEM_LIMIT),
        name="gla_scan",
    )(gla_in, gates, gla_in, gates)


def _diff_kernel(q_ref, k_ref, vt_ref, lam_ref, lam0_ref, g_ref, o_ref, *, n_ctx):
    lam_p = lam_ref[...]
    lam = (jnp.exp(jnp.sum(lam_p[0:1, :] * lam_p[1:2, :], axis=-1, keepdims=True))
           - jnp.exp(jnp.sum(lam_p[2:3, :] * lam_p[3:4, :], axis=-1, keepdims=True))
           + lam0_ref[...])
    lane = lax.broadcasted_iota(jnp.int32, (1, DIFF_W), 1)

    def attend(nk, parts):
        q = q_ref[...]
        step = nk // parts

        def scores(idx):
            lo = idx * DIFF_DH
            qm = jnp.where((lane >= lo) & (lane < lo + DIFF_DH), q, jnp.zeros_like(q))
            return [_mm_nt(k_ref[i * step:(i + 1) * step, :], qm).astype(MXU_DTYPE)
                    for i in range(parts)]

        def probs(sts):
            m = _col_max(sts[0])
            for st in sts[1:]:
                m = jnp.maximum(m, _col_max(st))
            return [jnp.exp2(st - m) for st in sts]

        def values(idx, ps):
            acc = None
            for i, p in enumerate(ps):
                pv = _mm(vt_ref[idx // 2, :, i * step:(i + 1) * step], p)
                acc = pv if acc is None else acc + pv
            return acc[0:DIFF_DV, :] / acc[DIFF_DV:DIFF_DV + 1, :]

        n_str = 2 * DIFF_HEADS
        sts = {0: scores(0), 1: scores(1)}
        ps = {0: probs(sts.pop(0))}
        comps = []
        for idx in range(n_str):
            if idx + 2 < n_str:
                sts[idx + 2] = scores(idx + 2)
            if idx + 1 < n_str:
                ps[idx + 1] = probs(sts.pop(idx + 1))
            comps.append(values(idx, ps.pop(idx)))
        heads = []
        for hh in range(DIFF_HEADS):
            o_h = comps[2 * hh] - lam * comps[2 * hh + 1]
            ms = jnp.mean(o_h * o_h, axis=0, keepdims=True)
            heads.append(o_h * lax.rsqrt(ms + LN_EPS))
        out = jnp.concatenate(heads, axis=0).T
        o_ref[...] = (out * g_ref[...]).astype(o_ref.dtype)

    j = pl.program_id(1)

    @pl.when(j == 0)
    def _():
        attend(n_ctx, 1)

    @pl.when(j > 0)
    def _():
        attend(k_ref.shape[0], 2)


def _col_max(x):
    r = x.shape[0]
    slab = 16
    while r % (2 * slab) == 0 and r // slab > 32:
        slab *= 2
    acc = x[0:slab, :]
    for i in range(1, r // slab):
        acc = jnp.maximum(acc, x[i * slab:(i + 1) * slab, :])
    return jnp.max(acc, axis=0, keepdims=True)


def _diff_call(dqk, dvt, lam_p, lam0, g_eff):
    bsz, s, _ = dqk.shape
    nt = s // TOK
    return pl.pallas_call(
        functools.partial(_diff_kernel, n_ctx=TOK),
        grid=(bsz, nt),
        in_specs=[
            pl.BlockSpec((None, TOK, DIFF_W), lambda b, j: (b, j, 0)),
            pl.BlockSpec((None, s, DIFF_W), lambda b, j: (b, 0, 1)),
            pl.BlockSpec((None, DIFF_HEADS, DIFF_VROWS, s), lambda b, j: (b, 0, 0, 0)),
            pl.BlockSpec((4, DIFF_DH), lambda b, j: (0, 0)),
            pl.BlockSpec((1, 1), lambda b, j: (0, 0)),
            pl.BlockSpec((1, DIFF_W), lambda b, j: (0, 0)),
        ],
        out_specs=pl.BlockSpec((None, TOK, DIFF_W), lambda b, j: (b, j, 0)),
        out_shape=jax.ShapeDtypeStruct((bsz, s, DIFF_W), MXU_DTYPE),
        compiler_params=pltpu.CompilerParams(
            dimension_semantics=("arbitrary", "arbitrary"), vmem_limit_bytes=VMEM_LIMIT),
        name="diff_attention",
    )(dqk, dqk, dvt, lam_p, lam0, g_eff)


def _swa_kernel(q_ref, k_ref, vt_ref, sink_ref, o_ref, *, n_ctx, n_lat):
    j = pl.program_id(1)
    tq = SWA_TQ
    n_win = 3
    n_ctx_t = n_ctx // tq
    s_rows = n_ctx + n_lat
    lane = lax.broadcasted_iota(jnp.int32, (1, LANES), 1)
    low = lane < SWA_DH
    q_per_kv = SWA_HEADS // SWA_KV_HEADS
    chains = [(sub, grp) for sub in range(TOK // tq) for grp in range(SWA_KV_HEADS)]

    def window(sub):
        p0 = j * TOK + sub * tq - n_ctx
        start = pl.multiple_of(jnp.clip(n_ctx + p0 - tq, 0, s_rows - n_win * tq), tq)
        return p0, start

    def scores(idx):
        sub, grp = chains[idx]
        _, start = window(sub)
        blocks = []
        for r in range(q_per_kv):
            hh = grp * q_per_kv + r
            qp = q_ref[sub * tq:(sub + 1) * tq, (hh // 2) * LANES:(hh // 2 + 1) * LANES]
            blocks.append(jnp.where(low if hh % 2 == 0 else ~low, qp, jnp.zeros_like(qp)))
        lhs = jnp.concatenate(blocks, axis=0)
        st_c = _mm_nt(k_ref[0:n_ctx, grp * LANES:(grp + 1) * LANES], lhs)
        st_l = _mm_nt(k_ref[pl.ds(start, n_win * tq), grp * LANES:(grp + 1) * LANES], lhs)
        return st_c, st_l

    nxt = scores(0)
    for idx, (sub, grp) in enumerate(chains):
        st_c, st_l = nxt
        if idx + 1 < len(chains):
            nxt = scores(idx + 1)
        p0, start = window(sub)
        pos_q = p0 + lax.broadcasted_iota(jnp.int32, (1, tq), 1)
        pos_k = start - n_ctx + lax.broadcasted_iota(jnp.int32, (n_win * tq, 1), 0)
        valid = (jnp.abs(pos_k - pos_q) <= WINDOW) & (pos_k >= 0) & (p0 >= 0)
        bias = jnp.where(valid, 0.0, -jnp.inf).astype(F32)
        st_l = st_l + jnp.concatenate([bias] * q_per_kv, axis=1)
        st_c = st_c.astype(MXU_DTYPE)
        st_l = st_l.astype(MXU_DTYPE)
        snk = jnp.concatenate([sink_ref[grp * q_per_kv + r:grp * q_per_kv + r + 1, :]
                               for r in range(q_per_kv)], axis=1)
        m = jnp.maximum(jnp.maximum(_col_max(st_c), _col_max(st_l)).astype(F32), snk)
        mb = m.astype(MXU_DTYPE)
        p_c = jnp.exp2(st_c - mb)
        p_l = jnp.exp2(st_l - mb)
        t0 = start // tq
        acc = None
        for i in range(n_ctx_t):
            pv = _mm(vt_ref[grp, i], p_c[i * tq:(i + 1) * tq, :])
            acc = pv if acc is None else acc + pv
        for i in range(n_win):
            acc = acc + _mm(vt_ref[grp, t0 + i], p_l[i * tq:(i + 1) * tq, :])
        l = acc[SWA_DH:SWA_DH + 1, :] + jnp.exp2(snk - mb.astype(F32))
        o = acc[0:SWA_DH, :] / l
        for pr in range(q_per_kv // 2):
            pair = grp * (q_per_kv // 2) + pr
            both = jnp.concatenate([o[:, (2 * pr) * tq:(2 * pr + 1) * tq],
                                    o[:, (2 * pr + 1) * tq:(2 * pr + 2) * tq]], axis=0)
            o_ref[sub * tq:(sub + 1) * tq, pair * LANES:(pair + 1) * LANES] = both.T.astype(o_ref.dtype)


def _swa_call(sq, sk, svt, sink_tile, n_ctx):
    bsz, s, _ = sq.shape
    return pl.pallas_call(
        functools.partial(_swa_kernel, n_ctx=n_ctx, n_lat=s - n_ctx),
        grid=(bsz, s // TOK),
        in_specs=[
            pl.BlockSpec((None, TOK, SWA_Q_W), lambda b, j: (b, j, 0)),
            pl.BlockSpec((None, s, 2 * LANES), lambda b, j: (b, 0, 0)),
            pl.BlockSpec((None, SWA_KV_HEADS, s // SWA_TQ, SWA_VROWS, SWA_TQ), lambda b, j: (b, 0, 0, 0, 0)),
            pl.BlockSpec((SWA_HEADS, LANES), lambda b, j: (0, 0)),
        ],
        out_specs=pl.BlockSpec((None, TOK, SWA_Q_W), lambda b, j: (b, j, 0)),
        out_shape=jax.ShapeDtypeStruct((bsz, s, SWA_Q_W), MXU_DTYPE),
        compiler_params=pltpu.CompilerParams(
            dimension_semantics=("arbitrary", "arbitrary"), vmem_limit_bytes=VMEM_LIMIT),
        name="windowed_gqa",
    )(sq, sk, svt, sink_tile)


def _outproj_kernel(of_ref, ob_ref, og_ref, diff_ref, swa_ref, h_ref, mod_ref, w_ref,
                    gng_ref, lng_ref, lnb_ref, o_ref, *, alpha):
    o = of_ref[...] + ob_ref[...]
    ms = _split_mm(o * o, _group_mean_matrix(GLA_W, GLA_DV))
    gla = o * lax.rsqrt(ms + LN_EPS) * gng_ref[...] * _silu(og_ref[...])
    y = (_mm(gla, w_ref[0:GLA_W, :])
         + _mm(diff_ref[...], w_ref[GLA_W:GLA_W + DIFF_W, :])
         + _mm(swa_ref[...], w_ref[GLA_W + DIFF_W:, :]))
    z = alpha * h_ref[...] + mod_ref[2:3, :] * y
    o_ref[...] = _ln_plain(z) * lng_ref[...] + lnb_ref[...]


def _outproj_call(o_f, o_b, gla_in, diff_o, swa_o, h, mods, layer, ctx_row, w_out, gng, lng, lnb, alpha):
    bsz, s, d = h.shape
    nt = s // TOK
    tile = lambda w: pl.BlockSpec((None, TOK, w), lambda b, j: (b, j, 0))
    const = lambda shape: pl.BlockSpec(shape, lambda b, j: (0,) * len(shape))
    return pl.pallas_call(
        functools.partial(_outproj_kernel, alpha=alpha),
        grid=(bsz, nt),
        in_specs=[
            tile(GLA_W), tile(GLA_W),
            pl.BlockSpec((None, TOK, GLA_W), lambda b, j: (b, j, 3)),
            tile(DIFF_W), tile(SWA_Q_W), tile(d),
            pl.BlockSpec((None, None, 6, d), functools.partial(_mod_index, layer=layer, ctx_row=ctx_row)),
            const((d, d)), const((1, GLA_W)), const((1, d)), const((1, d)),
        ],
        out_specs=tile(d),
        out_shape=jax.ShapeDtypeStruct((bsz, s, d), F32),
        compiler_params=pltpu.CompilerParams(
            dimension_semantics=("arbitrary", "arbitrary"), vmem_limit_bytes=VMEM_LIMIT),
        name="out_projection",
    )(o_f, o_b, gla_in, diff_o, swa_o, h, mods, w_out, gng, lng, lnb)


def _ffn_kernel(h_ref, mod_ref, wg_ref, wu_ref, wd_ref, lng_ref, lnb_ref, o_ref, *, alpha, skip_first):
    def body():
        x = h_ref[...]
        u = (_ln_plain(x) * (1.0 + mod_ref[4:5, :]) + mod_ref[3:4, :]).astype(MXU_DTYPE)
        a = _silu(_mm(u, wg_ref[...])) * _mm(u, wu_ref[...])
        f = _mm(a, wd_ref[...])
        z = alpha * x + mod_ref[5:6, :] * f
        o_ref[...] = _ln_plain(z) * lng_ref[...] + lnb_ref[...]

    if skip_first:
        pl.when(pl.program_id(1) > 0)(body)
    else:
        body()


def _ffn_call(h, mods, layer, ctx_row, w_g, w_u, w_d, lng, lnb, alpha, latent_only):
    bsz, s, d = h.shape
    nt = s // TOK
    hid = w_g.shape[1]
    single = pl.Buffered(1)
    const = lambda shape: pl.BlockSpec(shape, lambda b, j: (0,) * len(shape), pipeline_mode=single)
    if latent_only:
        out_spec = pl.BlockSpec((None, TOK, d), lambda b, j: (b, jnp.maximum(j - 1, 0), 0))
        out_rows = s - TOK
    else:
        out_spec = pl.BlockSpec((None, TOK, d), lambda b, j: (b, j, 0))
        out_rows = s
    return pl.pallas_call(
        functools.partial(_ffn_kernel, alpha=alpha, skip_first=latent_only),
        grid=(bsz, nt),
        in_specs=[
            pl.BlockSpec((None, TOK, d), lambda b, j: (b, j, 0)),
            pl.BlockSpec((None, None, 6, d), functools.partial(_mod_index, layer=layer, ctx_row=ctx_row)),
            const((d, hid)), const((d, hid)), const((hid, d)),
            pl.BlockSpec((1, d), lambda b, j: (0, 0)), pl.BlockSpec((1, d), lambda b, j: (0, 0)),
        ],
        out_specs=out_spec,
        out_shape=jax.ShapeDtypeStruct((bsz, out_rows, d), F32),
        compiler_params=pltpu.CompilerParams(
            dimension_semantics=("arbitrary", "arbitrary"), vmem_limit_bytes=VMEM_LIMIT),
        name="swiglu_ffn",
    )(h, mods, w_g, w_u, w_d, lng, lnb)


def _rope_tables(rows, n_ctx, dim):
    row = jnp.repeat(jnp.arange(rows, dtype=F32), GRID_W)
    col = jnp.tile(jnp.arange(GRID_W, dtype=F32), rows)
    n_freq = dim // 4
    inv = jnp.power(ROPE_BASE, -jnp.arange(n_freq, dtype=F32) / n_freq)
    ang = jnp.concatenate([row[:, None] * inv, col[:, None] * inv], axis=-1)
    cos, sin = jnp.cos(ang), jnp.sin(ang)
    reps = LANES // dim
    cos_t = jnp.tile(jnp.concatenate([cos, cos], axis=-1), (1, reps))
    sin_t = jnp.tile(jnp.concatenate([-sin, sin], axis=-1), (1, reps))
    cos_t = jnp.concatenate([jnp.ones((n_ctx, LANES), F32), cos_t], axis=0)
    sin_t = jnp.concatenate([jnp.zeros((n_ctx, LANES), F32), sin_t], axis=0)
    return cos_t, sin_t


def _reorder_w_in(w_in):
    o = 0
    parts = {}
    for name, width in (("gq", GLA_W), ("gk", GLA_W), ("gv", GLA_W), ("go", GLA_W),
                        ("zf", GLA_GATE_RANK), ("zb", GLA_GATE_RANK),
                        ("dq", DIFF_W), ("dk", DIFF_W), ("dv", DIFF_W),
                        ("sq", SWA_Q_W), ("sk", SWA_KV_W), ("sv", SWA_KV_W)):
        parts[name] = w_in[:, o:o + width]
        o += width
    cols = [parts[n] for n in ("gq", "gk", "gv", "go", "dq", "dk", "dv", "sq", "sk", "sv", "zf", "zb")]
    pad = IN_PAD_W - sum(c.shape[1] for c in cols)
    cols.append(jnp.zeros((w_in.shape[0], pad), w_in.dtype))
    return jnp.concatenate(cols, axis=1)


def kernel(x, c, ctx, c_ctx, w_ada, b_ada, w_in, w_gla_gate, b_gla_gate, gla_norm_g, diff_lambda,
           diff_norm_g, swa_sink, w_out, ln_g, ln_b, w_ffn_gate, w_ffn_up, w_ffn_down):
    bsz, n_lat, d = x.shape
    n_ctx = ctx.shape[1]
    depth = w_ada.shape[0]
    assert n_ctx == TOK and n_lat % TOK == 0 and n_lat % GRID_W == 0
    alpha = (2.0 * depth) ** 0.25
    wdt = MXU_DTYPE

    cond_rows = -(-(bsz + 1) // 8) * 8
    cond =jnp.concatenate([c, c_ctx[None, :], jnp.zeros((cond_rows - bsz - 1, d), F32)], axis=0)
    mods = _mods_call(cond, w_ada, b_ada).reshape(depth, cond_rows, 6, d)
    ctx_row = bsz

    rows = n_lat // GRID_W
    tabs = _rope_tables(rows, n_ctx, DIFF_DH) + _rope_tables(rows, n_ctx, SWA_DH)

    h = jnp.concatenate([ctx, x], axis=1)
    for layer in range(depth):
        last = layer == depth - 1
        lam_init = 0.8 - 0.6 * math.exp(-0.3 * layer)
        w_in_p = _reorder_w_in(w_in[layer]).astype(wdt)
        wg = jnp.zeros((LANES, 2 * GLA_W), F32)
        wg = wg.at[0:GLA_GATE_RANK, 0:GLA_W].set(w_gla_gate[layer, 0])
        wg = wg.at[GLA_GATE_RANK:2 * GLA_GATE_RANK, GLA_W:].set(w_gla_gate[layer, 1])
        bg = b_gla_gate[layer].reshape(1, 2 * GLA_W)

        gla_in, gates, dqk, dvt, sq, sk, svt = _inproj_call(h, mods, layer, ctx_row, w_in_p, wg.astype(wdt), bg, tabs)
        o_f, o_b = _gla_call(gla_in, gates)
        diff_o = _diff_call(dqk, dvt, diff_lambda[layer], jnp.full((1, 1), lam_init, F32),
                            (jnp.tile(diff_norm_g[layer], DIFF_HEADS) * (1.0 - lam_init)).reshape(1, DIFF_W))
        swa_o = _swa_call(sq, sk, svt,
                          jnp.broadcast_to(swa_sink[layer][:, None] * LOG2_E, (SWA_HEADS, LANES)), n_ctx)
        h_mid = _outproj_call(o_f, o_b, gla_in, diff_o, swa_o, h, mods, layer, ctx_row,
                              w_out[layer].astype(wdt),
                              jnp.tile(gla_norm_g[layer], GLA_HEADS).reshape(1, GLA_W),
                              ln_g[layer, 0].reshape(1, d), ln_b[layer, 0].reshape(1, d), alpha)
        h = _ffn_call(h_mid, mods, layer, ctx_row, w_ffn_gate[layer].astype(wdt),
                      w_ffn_up[layer].astype(wdt), w_ffn_down[layer].astype(wdt),
                      ln_g[layer, 1].reshape(1, d), ln_b[layer, 1].reshape(1, d), alpha, last)
    return h
```

```python
import functools
import math

import jax
import jax.numpy as jnp
from jax import lax
from jax.experimental import pallas as pl
from jax.experimental.pallas import tpu as pltpu

F32 = jnp.float32
MXU_DTYPE = jnp.bfloat16

GRID_W = 64
GLA_HEADS, GLA_DK, GLA_DV = 4, 64, 64
GLA_GATE_RANK = 16
GLA_GATE_NORM = 16.0
GLA_CHUNK = 64
GLA_SUB = 16
DIFF_HEADS, DIFF_DH = 4, 32
DIFF_DV = 2 * DIFF_DH
SWA_HEADS, SWA_KV_HEADS, SWA_DH = 8, 2, 64
WINDOW = 128
ROPE_BASE = 10000.0
LN_EPS = 1e-6

GLA_W = GLA_HEADS * GLA_DK
DIFF_W = DIFF_HEADS * 2 * DIFF_DH
SWA_Q_W = SWA_HEADS * SWA_DH
SWA_KV_W = SWA_KV_HEADS * SWA_DH
LANES = 128
TOK = 256
SWA_TQ = 128
DIFF_VROWS = DIFF_DV + 16
SWA_VROWS = SWA_DH + 16
LOG2_E = math.log2(math.e)
VMEM_LIMIT = 52 * 1024 * 1024
POST_VMEM_LIMIT = 58 * 1024 * 1024


def _mm(a, b):
    return jnp.dot(a.astype(MXU_DTYPE), b.astype(MXU_DTYPE), preferred_element_type=F32)


def _mm_nt(a, b):
    return lax.dot_general(a.astype(MXU_DTYPE), b.astype(MXU_DTYPE),
                           (((1,), (1,)), ((), ())), preferred_element_type=F32)


def _mm_tn(a, b):
    return lax.dot_general(a.astype(MXU_DTYPE), b.astype(MXU_DTYPE),
                           (((0,), (0,)), ((), ())), preferred_element_type=F32)


def _split_mm(a, b_exact):
    if MXU_DTYPE == F32:
        return jnp.dot(a, b_exact.astype(F32), preferred_element_type=F32)
    hi = a.astype(jnp.bfloat16)
    r1 = a - hi.astype(F32)
    mid = r1.astype(jnp.bfloat16)
    lo = (r1 - mid.astype(F32)).astype(jnp.bfloat16)
    b = b_exact.astype(jnp.bfloat16)
    return (jnp.dot(hi, b, preferred_element_type=F32)
            + jnp.dot(mid, b, preferred_element_type=F32)
            + jnp.dot(lo, b, preferred_element_type=F32))


def _split_mm_t(b_exact, a):
    if MXU_DTYPE == F32:
        return jnp.dot(b_exact.astype(F32), a, preferred_element_type=F32)
    hi = a.astype(jnp.bfloat16)
    r1 = a - hi.astype(F32)
    mid = r1.astype(jnp.bfloat16)
    lo = (r1 - mid.astype(F32)).astype(jnp.bfloat16)
    b = b_exact.astype(jnp.bfloat16)
    return (jnp.dot(b, hi, preferred_element_type=F32)
            + jnp.dot(b, mid, preferred_element_type=F32)
            + jnp.dot(b, lo, preferred_element_type=F32))


def _ln_plain(x):
    mu = jnp.mean(x, axis=-1, keepdims=True)
    xc = x - mu
    var = jnp.mean(xc * xc, axis=-1, keepdims=True)
    return xc * lax.rsqrt(var + LN_EPS)


def _silu(x):
    return x / (1.0 + jnp.exp(-x))


def _group_mean_matrix(width, group):
    r = lax.broadcasted_iota(jnp.int32, (width, width), 0) // group
    c = lax.broadcasted_iota(jnp.int32, (width, width), 1) // group
    return jnp.where(r == c, 1.0 / group, 0.0).astype(F32)


def _mods_kernel(a_ref, w_ref, b_ref, o_ref):
    a = a_ref[...]
    o_ref[...] = jnp.dot(_silu(a), w_ref[...], preferred_element_type=F32,
                         precision=lax.Precision.HIGHEST) + b_ref[...]


def _mods_call(cond, w_ada, b_ada):
    depth, d, width = w_ada.shape
    rows = cond.shape[0]
    bn = 1536
    return pl.pallas_call(
        _mods_kernel,
        grid=(depth, width // bn),
        in_specs=[
            pl.BlockSpec((rows, d), lambda l, n: (0, 0)),
            pl.BlockSpec((None, d, bn), lambda l, n: (l, 0, n)),
            pl.BlockSpec((None, 1, bn), lambda l, n: (l, 0, n)),
        ],
        out_specs=pl.BlockSpec((None, rows, bn), lambda l, n: (l, 0, n)),
        out_shape=jax.ShapeDtypeStruct((depth, rows, width), F32),
        compiler_params=pltpu.CompilerParams(vmem_limit_bytes=VMEM_LIMIT),
        name="adaln_mods",
    )(cond, w_ada, b_ada.reshape(depth, 1, width))


C_GQ, C_GK, C_GV, C_GO = 0, 256, 512, 768
C_DQ, C_DK, C_DV = 1024, 1280, 1536
C_SQ, C_SK, C_SV = 1792, 2304, 2432
C_Z = 2560
IN_PAD_W = 2688


def _rope(x, cos, sin_signed, half):
    outs = []
    lane = lax.broadcasted_iota(jnp.int32, (1, LANES), 1)
    first = (lane % (2 * half)) < half
    for s in range(x.shape[1] // LANES):
        xs = x[:, s * LANES:(s + 1) * LANES]
        up = pltpu.roll(xs, LANES - half, axis=1)
        dn = pltpu.roll(xs, half, axis=1)
        outs.append(xs * cos + jnp.where(first, up, dn) * sin_signed)
    return outs[0] if len(outs) == 1 else jnp.concatenate(outs, axis=1)


def _inproj_kernel(h_ref, mod_ref, w_ref, wg_ref, bg_ref, cd_ref, sd_ref, cs_ref, ss_ref,
                   gla_ref, gate_ref, dqk_ref, dvt_ref, sq_ref, sk_ref, svt_ref):
    x = h_ref[...]
    u = _ln_plain(x) * (1.0 + mod_ref[1:2, :]) + mod_ref[0:1, :]
    ub = u.astype(MXU_DTYPE)

    def project(c0, c1):
        return jnp.dot(ub, w_ref[:, c0:c1], preferred_element_type=F32)

    z = project(C_Z, C_Z + LANES)
    pd = project(C_DQ, C_DV + DIFF_W)

    gpre = _mm(z, wg_ref[...]) + bg_ref[...]
    logsig = jnp.minimum(gpre, 0.0) - jnp.log(1.0 + jnp.exp(-jnp.abs(gpre)))
    gate_ref[...] = logsig * (1.0 / GLA_GATE_NORM)

    ps = project(C_SQ, C_SV + SWA_KV_W)

    cd, sd = cd_ref[...], sd_ref[...]
    dq = _rope(pd[:, 0:DIFF_W], cd, sd, DIFF_DH // 2) * (DIFF_DH ** -0.5 * LOG2_E)
    dk = _rope(pd[:, DIFF_W:2 * DIFF_W], cd, sd, DIFF_DH // 2)
    dqk_ref[:, 0:DIFF_W] = dq.astype(dqk_ref.dtype)
    dqk_ref[:, DIFF_W:2 * DIFF_W] = dk.astype(dqk_ref.dtype)
    vt = pd[:, 2 * DIFF_W:3 * DIFF_W].T
    ones_row = jnp.where(lax.broadcasted_iota(jnp.int32, (DIFF_VROWS - DIFF_DV, TOK), 0) == 0, 1.0, 0.0)
    for hh in range(DIFF_HEADS):
        dvt_ref[hh, 0:DIFF_DV, :] = vt[hh * DIFF_DV:(hh + 1) * DIFF_DV, :].astype(dvt_ref.dtype)
        dvt_ref[hh, DIFF_DV:DIFF_VROWS, :] = ones_row.astype(dvt_ref.dtype)

    pg = project(C_GQ, C_GO + GLA_W)

    cs, ss = cs_ref[...], ss_ref[...]
    sq = _rope(ps[:, 0:SWA_Q_W], cs, ss, SWA_DH // 2) * (SWA_DH ** -0.5 * LOG2_E)
    sq_ref[...] = sq.astype(sq_ref.dtype)
    sk = _rope(ps[:, SWA_Q_W:SWA_Q_W + SWA_KV_W], cs, ss, SWA_DH // 2)
    lane = lax.broadcasted_iota(jnp.int32, (1, LANES), 1)
    low = lane < SWA_DH
    sw = pltpu.roll(sk, SWA_DH, axis=1)
    sk_ref[:, 0:LANES] = jnp.where(low, sk, sw).astype(sk_ref.dtype)
    sk_ref[:, LANES:2 * LANES] = jnp.where(low, sw, sk).astype(sk_ref.dtype)
    svt = ps[:, SWA_Q_W + SWA_KV_W:SWA_Q_W + 2 * SWA_KV_W].T
    ones_s = jnp.where(lax.broadcasted_iota(jnp.int32, (SWA_VROWS - SWA_DH, SWA_TQ), 0) == 0, 1.0, 0.0)
    for grp in range(SWA_KV_HEADS):
        for i in range(TOK // SWA_TQ):
            svt_ref[grp, i, 0:SWA_DH, :] = svt[grp * SWA_DH:(grp + 1) * SWA_DH,
                                               i * SWA_TQ:(i + 1) * SWA_TQ].astype(svt_ref.dtype)
            svt_ref[grp, i, SWA_DH:SWA_VROWS, :] = ones_s.astype(svt_ref.dtype)

    gla_ref[:, 0:GLA_W] = pg[:, 0:GLA_W] * (GLA_DK ** -0.5)
    gla_ref[:, GLA_W:4 * GLA_W] = pg[:, GLA_W:4 * GLA_W]


def _mod_index(b, j, *, layer, ctx_row):
    return (layer, jnp.where(j == 0, ctx_row, b), 0, 0)


def _inproj_call(h, mods, layer, ctx_row, w_in_p, wg, bg, tabs):
    bsz, s, d = h.shape
    nt = s // TOK
    act = MXU_DTYPE
    tab_spec = pl.BlockSpec((TOK, LANES), lambda b, j: (j, 0))
    return pl.pallas_call(
        _inproj_kernel,
        grid=(bsz, nt),
        in_specs=[
            pl.BlockSpec((None, TOK, d), lambda b, j: (b, j, 0)),
            pl.BlockSpec((None, None, 6, d), functools.partial(_mod_index, layer=layer, ctx_row=ctx_row)),
            pl.BlockSpec((d, IN_PAD_W), lambda b, j: (0, 0)),
            pl.BlockSpec((LANES, 2 * GLA_W), lambda b, j: (0, 0)),
            pl.BlockSpec((1, 2 * GLA_W), lambda b, j: (0, 0)),
            tab_spec, tab_spec, tab_spec, tab_spec,
        ],
        out_specs=[
            pl.BlockSpec((None, TOK, 4 * GLA_W), lambda b, j: (b, j, 0)),
            pl.BlockSpec((None, TOK, 2 * GLA_W), lambda b, j: (b, j, 0)),
            pl.BlockSpec((None, TOK, 2 * DIFF_W), lambda b, j: (b, j, 0)),
            pl.BlockSpec((None, DIFF_HEADS, DIFF_VROWS, TOK), lambda b, j: (b, 0, 0, j)),
            pl.BlockSpec((None, TOK, SWA_Q_W), lambda b, j: (b, j, 0)),
            pl.BlockSpec((None, TOK, 2 * LANES), lambda b, j: (b, j, 0)),
            pl.BlockSpec((None, SWA_KV_HEADS, TOK // SWA_TQ, SWA_VROWS, SWA_TQ), lambda b, j: (b, 0, j, 0, 0)),
        ],
        out_shape=[
            jax.ShapeDtypeStruct((bsz, s, 4 * GLA_W), F32),
            jax.ShapeDtypeStruct((bsz, s, 2 * GLA_W), F32),
            jax.ShapeDtypeStruct((bsz, s, 2 * DIFF_W), act),
            jax.ShapeDtypeStruct((bsz, DIFF_HEADS, DIFF_VROWS, s), act),
            jax.ShapeDtypeStruct((bsz, s, SWA_Q_W), act),
            jax.ShapeDtypeStruct((bsz, s, 2 * LANES), act),
            jax.ShapeDtypeStruct((bsz, SWA_KV_HEADS, s // SWA_TQ, SWA_VROWS, SWA_TQ), act),
        ],
        compiler_params=pltpu.CompilerParams(
            dimension_semantics=("arbitrary", "arbitrary"), vmem_limit_bytes=VMEM_LIMIT),
        name="in_projection",
    )(h, mods, w_in_p, wg, bg, *tabs)


def _gla_direction(gla_ref, gate_ref, gate_col, o_ref, st_ref, reverse, consts):
    tri, head_lane, blockdiag, sub_masks = consts
    c_, sub = GLA_CHUNK, GLA_SUB
    nsub = c_ // sub
    nchunk = TOK // c_
    g_all = gate_ref[:, gate_col:gate_col + GLA_W]
    b_all = _split_mm_t(tri, g_all)
    order = range(nchunk - 1, -1, -1) if reverse else range(nchunk)
    zero = jnp.zeros((), F32)
    chunks = []
    for c in order:
        r0 = c * c_
        q = gla_ref[r0:r0 + c_, 0:GLA_W]
        k = gla_ref[r0:r0 + c_, GLA_W:2 * GLA_W]
        v = gla_ref[r0:r0 + c_, 2 * GLA_W:3 * GLA_W]
        b = b_all[r0:r0 + c_, :]
        b_end = b[0:1, :] if reverse else b[c_ - 1:c_, :]
        atts = []
        for i in range(nsub):
            t0, t1 = i * sub, (i + 1) * sub
            if reverse:
                ref_b = b[t1:t1 + 1, :] if i < nsub - 1 else jnp.zeros((1, GLA_W), F32)
                k0, k1 = t0, c_
            else:
                ref_b = b[t0 - 1:t0, :] if i > 0 else jnp.zeros((1, GLA_W), F32)
                k0, k1 = 0, t1
            qd = q[t0:t1, :] * jnp.exp(b[t0:t1, :] - ref_b)
            lhs = jnp.concatenate([jnp.where(head_lane[hh], qd, zero) for hh in range(GLA_HEADS)], axis=0)
            kk = k[k0:k1, :] * jnp.exp(ref_b - b[k0:k1, :])
            atts.append((_mm_nt(lhs, kk), k0, k1))
        upd = _mm_tn(v, k * jnp.exp(b_end - b))
        chunks.append(dict(r0=r0, v=v, atts=atts, upd=upd, qs=q * jnp.exp(b), decay=jnp.exp(b_end)))
    for ch in chunks:
        o_parts = []
        for i, (att, k0, k1) in enumerate(ch["atts"]):
            res = _mm(jnp.where(sub_masks[(reverse, i)], att, zero), ch["v"][k0:k1, :])
            o_i = res[(GLA_HEADS - 1) * sub:GLA_HEADS * sub, :]
            for hh in range(GLA_HEADS - 2, -1, -1):
                o_i = jnp.where(head_lane[hh], res[hh * sub:(hh + 1) * sub, :], o_i)
            o_parts.append(o_i)
        ch["o_intra"] = jnp.concatenate(o_parts, axis=0)
    st = st_ref[...]
    for ch in chunks:
        o_ref[ch["r0"]:ch["r0"] + c_, :] = ch["o_intra"] + _mm_nt(ch["qs"], st)
        st = st * ch["decay"] + jnp.where(blockdiag, ch["upd"], zero)
    st_ref[...] = st


def _gla_consts():
    c_, sub = GLA_CHUNK, GLA_SUB
    nsub = c_ // sub
    r = lax.broadcasted_iota(jnp.int32, (TOK, TOK), 0)
    cc = lax.broadcasted_iota(jnp.int32, (TOK, TOK), 1)
    same_chunk = (r // c_) == (cc // c_)
    tri_f = jnp.where(same_chunk & (cc <= r), 1.0, 0.0).astype(F32)
    tri_b = jnp.where(same_chunk & (cc >= r), 1.0, 0.0).astype(F32)
    lane = lax.broadcasted_iota(jnp.int32, (1, GLA_W), 1)
    head_lane = [(lane // GLA_DK) == hh for hh in range(GLA_HEADS)]
    blockdiag = (r // GLA_DV) == (cc // GLA_DK)
    sub_masks = {}
    for reverse in (False, True):
        for i in range(nsub):
            nk = (nsub - i) * sub if reverse else (i + 1) * sub
            rr = lax.broadcasted_iota(jnp.int32, (GLA_HEADS * sub, nk), 0) % sub
            kc = lax.broadcasted_iota(jnp.int32, (GLA_HEADS * sub, nk), 1)
            if reverse:
                sub_masks[(reverse, i)] = kc >= rr
            else:
                sub_masks[(reverse, i)] = (kc - (nk - sub)) <= rr
    return tri_f, tri_b, head_lane, blockdiag, sub_masks


def _gla_kernel(gla_f_ref, gate_f_ref, gla_b_ref, gate_b_ref, of_ref, ob_ref, stf_ref, stb_ref):
    @pl.when(pl.program_id(1) == 0)
    def _():
        stf_ref[...] = jnp.zeros_like(stf_ref)
        stb_ref[...] = jnp.zeros_like(stb_ref)

    tri_f, tri_b, head_lane, blockdiag, sub_masks = _gla_consts()
    _gla_direction(gla_f_ref, gate_f_ref, 0, of_ref, stf_ref, False, (tri_f, head_lane, blockdiag, sub_masks))
    _gla_direction(gla_b_ref, gate_b_ref, GLA_W, ob_ref, stb_ref, True, (tri_b, head_lane, blockdiag, sub_masks))


def _gla_call(gla_in, gates):
    bsz, s, _ = gla_in.shape
    nt = s // TOK

    def fwd(b, j):
        return (b, j, 0)

    def bwd(b, j):
        return (b, jnp.where(j == 0, 0, nt - j), 0)

    return pl.pallas_call(
        _gla_kernel,
        grid=(bsz, nt),
        in_specs=[
            pl.BlockSpec((None, TOK, 4 * GLA_W), fwd),
            pl.BlockSpec((None, TOK, 2 * GLA_W), fwd),
            pl.BlockSpec((None, TOK, 4 * GLA_W), bwd),
            pl.BlockSpec((None, TOK, 2 * GLA_W), bwd),
        ],
        out_specs=[
            pl.BlockSpec((None, TOK, GLA_W), fwd),
            pl.BlockSpec((None, TOK, GLA_W), bwd),
        ],
        out_shape=[jax.ShapeDtypeStruct((bsz, s, GLA_W), F32)] * 2,
        scratch_shapes=[pltpu.VMEM((GLA_W, GLA_W), F32), pltpu.VMEM((GLA_W, GLA_W), F32)],
        compiler_params=pltpu.CompilerParams(
            dimension_semantics=("arbitrary", "arbitrary"), vmem_limit_bytes=VMEM_LIMIT),
        name="gla_scan",
    )(gla_in, gates, gla_in, gates)


def _diff_kernel(q_ref, k_ref, vt_ref, lam_ref, lam0_ref, g_ref, o_ref, *, n_ctx):
    lam_p = lam_ref[...]
    lam = (jnp.exp(jnp.sum(lam_p[0:1, :] * lam_p[1:2, :], axis=-1, keepdims=True))
           - jnp.exp(jnp.sum(lam_p[2:3, :] * lam_p[3:4, :], axis=-1, keepdims=True))
           + lam0_ref[...])
    lane = lax.broadcasted_iota(jnp.int32, (1, DIFF_W), 1)

    def attend(nk, parts):
        q = q_ref[...]
        step = nk // parts

        def scores(idx):
            lo = idx * DIFF_DH
            qm = jnp.where((lane >= lo) & (lane < lo + DIFF_DH), q, jnp.zeros_like(q))
            return [_mm_nt(k_ref[i * step:(i + 1) * step, :], qm).astype(MXU_DTYPE)
                    for i in range(parts)]

        def probs(sts):
            m = _col_max(sts[0])
            for st in sts[1:]:
                m = jnp.maximum(m, _col_max(st))
            return [jnp.exp2(st - m) for st in sts]

        def values(idx, ps):
            acc = None
            for i, p in enumerate(ps):
                pv = _mm(vt_ref[idx // 2, :, i * step:(i + 1) * step], p)
                acc = pv if acc is None else acc + pv
            return acc[0:DIFF_DV, :] / acc[DIFF_DV:DIFF_DV + 1, :]

        n_str = 2 * DIFF_HEADS
        sts = {0: scores(0), 1: scores(1)}
        ps = {0: probs(sts.pop(0))}
        comps = []
        for idx in range(n_str):
            if idx + 2 < n_str:
                sts[idx + 2] = scores(idx + 2)
            if idx + 1 < n_str:
                ps[idx + 1] = probs(sts.pop(idx + 1))
            comps.append(values(idx, ps.pop(idx)))
        heads = []
        for hh in range(DIFF_HEADS):
            o_h = comps[2 * hh] - lam * comps[2 * hh + 1]
            ms = jnp.mean(o_h * o_h, axis=0, keepdims=True)
            heads.append(o_h * lax.rsqrt(ms + LN_EPS))
        out = jnp.concatenate(heads, axis=0).T
        o_ref[...] = (out * g_ref[...]).astype(o_ref.dtype)

    j = pl.program_id(1)

    @pl.when(j == 0)
    def _():
        attend(n_ctx, 1)

    @pl.when(j > 0)
    def _():
        attend(k_ref.shape[0], 2)


def _col_max(x):
    r = x.shape[0]
    slab = 16
    while r % (2 * slab) == 0 and r // slab > 32:
        slab *= 2
    acc = x[0:slab, :]
    for i in range(1, r // slab):
        acc = jnp.maximum(acc, x[i * slab:(i + 1) * slab, :])
    return jnp.max(acc, axis=0, keepdims=True)


def _diff_call(dqk, dvt, lam_p, lam0, g_eff):
    bsz, s, _ = dqk.shape
    nt = s // TOK
    return pl.pallas_call(
        functools.partial(_diff_kernel, n_ctx=TOK),
        grid=(bsz, nt),
        in_specs=[
            pl.BlockSpec((None, TOK, DIFF_W), lambda b, j: (b, j, 0)),
            pl.BlockSpec((None, s, DIFF_W), lambda b, j: (b, 0, 1)),
            pl.BlockSpec((None, DIFF_HEADS, DIFF_VROWS, s), lambda b, j: (b, 0, 0, 0)),
            pl.BlockSpec((4, DIFF_DH), lambda b, j: (0, 0)),
            pl.BlockSpec((1, 1), lambda b, j: (0, 0)),
            pl.BlockSpec((1, DIFF_W), lambda b, j: (0, 0)),
        ],
        out_specs=pl.BlockSpec((None, TOK, DIFF_W), lambda b, j: (b, j, 0)),
        out_shape=jax.ShapeDtypeStruct((bsz, s, DIFF_W), MXU_DTYPE),
        compiler_params=pltpu.CompilerParams(
            dimension_semantics=("arbitrary", "arbitrary"), vmem_limit_bytes=VMEM_LIMIT),
        name="diff_attention",
    )(dqk, dqk, dvt, lam_p, lam0, g_eff)


def _swa_kernel(q_ref, k_ref, vt_ref, sink_ref, o_ref, *, n_ctx, n_lat):
    j = pl.program_id(1)
    tq = SWA_TQ
    n_win = 3
    n_ctx_t = n_ctx // tq
    s_rows = n_ctx + n_lat
    lane = lax.broadcasted_iota(jnp.int32, (1, LANES), 1)
    low = lane < SWA_DH
    q_per_kv = SWA_HEADS // SWA_KV_HEADS
    chains = [(sub, grp) for sub in range(TOK // tq) for grp in range(SWA_KV_HEADS)]

    def window(sub):
        p0 = j * TOK + sub * tq - n_ctx
        start = pl.multiple_of(jnp.clip(n_ctx + p0 - tq, 0, s_rows - n_win * tq), tq)
        return p0, start

    def scores(idx):
        sub, grp = chains[idx]
        _, start = window(sub)
        blocks = []
        for r in range(q_per_kv):
            hh = grp * q_per_kv + r
            qp = q_ref[sub * tq:(sub + 1) * tq, (hh // 2) * LANES:(hh // 2 + 1) * LANES]
            blocks.append(jnp.where(low if hh % 2 == 0 else ~low, qp, jnp.zeros_like(qp)))
        lhs = jnp.concatenate(blocks, axis=0)
        st_c = _mm_nt(k_ref[0:n_ctx, grp * LANES:(grp + 1) * LANES], lhs)
        st_l = _mm_nt(k_ref[pl.ds(start, n_win * tq), grp * LANES:(grp + 1) * LANES], lhs)
        return st_c, st_l

    def probs(idx, st):
        sub, grp = chains[idx]
        st_c, st_l = st
        p0, start = window(sub)
        pos_q = p0 + lax.broadcasted_iota(jnp.int32, (1, tq), 1)
        pos_k = start - n_ctx + lax.broadcasted_iota(jnp.int32, (n_win * tq, 1), 0)
        valid = (jnp.abs(pos_k - pos_q) <= WINDOW) & (pos_k >= 0) & (p0 >= 0)
        bias = jnp.where(valid, 0.0, -jnp.inf).astype(F32)
        st_l = st_l + jnp.concatenate([bias] * q_per_kv, axis=1)
        st_c = st_c.astype(MXU_DTYPE)
        st_l = st_l.astype(MXU_DTYPE)
        snk = jnp.concatenate([sink_ref[grp * q_per_kv + r:grp * q_per_kv + r + 1, :]
                               for r in range(q_per_kv)], axis=1)
        m = jnp.maximum(jnp.maximum(_col_max(st_c), _col_max(st_l)).astype(F32), snk)
        mb = m.astype(MXU_DTYPE)
        return jnp.exp2(st_c - mb), jnp.exp2(st_l - mb), jnp.exp2(snk - mb.astype(F32))

    def values(idx, pr):
        sub, grp = chains[idx]
        p_c, p_l, p_snk = pr
        _, start = window(sub)
        t0 = start // tq
        acc = None
        for i in range(n_ctx_t):
            pv = _mm(vt_ref[grp, i], p_c[i * tq:(i + 1) * tq, :])
            acc = pv if acc is None else acc + pv
        for i in range(n_win):
            acc = acc + _mm(vt_ref[grp, t0 + i], p_l[i * tq:(i + 1) * tq, :])
        o = acc[0:SWA_DH, :] / (acc[SWA_DH:SWA_DH + 1, :] + p_snk)
        for hp in range(q_per_kv // 2):
            pair = grp * (q_per_kv // 2) + hp
            both = jnp.concatenate([o[:, (2 * hp) * tq:(2 * hp + 1) * tq],
                                    o[:, (2 * hp + 1) * tq:(2 * hp + 2) * tq]], axis=0)
            o_ref[sub * tq:(sub + 1) * tq, pair * LANES:(pair + 1) * LANES] = both.T.astype(o_ref.dtype)

    n_ch = len(chains)
    sts = {0: scores(0), 1: scores(1)}
    prs = {0: probs(0, sts.pop(0))}
    for idx in range(n_ch):
        if idx + 2 < n_ch:
            sts[idx + 2] = scores(idx + 2)
        if idx + 1 < n_ch:
            prs[idx + 1] = probs(idx + 1, sts.pop(idx + 1))
        values(idx, prs.pop(idx))


def _swa_call(sq, sk, svt, sink_tile, n_ctx):
    bsz, s, _ = sq.shape
    return pl.pallas_call(
        functools.partial(_swa_kernel, n_ctx=n_ctx, n_lat=s - n_ctx),
        grid=(bsz, s // TOK),
        in_specs=[
            pl.BlockSpec((None, TOK, SWA_Q_W), lambda b, j: (b, j, 0)),
            pl.BlockSpec((None, s, 2 * LANES), lambda b, j: (b, 0, 0)),
            pl.BlockSpec((None, SWA_KV_HEADS, s // SWA_TQ, SWA_VROWS, SWA_TQ), lambda b, j: (b, 0, 0, 0, 0)),
            pl.BlockSpec((SWA_HEADS, LANES), lambda b, j: (0, 0)),
        ],
        out_specs=pl.BlockSpec((None, TOK, SWA_Q_W), lambda b, j: (b, j, 0)),
        out_shape=jax.ShapeDtypeStruct((bsz, s, SWA_Q_W), MXU_DTYPE),
        compiler_params=pltpu.CompilerParams(
            dimension_semantics=("arbitrary", "arbitrary"), vmem_limit_bytes=VMEM_LIMIT),
        name="windowed_gqa",
    )(sq, sk, svt, sink_tile)


def _outproj_kernel(of_ref, ob_ref, og_ref, diff_ref, swa_ref, h_ref, mod_ref, w_ref,
                    gng_ref, lng_ref, lnb_ref, o_ref, *, alpha):
    o = of_ref[...] + ob_ref[...]
    ms = _split_mm(o * o, _group_mean_matrix(GLA_W, GLA_DV))
    gla = o * lax.rsqrt(ms + LN_EPS) * gng_ref[...] * _silu(og_ref[...])
    y = (_mm(gla, w_ref[0:GLA_W, :])
         + _mm(diff_ref[...], w_ref[GLA_W:GLA_W + DIFF_W, :])
         + _mm(swa_ref[...], w_ref[GLA_W + DIFF_W:, :]))
    z = alpha * h_ref[...] + mod_ref[2:3, :] * y
    o_ref[...] = _ln_plain(z) * lng_ref[...] + lnb_ref[...]


def _outproj_call(o_f, o_b, gla_in, diff_o, swa_o, h, mods, layer, ctx_row, w_out, gng, lng, lnb, alpha):
    bsz, s, d = h.shape
    nt = s // TOK
    tile = lambda w: pl.BlockSpec((None, TOK, w), lambda b, j: (b, j, 0))
    const = lambda shape: pl.BlockSpec(shape, lambda b, j: (0,) * len(shape))
    return pl.pallas_call(
        functools.partial(_outproj_kernel, alpha=alpha),
        grid=(bsz, nt),
        in_specs=[
            tile(GLA_W), tile(GLA_W),
            pl.BlockSpec((None, TOK, GLA_W), lambda b, j: (b, j, 3)),
            tile(DIFF_W), tile(SWA_Q_W), tile(d),
            pl.BlockSpec((None, None, 6, d), functools.partial(_mod_index, layer=layer, ctx_row=ctx_row)),
            const((d, d)), const((1, GLA_W)), const((1, d)), const((1, d)),
        ],
        out_specs=tile(d),
        out_shape=jax.ShapeDtypeStruct((bsz, s, d), F32),
        compiler_params=pltpu.CompilerParams(
            dimension_semantics=("arbitrary", "arbitrary"), vmem_limit_bytes=VMEM_LIMIT),
        name="out_projection",
    )(o_f, o_b, gla_in, diff_o, swa_o, h, mods, w_out, gng, lng, lnb)


def _ffn_kernel(h_ref, mod_ref, wg_ref, wu_ref, wd_ref, lng_ref, lnb_ref, o_ref, *, alpha, skip_first):
    def body():
        x = h_ref[...]
        u = (_ln_plain(x) * (1.0 + mod_ref[4:5, :]) + mod_ref[3:4, :]).astype(MXU_DTYPE)
        a = _silu(_mm(u, wg_ref[...])) * _mm(u, wu_ref[...])
        f = _mm(a, wd_ref[...])
        z = alpha * x + mod_ref[5:6, :] * f
        o_ref[...] = _ln_plain(z) * lng_ref[...] + lnb_ref[...]

    if skip_first:
        pl.when(pl.program_id(1) > 0)(body)
    else:
        body()


def _ffn_call(h, mods, layer, ctx_row, w_g, w_u, w_d, lng, lnb, alpha, latent_only):
    bsz, s, d = h.shape
    nt = s // TOK
    hid = w_g.shape[1]
    single = pl.Buffered(1)
    const = lambda shape: pl.BlockSpec(shape, lambda b, j: (0,) * len(shape), pipeline_mode=single)
    if latent_only:
        out_spec = pl.BlockSpec((None, TOK, d), lambda b, j: (b, jnp.maximum(j - 1, 0), 0))
        out_rows = s - TOK
    else:
        out_spec = pl.BlockSpec((None, TOK, d), lambda b, j: (b, j, 0))
        out_rows = s
    return pl.pallas_call(
        functools.partial(_ffn_kernel, alpha=alpha, skip_first=latent_only),
        grid=(bsz, nt),
        in_specs=[
            pl.BlockSpec((None, TOK, d), lambda b, j: (b, j, 0)),
            pl.BlockSpec((None, None, 6, d), functools.partial(_mod_index, layer=layer, ctx_row=ctx_row)),
            const((d, hid)), const((d, hid)), const((hid, d)),
            pl.BlockSpec((1, d), lambda b, j: (0, 0)), pl.BlockSpec((1, d), lambda b, j: (0, 0)),
        ],
        out_specs=out_spec,
        out_shape=jax.ShapeDtypeStruct((bsz, out_rows, d), F32),
        compiler_params=pltpu.CompilerParams(
            dimension_semantics=("arbitrary", "arbitrary"), vmem_limit_bytes=VMEM_LIMIT),
        name="swiglu_ffn",
    )(h, mods, w_g, w_u, w_d, lng, lnb)


POST_TOK = 2 * TOK
MXU_TILE = 256


def _ffn_chunk_bounds(hid):
    cut = -(-(hid // MXU_TILE) // 2) * MXU_TILE
    return ((0, cut), (cut, hid))


def _post_kernel(of_ref, ob_ref, og_ref, diff_ref, swa_ref, h_ref, mod_ref, wo_ref, wg_ref, wu_ref, wd_ref,
                 gng_ref, lng_ref, lnb_ref, o_ref, *, alpha, tiles_per_sample, ctx_row):
    t = pl.program_id(0)
    halves = POST_TOK // TOK
    hid = wg_ref.shape[1]
    bounds = _ffn_chunk_bounds(hid)
    gm = _group_mean_matrix(GLA_W, GLA_DV)
    st = [dict() for _ in range(halves)]

    def out_proj(x):
        rows = slice(x * TOK, (x + 1) * TOK)
        g = t * halves + x
        row = jnp.where(g % tiles_per_sample == 0, ctx_row, g // tiles_per_sample)
        st[x]["mod"] = mod_ref[row]
        o = of_ref[rows, :] + ob_ref[rows, :]
        ms = _split_mm(o * o, gm)
        gla = o * lax.rsqrt(ms + LN_EPS) * gng_ref[...] * _silu(og_ref[rows, :])
        st[x]["y"] = (_mm(gla, wo_ref[0:GLA_W, :])
                      + _mm(diff_ref[rows, :], wo_ref[GLA_W:GLA_W + DIFF_W, :])
                      + _mm(swa_ref[rows, :], wo_ref[GLA_W + DIFF_W:, :]))

    def mid_norm(x):
        rows = slice(x * TOK, (x + 1) * TOK)
        mod = st[x]["mod"]
        z = alpha * h_ref[rows, :] + mod[2:3, :] * st[x].pop("y")
        hm = _ln_plain(z) * lng_ref[0:1, :] + lnb_ref[0:1, :]
        st[x]["hm"] = hm
        st[x]["u"] = (_ln_plain(hm) * (1.0 + mod[4:5, :]) + mod[3:4, :]).astype(MXU_DTYPE)

    def gate_up(x, c):
        u = st[x]["u"]
        c0, c1 = bounds[c]
        st[x]["g", c] = jnp.dot(u, wg_ref[:, c0:c1], preferred_element_type=F32)
        st[x]["p", c] = jnp.dot(u, wu_ref[:, c0:c1], preferred_element_type=F32)

    def act(x, c):
        st[x]["a", c] = (_silu(st[x].pop(("g", c))) * st[x].pop(("p", c))).astype(MXU_DTYPE)

    def down(x, c):
        c0, c1 = bounds[c]
        f = jnp.dot(st[x].pop(("a", c)), wd_ref[c0:c1, :], preferred_element_type=F32)
        st[x]["f"] = f if c == 0 else st[x]["f"] + f

    def final(x):
        rows = slice(x * TOK, (x + 1) * TOK)
        z = alpha * st[x]["hm"] + st[x]["mod"][5:6, :] * st[x]["f"]
        o_ref[rows, :] = _ln_plain(z) * lng_ref[1:2, :] + lnb_ref[1:2, :]

    a_, b_ = 0, 1
    out_proj(a_)
    out_proj(b_)
    mid_norm(a_)
    gate_up(a_, 0)
    mid_norm(b_)
    gate_up(a_, 1)
    act(a_, 0)
    gate_up(b_, 0)
    act(a_, 1)
    down(a_, 0)
    gate_up(b_, 1)
    act(b_, 0)
    down(a_, 1)
    act(b_, 1)
    down(b_, 0)
    final(a_)
    down(b_, 1)
    final(b_)


def _post_call(o_f, o_b, gla_in, diff_o, swa_o, h, mods, layer, ctx_row, w_out, w_g, w_u, w_d, gng, lng, lnb, alpha):
    bsz, s, d = h.shape
    rows = bsz * s
    assert rows % POST_TOK == 0
    hid = w_g.shape[1]
    flat = lambda a: a.reshape(rows, a.shape[-1])
    tile = lambda w: pl.BlockSpec((POST_TOK, w), lambda t: (t, 0))
    single = pl.Buffered(1)
    const = lambda shape: pl.BlockSpec(shape, lambda t: (0,) * len(shape), pipeline_mode=single)
    out = pl.pallas_call(
        functools.partial(_post_kernel, alpha=alpha, tiles_per_sample=s // TOK, ctx_row=ctx_row),
        grid=(rows // POST_TOK,),
        in_specs=[
            tile(GLA_W), tile(GLA_W),
            pl.BlockSpec((POST_TOK, GLA_W), lambda t: (t, 3)),
            tile(DIFF_W), tile(SWA_Q_W), tile(d),
            pl.BlockSpec((None,) + mods.shape[1:], lambda t: (layer, 0, 0, 0)),
            const((d, d)), const((d, hid)), const((d, hid)), const((hid, d)),
            pl.BlockSpec((1, GLA_W), lambda t: (0, 0)),
            pl.BlockSpec((2, d), lambda t: (0, 0)), pl.BlockSpec((2, d), lambda t: (0, 0)),
        ],
        out_specs=tile(d),
        out_shape=jax.ShapeDtypeStruct((rows, d), F32),
        compiler_params=pltpu.CompilerParams(
            dimension_semantics=("arbitrary",), vmem_limit_bytes=POST_VMEM_LIMIT),
        name="post_attention",
    )(flat(o_f), flat(o_b), flat(gla_in), flat(diff_o), flat(swa_o), flat(h), mods,
      w_out, w_g, w_u, w_d, gng, lng, lnb)
    return out.reshape(bsz, s, d)


def _rope_tables(rows, n_ctx, dim):
    row = jnp.repeat(jnp.arange(rows, dtype=F32), GRID_W)
    col = jnp.tile(jnp.arange(GRID_W, dtype=F32), rows)
    n_freq = dim // 4
    inv = jnp.power(ROPE_BASE, -jnp.arange(n_freq, dtype=F32) / n_freq)
    ang = jnp.concatenate([row[:, None] * inv, col[:, None] * inv], axis=-1)
    cos, sin = jnp.cos(ang), jnp.sin(ang)
    reps = LANES // dim
    cos_t = jnp.tile(jnp.concatenate([cos, cos], axis=-1), (1, reps))
    sin_t = jnp.tile(jnp.concatenate([-sin, sin], axis=-1), (1, reps))
    cos_t = jnp.concatenate([jnp.ones((n_ctx, LANES), F32), cos_t], axis=0)
    sin_t = jnp.concatenate([jnp.zeros((n_ctx, LANES), F32), sin_t], axis=0)
    return cos_t, sin_t


def _reorder_w_in(w_in):
    o = 0
    parts = {}
    for name, width in (("gq", GLA_W), ("gk", GLA_W), ("gv", GLA_W), ("go", GLA_W),
                        ("zf", GLA_GATE_RANK), ("zb", GLA_GATE_RANK),
                        ("dq", DIFF_W), ("dk", DIFF_W), ("dv", DIFF_W),
                        ("sq", SWA_Q_W), ("sk", SWA_KV_W), ("sv", SWA_KV_W)):
        parts[name] = w_in[:, o:o + width]
        o += width
    cols = [parts[n] for n in ("gq", "gk", "gv", "go", "dq", "dk", "dv", "sq", "sk", "sv", "zf", "zb")]
    pad = IN_PAD_W - sum(c.shape[1] for c in cols)
    cols.append(jnp.zeros((w_in.shape[0], pad), w_in.dtype))
    return jnp.concatenate(cols, axis=1)


def kernel(x, c, ctx, c_ctx, w_ada, b_ada, w_in, w_gla_gate, b_gla_gate, gla_norm_g, diff_lambda,
           diff_norm_g, swa_sink, w_out, ln_g, ln_b, w_ffn_gate, w_ffn_up, w_ffn_down):
    bsz, n_lat, d = x.shape
    n_ctx = ctx.shape[1]
    depth = w_ada.shape[0]
    assert n_ctx == TOK and n_lat % TOK == 0 and n_lat % GRID_W == 0
    alpha = (2.0 * depth) ** 0.25
    wdt = MXU_DTYPE

    cond_rows = -(-(bsz + 1) // 8) * 8
    cond =jnp.concatenate([c, c_ctx[None, :], jnp.zeros((cond_rows - bsz - 1, d), F32)], axis=0)
    mods = _mods_call(cond, w_ada, b_ada).reshape(depth, cond_rows, 6, d)
    ctx_row = bsz

    rows = n_lat // GRID_W
    tabs = _rope_tables(rows, n_ctx, DIFF_DH) + _rope_tables(rows, n_ctx, SWA_DH)

    h = jnp.concatenate([ctx, x], axis=1)
    for layer in range(depth):
        last = layer == depth - 1
        lam_init = 0.8 - 0.6 * math.exp(-0.3 * layer)
        w_in_p = _reorder_w_in(w_in[layer]).astype(wdt)
        wg = jnp.zeros((LANES, 2 * GLA_W), F32)
        wg = wg.at[0:GLA_GATE_RANK, 0:GLA_W].set(w_gla_gate[layer, 0])
        wg = wg.at[GLA_GATE_RANK:2 * GLA_GATE_RANK, GLA_W:].set(w_gla_gate[layer, 1])
        bg = b_gla_gate[layer].reshape(1, 2 * GLA_W)

        gla_in, gates, dqk, dvt, sq, sk, svt = _inproj_call(h, mods, layer, ctx_row, w_in_p, wg.astype(wdt), bg, tabs)
        o_f, o_b = _gla_call(gla_in, gates)
        diff_o = _diff_call(dqk, dvt, diff_lambda[layer], jnp.full((1, 1), lam_init, F32),
                            (jnp.tile(diff_norm_g[layer], DIFF_HEADS) * (1.0 - lam_init)).reshape(1, DIFF_W))
        swa_o = _swa_call(sq, sk, svt,
                          jnp.broadcast_to(swa_sink[layer][:, None] * LOG2_E, (SWA_HEADS, LANES)), n_ctx)
        gng = jnp.tile(gla_norm_g[layer], GLA_HEADS).reshape(1, GLA_W)
        w_o, w_g, w_u, w_d = (w_out[layer].astype(wdt), w_ffn_gate[layer].astype(wdt),
                              w_ffn_up[layer].astype(wdt), w_ffn_down[layer].astype(wdt))
        if last:
            h_mid = _outproj_call(o_f, o_b, gla_in, diff_o, swa_o, h, mods, layer, ctx_row, w_o, gng,
                                  ln_g[layer, 0].reshape(1, d), ln_b[layer, 0].reshape(1, d), alpha)
            h = _ffn_call(h_mid, mods, layer, ctx_row, w_g, w_u, w_d,
                          ln_g[layer, 1].reshape(1, d), ln_b[layer, 1].reshape(1, d), alpha, True)
        else:
            h = _post_call(o_f, o_b, gla_in, diff_o, swa_o, h, mods, layer, ctx_row, w_o, w_g, w_u, w_d,
                           gng, ln_g[layer], ln_b[layer], alpha)
    return h
```

```python
import functools
import math

import jax
import jax.numpy as jnp
from jax import lax
from jax.experimental import pallas as pl
from jax.experimental.pallas import tpu as pltpu

F32 = jnp.float32
MXU_DTYPE = jnp.bfloat16

GRID_W = 64
GLA_HEADS, GLA_DK, GLA_DV = 4, 64, 64
GLA_GATE_RANK = 16
GLA_GATE_NORM = 16.0
GLA_CHUNK = 64
GLA_SUB = 16
DIFF_HEADS, DIFF_DH = 4, 32
DIFF_DV = 2 * DIFF_DH
SWA_HEADS, SWA_KV_HEADS, SWA_DH = 8, 2, 64
WINDOW = 128
ROPE_BASE = 10000.0
LN_EPS = 1e-6

GLA_W = GLA_HEADS * GLA_DK
DIFF_W = DIFF_HEADS * 2 * DIFF_DH
SWA_Q_W = SWA_HEADS * SWA_DH
SWA_KV_W = SWA_KV_HEADS * SWA_DH
LANES = 128
TOK = 256
SWA_TQ = 128
DIFF_VROWS = DIFF_DV + 16
SWA_VROWS = SWA_DH + 16
LOG2_E = math.log2(math.e)
VMEM_LIMIT = 52 * 1024 * 1024
POST_VMEM_LIMIT = 58 * 1024 * 1024


def _mm(a, b):
    return jnp.dot(a.astype(MXU_DTYPE), b.astype(MXU_DTYPE), preferred_element_type=F32)


def _mm_nt(a, b):
    return lax.dot_general(a.astype(MXU_DTYPE), b.astype(MXU_DTYPE),
                           (((1,), (1,)), ((), ())), preferred_element_type=F32)


def _mm_tn(a, b):
    return lax.dot_general(a.astype(MXU_DTYPE), b.astype(MXU_DTYPE),
                           (((0,), (0,)), ((), ())), preferred_element_type=F32)


def _split_mm(a, b_exact):
    if MXU_DTYPE == F32:
        return jnp.dot(a, b_exact.astype(F32), preferred_element_type=F32)
    hi = a.astype(jnp.bfloat16)
    r1 = a - hi.astype(F32)
    mid = r1.astype(jnp.bfloat16)
    lo = (r1 - mid.astype(F32)).astype(jnp.bfloat16)
    b = b_exact.astype(jnp.bfloat16)
    return (jnp.dot(hi, b, preferred_element_type=F32)
            + jnp.dot(mid, b, preferred_element_type=F32)
            + jnp.dot(lo, b, preferred_element_type=F32))


def _split_mm_t(b_exact, a):
    if MXU_DTYPE == F32:
        return jnp.dot(b_exact.astype(F32), a, preferred_element_type=F32)
    hi = a.astype(jnp.bfloat16)
    r1 = a - hi.astype(F32)
    mid = r1.astype(jnp.bfloat16)
    lo = (r1 - mid.astype(F32)).astype(jnp.bfloat16)
    b = b_exact.astype(jnp.bfloat16)
    return (jnp.dot(b, hi, preferred_element_type=F32)
            + jnp.dot(b, mid, preferred_element_type=F32)
            + jnp.dot(b, lo, preferred_element_type=F32))


def _ln_plain(x):
    mu = jnp.mean(x, axis=-1, keepdims=True)
    xc = x - mu
    var = jnp.mean(xc * xc, axis=-1, keepdims=True)
    return xc * lax.rsqrt(var + LN_EPS)


def _silu(x):
    return x / (1.0 + jnp.exp(-x))


def _group_mean_matrix(width, group):
    r = lax.broadcasted_iota(jnp.int32, (width, width), 0) // group
    c = lax.broadcasted_iota(jnp.int32, (width, width), 1) // group
    return jnp.where(r == c, 1.0 / group, 0.0).astype(F32)


def _mods_kernel(a_ref, w_ref, b_ref, o_ref):
    a = a_ref[...]
    o_ref[...] = jnp.dot(_silu(a), w_ref[...], preferred_element_type=F32,
                         precision=lax.Precision.HIGHEST) + b_ref[...]


def _mods_call(cond, w_ada, b_ada):
    depth, d, width = w_ada.shape
    rows = cond.shape[0]
    bn = 1536
    return pl.pallas_call(
        _mods_kernel,
        grid=(depth, width // bn),
        in_specs=[
            pl.BlockSpec((rows, d), lambda l, n: (0, 0)),
            pl.BlockSpec((None, d, bn), lambda l, n: (l, 0, n)),
            pl.BlockSpec((None, 1, bn), lambda l, n: (l, 0, n)),
        ],
        out_specs=pl.BlockSpec((None, rows, bn), lambda l, n: (l, 0, n)),
        out_shape=jax.ShapeDtypeStruct((depth, rows, width), F32),
        compiler_params=pltpu.CompilerParams(vmem_limit_bytes=VMEM_LIMIT),
        name="adaln_mods",
    )(cond, w_ada, b_ada.reshape(depth, 1, width))


C_GQ, C_GK, C_GV, C_GO = 0, 256, 512, 768
C_DQ, C_DK, C_DV = 1024, 1280, 1536
C_SQ, C_SK, C_SV = 1792, 2304, 2432
C_Z = 2560
IN_PAD_W = 2688


def _rope(x, cos, sin_signed, half):
    outs = []
    lane = lax.broadcasted_iota(jnp.int32, (1, LANES), 1)
    first = (lane % (2 * half)) < half
    for s in range(x.shape[1] // LANES):
        xs = x[:, s * LANES:(s + 1) * LANES]
        up = pltpu.roll(xs, LANES - half, axis=1)
        dn = pltpu.roll(xs, half, axis=1)
        outs.append(xs * cos + jnp.where(first, up, dn) * sin_signed)
    return outs[0] if len(outs) == 1 else jnp.concatenate(outs, axis=1)


def _inproj_kernel(h_ref, mod_ref, w_ref, wg_ref, bg_ref, cd_ref, sd_ref, cs_ref, ss_ref,
                   gla_ref, gate_ref, dqk_ref, dvt_ref, sq_ref, sk_ref, svt_ref):
    j = pl.program_id(1)
    sps = h_ref.shape[0]
    cd, sd, cs, ss = cd_ref[...], sd_ref[...], cs_ref[...], ss_ref[...]
    lane = lax.broadcasted_iota(jnp.int32, (1, LANES), 1)
    low = lane < SWA_DH
    ones_d = jnp.where(lax.broadcasted_iota(jnp.int32, (DIFF_VROWS - DIFF_DV, TOK), 0) == 0, 1.0, 0.0)
    ones_s = jnp.where(lax.broadcasted_iota(jnp.int32, (SWA_VROWS - SWA_DH, SWA_TQ), 0) == 0, 1.0, 0.0)
    st = [dict() for _ in range(sps)]

    def norm(n):
        mod = mod_ref[jnp.where(j == 0, 0, n)]
        st[n]["u"] = (_ln_plain(h_ref[n]) * (1.0 + mod[1:2, :]) + mod[0:1, :]).astype(MXU_DTYPE)

    def project(n, name, c0, c1):
        st[n][name] = jnp.dot(st[n]["u"], w_ref[:, c0:c1], preferred_element_type=F32)

    def gate_out(n):
        gpre = _mm(st[n].pop("z"), wg_ref[...]) + bg_ref[...]
        logsig = jnp.minimum(gpre, 0.0) - jnp.log(1.0 + jnp.exp(-jnp.abs(gpre)))
        gate_ref[n] = logsig * (1.0 / GLA_GATE_NORM)

    def diff_out(n):
        pd = st[n].pop("d")
        dq = _rope(pd[:, 0:DIFF_W], cd, sd, DIFF_DH // 2) * (DIFF_DH ** -0.5 * LOG2_E)
        dk = _rope(pd[:, DIFF_W:2 * DIFF_W], cd, sd, DIFF_DH // 2)
        dqk_ref[n, :, 0:DIFF_W] = dq.astype(dqk_ref.dtype)
        dqk_ref[n, :, DIFF_W:2 * DIFF_W] = dk.astype(dqk_ref.dtype)
        vt = pd[:, 2 * DIFF_W:3 * DIFF_W].T
        for hh in range(DIFF_HEADS):
            dvt_ref[n, hh, 0:DIFF_DV, :] = vt[hh * DIFF_DV:(hh + 1) * DIFF_DV, :].astype(dvt_ref.dtype)
            dvt_ref[n, hh, DIFF_DV:DIFF_VROWS, :] = ones_d.astype(dvt_ref.dtype)

    def swa_out(n):
        ps = st[n].pop("s")
        sq = _rope(ps[:, 0:SWA_Q_W], cs, ss, SWA_DH // 2) * (SWA_DH ** -0.5 * LOG2_E)
        sq_ref[n] = sq.astype(sq_ref.dtype)
        sk = _rope(ps[:, SWA_Q_W:SWA_Q_W + SWA_KV_W], cs, ss, SWA_DH // 2)
        sw = pltpu.roll(sk, SWA_DH, axis=1)
        sk_ref[n, :, 0:LANES] = jnp.where(low, sk, sw).astype(sk_ref.dtype)
        sk_ref[n, :, LANES:2 * LANES] = jnp.where(low, sw, sk).astype(sk_ref.dtype)
        svt = ps[:, SWA_Q_W + SWA_KV_W:SWA_Q_W + 2 * SWA_KV_W].T
        for grp in range(SWA_KV_HEADS):
            for i in range(TOK // SWA_TQ):
                svt_ref[n, grp, i, 0:SWA_DH, :] = svt[grp * SWA_DH:(grp + 1) * SWA_DH,
                                                      i * SWA_TQ:(i + 1) * SWA_TQ].astype(svt_ref.dtype)
                svt_ref[n, grp, i, SWA_DH:SWA_VROWS, :] = ones_s.astype(svt_ref.dtype)

    def gla_out(n):
        pg = st[n].pop("g")
        gla_ref[n, :, 0:GLA_W] = pg[:, 0:GLA_W] * (GLA_DK ** -0.5)
        gla_ref[n, :, GLA_W:4 * GLA_W] = pg[:, GLA_W:4 * GLA_W]

    def steps(n):
        return [
            lambda: norm(n),
            lambda: project(n, "z", C_Z, C_Z + LANES),
            lambda: project(n, "d", C_DQ, C_DV + DIFF_W),
            lambda: gate_out(n),
            lambda: project(n, "s", C_SQ, C_SV + SWA_KV_W),
            lambda: diff_out(n),
            lambda: project(n, "g", C_GQ, C_GO + GLA_W),
            lambda: swa_out(n),
            lambda: gla_out(n),
        ]

    lag = 4
    plans = [steps(n) for n in range(sps)]
    for tick in range(len(plans[0]) + lag * (sps - 1)):
        for n in range(sps):
            i = tick - lag * n
            if 0 <= i < len(plans[n]):
                plans[n][i]()


def _mod_index(b, j, *, layer, ctx_row):
    return (layer, jnp.where(j == 0, ctx_row, b), 0, 0)


def _inproj_call(h, mods, layer, ctx_row, w_in_p, wg, bg, tabs):
    bsz, s, d = h.shape
    nt = s // TOK
    act = MXU_DTYPE
    sps = 2 if (bsz % 2 == 0 and ctx_row % 2 == 0) else 1
    tab_spec = pl.BlockSpec((TOK, LANES), lambda b, j: (j, 0))

    def mod_index(b, j):
        return (layer, jnp.where(j == 0, ctx_row // sps, b), 0, 0)

    return pl.pallas_call(
        _inproj_kernel,
        grid=(bsz // sps, nt),
        in_specs=[
            pl.BlockSpec((sps, TOK, d), lambda b, j: (b, j, 0)),
            pl.BlockSpec((None, sps, 6, d), mod_index),
            pl.BlockSpec((d, IN_PAD_W), lambda b, j: (0, 0)),
            pl.BlockSpec((LANES, 2 * GLA_W), lambda b, j: (0, 0)),
            pl.BlockSpec((1, 2 * GLA_W), lambda b, j: (0, 0)),
            tab_spec, tab_spec, tab_spec, tab_spec,
        ],
        out_specs=[
            pl.BlockSpec((sps, TOK, 4 * GLA_W), lambda b, j: (b, j, 0)),
            pl.BlockSpec((sps, TOK, 2 * GLA_W), lambda b, j: (b, j, 0)),
            pl.BlockSpec((sps, TOK, 2 * DIFF_W), lambda b, j: (b, j, 0)),
            pl.BlockSpec((sps, DIFF_HEADS, DIFF_VROWS, TOK), lambda b, j: (b, 0, 0, j)),
            pl.BlockSpec((sps, TOK, SWA_Q_W), lambda b, j: (b, j, 0)),
            pl.BlockSpec((sps, TOK, 2 * LANES), lambda b, j: (b, j, 0)),
            pl.BlockSpec((sps, SWA_KV_HEADS, TOK // SWA_TQ, SWA_VROWS, SWA_TQ), lambda b, j: (b, 0, j, 0, 0)),
        ],
        out_shape=[
            jax.ShapeDtypeStruct((bsz, s, 4 * GLA_W), F32),
            jax.ShapeDtypeStruct((bsz, s, 2 * GLA_W), F32),
            jax.ShapeDtypeStruct((bsz, s, 2 * DIFF_W), act),
            jax.ShapeDtypeStruct((bsz, DIFF_HEADS, DIFF_VROWS, s), act),
            jax.ShapeDtypeStruct((bsz, s, SWA_Q_W), act),
            jax.ShapeDtypeStruct((bsz, s, 2 * LANES), act),
            jax.ShapeDtypeStruct((bsz, SWA_KV_HEADS, s // SWA_TQ, SWA_VROWS, SWA_TQ), act),
        ],
        compiler_params=pltpu.CompilerParams(
            dimension_semantics=("arbitrary", "arbitrary"), vmem_limit_bytes=VMEM_LIMIT),
        name="in_projection",
    )(h, mods, w_in_p, wg, bg, *tabs)


def _gla_direction(gla_ref, gate_ref, gate_col, o_ref, st_ref, reverse, consts):
    tri, head_lane, blockdiag, sub_masks = consts
    c_, sub = GLA_CHUNK, GLA_SUB
    nsub = c_ // sub
    nchunk = TOK // c_
    g_all = gate_ref[:, gate_col:gate_col + GLA_W]
    b_all = _split_mm_t(tri, g_all)
    order = range(nchunk - 1, -1, -1) if reverse else range(nchunk)
    zero = jnp.zeros((), F32)
    chunks = []
    for c in order:
        r0 = c * c_
        q = gla_ref[r0:r0 + c_, 0:GLA_W]
        k = gla_ref[r0:r0 + c_, GLA_W:2 * GLA_W]
        v = gla_ref[r0:r0 + c_, 2 * GLA_W:3 * GLA_W]
        b = b_all[r0:r0 + c_, :]
        b_end = b[0:1, :] if reverse else b[c_ - 1:c_, :]
        atts = []
        for i in range(nsub):
            t0, t1 = i * sub, (i + 1) * sub
            if reverse:
                ref_b = b[t1:t1 + 1, :] if i < nsub - 1 else jnp.zeros((1, GLA_W), F32)
                k0, k1 = t0, c_
            else:
                ref_b = b[t0 - 1:t0, :] if i > 0 else jnp.zeros((1, GLA_W), F32)
                k0, k1 = 0, t1
            qd = q[t0:t1, :] * jnp.exp(b[t0:t1, :] - ref_b)
            lhs = jnp.concatenate([jnp.where(head_lane[hh], qd, zero) for hh in range(GLA_HEADS)], axis=0)
            kk = k[k0:k1, :] * jnp.exp(ref_b - b[k0:k1, :])
            atts.append((_mm_nt(lhs, kk), k0, k1))
        upd = _mm_tn(v, k * jnp.exp(b_end - b))
        chunks.append(dict(r0=r0, v=v, atts=atts, upd=upd, qs=q * jnp.exp(b), decay=jnp.exp(b_end)))
    for ch in chunks:
        o_parts = []
        for i, (att, k0, k1) in enumerate(ch["atts"]):
            res = _mm(jnp.where(sub_masks[(reverse, i)], att, zero), ch["v"][k0:k1, :])
            o_i = res[(GLA_HEADS - 1) * sub:GLA_HEADS * sub, :]
            for hh in range(GLA_HEADS - 2, -1, -1):
                o_i = jnp.where(head_lane[hh], res[hh * sub:(hh + 1) * sub, :], o_i)
            o_parts.append(o_i)
        ch["o_intra"] = jnp.concatenate(o_parts, axis=0)
    st = st_ref[...]
    for ch in chunks:
        o_ref[ch["r0"]:ch["r0"] + c_, :] = ch["o_intra"] + _mm_nt(ch["qs"], st)
        st = st * ch["decay"] + jnp.where(blockdiag, ch["upd"], zero)
    st_ref[...] = st


def _gla_consts():
    c_, sub = GLA_CHUNK, GLA_SUB
    nsub = c_ // sub
    r = lax.broadcasted_iota(jnp.int32, (TOK, TOK), 0)
    cc = lax.broadcasted_iota(jnp.int32, (TOK, TOK), 1)
    same_chunk = (r // c_) == (cc // c_)
    tri_f = jnp.where(same_chunk & (cc <= r), 1.0, 0.0).astype(F32)
    tri_b = jnp.where(same_chunk & (cc >= r), 1.0, 0.0).astype(F32)
    lane = lax.broadcasted_iota(jnp.int32, (1, GLA_W), 1)
    head_lane = [(lane // GLA_DK) == hh for hh in range(GLA_HEADS)]
    blockdiag = (r // GLA_DV) == (cc // GLA_DK)
    sub_masks = {}
    for reverse in (False, True):
        for i in range(nsub):
            nk = (nsub - i) * sub if reverse else (i + 1) * sub
            rr = lax.broadcasted_iota(jnp.int32, (GLA_HEADS * sub, nk), 0) % sub
            kc = lax.broadcasted_iota(jnp.int32, (GLA_HEADS * sub, nk), 1)
            if reverse:
                sub_masks[(reverse, i)] = kc >= rr
            else:
                sub_masks[(reverse, i)] = (kc - (nk - sub)) <= rr
    return tri_f, tri_b, head_lane, blockdiag, sub_masks


def _gla_kernel(gla_f_ref, gate_f_ref, gla_b_ref, gate_b_ref, of_ref, ob_ref, stf_ref, stb_ref):
    @pl.when(pl.program_id(1) == 0)
    def _():
        stf_ref[...] = jnp.zeros_like(stf_ref)
        stb_ref[...] = jnp.zeros_like(stb_ref)

    tri_f, tri_b, head_lane, blockdiag, sub_masks = _gla_consts()
    for smp in range(gla_f_ref.shape[0]):
        _gla_direction(gla_f_ref.at[smp], gate_f_ref.at[smp], 0, of_ref.at[smp], stf_ref.at[smp], False,
                       (tri_f, head_lane, blockdiag, sub_masks))
        _gla_direction(gla_b_ref.at[smp], gate_b_ref.at[smp], GLA_W, ob_ref.at[smp], stb_ref.at[smp], True,
                       (tri_b, head_lane, blockdiag, sub_masks))


def _gla_call(gla_in, gates):
    bsz, s, _ = gla_in.shape
    nt = s // TOK
    sps = 2 if bsz % 2 == 0 else 1

    def fwd(b, j):
        return (b, j, 0)

    def bwd(b, j):
        return (b, jnp.where(j == 0, 0, nt - j), 0)

    return pl.pallas_call(
        _gla_kernel,
        grid=(bsz // sps, nt),
        in_specs=[
            pl.BlockSpec((sps, TOK, 4 * GLA_W), fwd),
            pl.BlockSpec((sps, TOK, 2 * GLA_W), fwd),
            pl.BlockSpec((sps, TOK, 4 * GLA_W), bwd),
            pl.BlockSpec((sps, TOK, 2 * GLA_W), bwd),
        ],
        out_specs=[
            pl.BlockSpec((sps, TOK, GLA_W), fwd),
            pl.BlockSpec((sps, TOK, GLA_W), bwd),
        ],
        out_shape=[jax.ShapeDtypeStruct((bsz, s, GLA_W), F32)] * 2,
        scratch_shapes=[pltpu.VMEM((sps, GLA_W, GLA_W), F32), pltpu.VMEM((sps, GLA_W, GLA_W), F32)],
        compiler_params=pltpu.CompilerParams(
            dimension_semantics=("arbitrary", "arbitrary"), vmem_limit_bytes=VMEM_LIMIT),
        name="gla_scan",
    )(gla_in, gates, gla_in, gates)


def _diff_kernel(q_ref, k_ref, vt_ref, lam_ref, lam0_ref, g_ref, o_ref, *, n_ctx):
    lam_p = lam_ref[...]
    lam = (jnp.exp(jnp.sum(lam_p[0:1, :] * lam_p[1:2, :], axis=-1, keepdims=True))
           - jnp.exp(jnp.sum(lam_p[2:3, :] * lam_p[3:4, :], axis=-1, keepdims=True))
           + lam0_ref[...])
    lane = lax.broadcasted_iota(jnp.int32, (1, DIFF_W), 1)

    def attend(nk, parts):
        q = q_ref[...]
        step = nk // parts

        def scores(idx):
            lo = idx * DIFF_DH
            qm = jnp.where((lane >= lo) & (lane < lo + DIFF_DH), q, jnp.zeros_like(q))
            return [_mm_nt(k_ref[i * step:(i + 1) * step, :], qm).astype(MXU_DTYPE)
                    for i in range(parts)]

        def probs(sts):
            m = _col_max(sts[0])
            for st in sts[1:]:
                m = jnp.maximum(m, _col_max(st))
            return [jnp.exp2(st - m) for st in sts]

        def values(idx, ps):
            acc = None
            for i, p in enumerate(ps):
                pv = _mm(vt_ref[idx // 2, :, i * step:(i + 1) * step], p)
                acc = pv if acc is None else acc + pv
            return acc[0:DIFF_DV, :] / acc[DIFF_DV:DIFF_DV + 1, :]

        n_str = 2 * DIFF_HEADS
        sts = {0: scores(0), 1: scores(1)}
        ps = {0: probs(sts.pop(0))}
        comps = []
        for idx in range(n_str):
            if idx + 2 < n_str:
                sts[idx + 2] = scores(idx + 2)
            if idx + 1 < n_str:
                ps[idx + 1] = probs(sts.pop(idx + 1))
            comps.append(values(idx, ps.pop(idx)))
        heads = []
        for hh in range(DIFF_HEADS):
            o_h = comps[2 * hh] - lam * comps[2 * hh + 1]
            ms = jnp.mean(o_h * o_h, axis=0, keepdims=True)
            heads.append(o_h * lax.rsqrt(ms + LN_EPS))
        out = jnp.concatenate(heads, axis=0).T
        o_ref[...] = (out * g_ref[...]).astype(o_ref.dtype)

    j = pl.program_id(1)

    @pl.when(j == 0)
    def _():
        attend(n_ctx, 1)

    @pl.when(j > 0)
    def _():
        attend(k_ref.shape[0], 2)


def _col_max(x):
    r = x.shape[0]
    slab = 16
    while r % (2 * slab) == 0 and r // slab > 32:
        slab *= 2
    acc = x[0:slab, :]
    for i in range(1, r // slab):
        acc = jnp.maximum(acc, x[i * slab:(i + 1) * slab, :])
    return jnp.max(acc, axis=0, keepdims=True)


def _diff_call(dqk, dvt, lam_p, lam0, g_eff):
    bsz, s, _ = dqk.shape
    nt = s // TOK
    return pl.pallas_call(
        functools.partial(_diff_kernel, n_ctx=TOK),
        grid=(bsz, nt),
        in_specs=[
            pl.BlockSpec((None, TOK, DIFF_W), lambda b, j: (b, j, 0)),
            pl.BlockSpec((None, s, DIFF_W), lambda b, j: (b, 0, 1)),
            pl.BlockSpec((None, DIFF_HEADS, DIFF_VROWS, s), lambda b, j: (b, 0, 0, 0)),
            pl.BlockSpec((4, DIFF_DH), lambda b, j: (0, 0)),
            pl.BlockSpec((1, 1), lambda b, j: (0, 0)),
            pl.BlockSpec((1, DIFF_W), lambda b, j: (0, 0)),
        ],
        out_specs=pl.BlockSpec((None, TOK, DIFF_W), lambda b, j: (b, j, 0)),
        out_shape=jax.ShapeDtypeStruct((bsz, s, DIFF_W), MXU_DTYPE),
        compiler_params=pltpu.CompilerParams(
            dimension_semantics=("arbitrary", "arbitrary"), vmem_limit_bytes=VMEM_LIMIT),
        name="diff_attention",
    )(dqk, dqk, dvt, lam_p, lam0, g_eff)


def _swa_kernel(q_ref, k_ref, vt_ref, sink_ref, o_ref, *, n_ctx, n_lat):
    j = pl.program_id(1)
    tq = SWA_TQ
    n_win = 3
    n_ctx_t = n_ctx // tq
    s_rows = n_ctx + n_lat
    lane = lax.broadcasted_iota(jnp.int32, (1, LANES), 1)
    low = lane < SWA_DH
    q_per_kv = SWA_HEADS // SWA_KV_HEADS
    chains = [(sub, grp) for sub in range(TOK // tq) for grp in range(SWA_KV_HEADS)]

    def window(sub):
        p0 = j * TOK + sub * tq - n_ctx
        start = pl.multiple_of(jnp.clip(n_ctx + p0 - tq, 0, s_rows - n_win * tq), tq)
        return p0, start

    def scores(idx):
        sub, grp = chains[idx]
        _, start = window(sub)
        blocks = []
        for r in range(q_per_kv):
            hh = grp * q_per_kv + r
            qp = q_ref[sub * tq:(sub + 1) * tq, (hh // 2) * LANES:(hh // 2 + 1) * LANES]
            blocks.append(jnp.where(low if hh % 2 == 0 else ~low, qp, jnp.zeros_like(qp)))
        lhs = jnp.concatenate(blocks, axis=0)
        st_c = _mm_nt(k_ref[0:n_ctx, grp * LANES:(grp + 1) * LANES], lhs)
        st_l = _mm_nt(k_ref[pl.ds(start, n_win * tq), grp * LANES:(grp + 1) * LANES], lhs)
        return st_c, st_l

    def probs(idx, st):
        sub, grp = chains[idx]
        st_c, st_l = st
        p0, start = window(sub)
        pos_q = p0 + lax.broadcasted_iota(jnp.int32, (1, tq), 1)
        pos_k = start - n_ctx + lax.broadcasted_iota(jnp.int32, (n_win * tq, 1), 0)
        valid = (jnp.abs(pos_k - pos_q) <= WINDOW) & (pos_k >= 0) & (p0 >= 0)
        bias = jnp.where(valid, 0.0, -jnp.inf).astype(F32)
        st_l = st_l + jnp.concatenate([bias] * q_per_kv, axis=1)
        st_c = st_c.astype(MXU_DTYPE)
        st_l = st_l.astype(MXU_DTYPE)
        snk = jnp.concatenate([sink_ref[grp * q_per_kv + r:grp * q_per_kv + r + 1, :]
                               for r in range(q_per_kv)], axis=1)
        m = jnp.maximum(jnp.maximum(_col_max(st_c), _col_max(st_l)).astype(F32), snk)
        mb = m.astype(MXU_DTYPE)
        return jnp.exp2(st_c - mb), jnp.exp2(st_l - mb), jnp.exp2(snk - mb.astype(F32))

    def values(idx, pr):
        sub, grp = chains[idx]
        p_c, p_l, p_snk = pr
        _, start = window(sub)
        t0 = start // tq
        acc = None
        for i in range(n_ctx_t):
            pv = _mm(vt_ref[grp, i], p_c[i * tq:(i + 1) * tq, :])
            acc = pv if acc is None else acc + pv
        for i in range(n_win):
            acc = acc + _mm(vt_ref[grp, t0 + i], p_l[i * tq:(i + 1) * tq, :])
        o = acc[0:SWA_DH, :] / (acc[SWA_DH:SWA_DH + 1, :] + p_snk)
        for hp in range(q_per_kv // 2):
            pair = grp * (q_per_kv // 2) + hp
            both = jnp.concatenate([o[:, (2 * hp) * tq:(2 * hp + 1) * tq],
                                    o[:, (2 * hp + 1) * tq:(2 * hp + 2) * tq]], axis=0)
            o_ref[sub * tq:(sub + 1) * tq, pair * LANES:(pair + 1) * LANES] = both.T.astype(o_ref.dtype)

    n_ch = len(chains)
    sts = {0: scores(0), 1: scores(1)}
    prs = {0: probs(0, sts.pop(0))}
    for idx in range(n_ch):
        if idx + 2 < n_ch:
            sts[idx + 2] = scores(idx + 2)
        if idx + 1 < n_ch:
            prs[idx + 1] = probs(idx + 1, sts.pop(idx + 1))
        values(idx, prs.pop(idx))


def _swa_call(sq, sk, svt, sink_tile, n_ctx):
    bsz, s, _ = sq.shape
    return pl.pallas_call(
        functools.partial(_swa_kernel, n_ctx=n_ctx, n_lat=s - n_ctx),
        grid=(bsz, s // TOK),
        in_specs=[
            pl.BlockSpec((None, TOK, SWA_Q_W), lambda b, j: (b, j, 0)),
            pl.BlockSpec((None, s, 2 * LANES), lambda b, j: (b, 0, 0)),
            pl.BlockSpec((None, SWA_KV_HEADS, s // SWA_TQ, SWA_VROWS, SWA_TQ), lambda b, j: (b, 0, 0, 0, 0)),
            pl.BlockSpec((SWA_HEADS, LANES), lambda b, j: (0, 0)),
        ],
        out_specs=pl.BlockSpec((None, TOK, SWA_Q_W), lambda b, j: (b, j, 0)),
        out_shape=jax.ShapeDtypeStruct((bsz, s, SWA_Q_W), MXU_DTYPE),
        compiler_params=pltpu.CompilerParams(
            dimension_semantics=("arbitrary", "arbitrary"), vmem_limit_bytes=VMEM_LIMIT),
        name="windowed_gqa",
    )(sq, sk, svt, sink_tile)


def _outproj_kernel(of_ref, ob_ref, og_ref, diff_ref, swa_ref, h_ref, mod_ref, w_ref,
                    gng_ref, lng_ref, lnb_ref, o_ref, *, alpha):
    o = of_ref[...] + ob_ref[...]
    ms = _split_mm(o * o, _group_mean_matrix(GLA_W, GLA_DV))
    gla = o * lax.rsqrt(ms + LN_EPS) * gng_ref[...] * _silu(og_ref[...])
    y = (_mm(gla, w_ref[0:GLA_W, :])
         + _mm(diff_ref[...], w_ref[GLA_W:GLA_W + DIFF_W, :])
         + _mm(swa_ref[...], w_ref[GLA_W + DIFF_W:, :]))
    z = alpha * h_ref[...] + mod_ref[2:3, :] * y
    o_ref[...] = _ln_plain(z) * lng_ref[...] + lnb_ref[...]


def _outproj_call(o_f, o_b, gla_in, diff_o, swa_o, h, mods, layer, ctx_row, w_out, gng, lng, lnb, alpha):
    bsz, s, d = h.shape
    nt = s // TOK
    tile = lambda w: pl.BlockSpec((None, TOK, w), lambda b, j: (b, j, 0))
    const = lambda shape: pl.BlockSpec(shape, lambda b, j: (0,) * len(shape))
    return pl.pallas_call(
        functools.partial(_outproj_kernel, alpha=alpha),
        grid=(bsz, nt),
        in_specs=[
            tile(GLA_W), tile(GLA_W),
            pl.BlockSpec((None, TOK, GLA_W), lambda b, j: (b, j, 3)),
            tile(DIFF_W), tile(SWA_Q_W), tile(d),
            pl.BlockSpec((None, None, 6, d), functools.partial(_mod_index, layer=layer, ctx_row=ctx_row)),
            const((d, d)), const((1, GLA_W)), const((1, d)), const((1, d)),
        ],
        out_specs=tile(d),
        out_shape=jax.ShapeDtypeStruct((bsz, s, d), F32),
        compiler_params=pltpu.CompilerParams(
            dimension_semantics=("arbitrary", "arbitrary"), vmem_limit_bytes=VMEM_LIMIT),
        name="out_projection",
    )(o_f, o_b, gla_in, diff_o, swa_o, h, mods, w_out, gng, lng, lnb)


def _ffn_kernel(h_ref, mod_ref, wg_ref, wu_ref, wd_ref, lng_ref, lnb_ref, o_ref, *, alpha, skip_first):
    def body():
        x = h_ref[...]
        u = (_ln_plain(x) * (1.0 + mod_ref[4:5, :]) + mod_ref[3:4, :]).astype(MXU_DTYPE)
        a = _silu(_mm(u, wg_ref[...])) * _mm(u, wu_ref[...])
        f = _mm(a, wd_ref[...])
        z = alpha * x + mod_ref[5:6, :] * f
        o_ref[...] = _ln_plain(z) * lng_ref[...] + lnb_ref[...]

    if skip_first:
        pl.when(pl.program_id(1) > 0)(body)
    else:
        body()


def _ffn_call(h, mods, layer, ctx_row, w_g, w_u, w_d, lng, lnb, alpha, latent_only):
    bsz, s, d = h.shape
    nt = s // TOK
    hid = w_g.shape[1]
    single = pl.Buffered(1)
    const = lambda shape: pl.BlockSpec(shape, lambda b, j: (0,) * len(shape), pipeline_mode=single)
    if latent_only:
        out_spec = pl.BlockSpec((None, TOK, d), lambda b, j: (b, jnp.maximum(j - 1, 0), 0))
        out_rows = s - TOK
    else:
        out_spec = pl.BlockSpec((None, TOK, d), lambda b, j: (b, j, 0))
        out_rows = s
    return pl.pallas_call(
        functools.partial(_ffn_kernel, alpha=alpha, skip_first=latent_only),
        grid=(bsz, nt),
        in_specs=[
            pl.BlockSpec((None, TOK, d), lambda b, j: (b, j, 0)),
            pl.BlockSpec((None, None, 6, d), functools.partial(_mod_index, layer=layer, ctx_row=ctx_row)),
            const((d, hid)), const((d, hid)), const((hid, d)),
            pl.BlockSpec((1, d), lambda b, j: (0, 0)), pl.BlockSpec((1, d), lambda b, j: (0, 0)),
        ],
        out_specs=out_spec,
        out_shape=jax.ShapeDtypeStruct((bsz, out_rows, d), F32),
        compiler_params=pltpu.CompilerParams(
            dimension_semantics=("arbitrary", "arbitrary"), vmem_limit_bytes=VMEM_LIMIT),
        name="swiglu_ffn",
    )(h, mods, w_g, w_u, w_d, lng, lnb)


POST_TOK = 2 * TOK
MXU_TILE = 256


def _ffn_chunk_bounds(hid):
    cut = -(-(hid // MXU_TILE) // 2) * MXU_TILE
    return ((0, cut), (cut, hid))


def _post_kernel(of_ref, ob_ref, og_ref, diff_ref, swa_ref, h_ref, mod_ref, wo_ref, wg_ref, wu_ref, wd_ref,
                 gng_ref, lng_ref, lnb_ref, o_ref, *, alpha, tiles_per_sample, ctx_row):
    t = pl.program_id(0)
    halves = POST_TOK // TOK
    hid = wg_ref.shape[1]
    bounds = _ffn_chunk_bounds(hid)
    gm = _group_mean_matrix(GLA_W, GLA_DV)
    st = [dict() for _ in range(halves)]

    def out_proj(x):
        rows = slice(x * TOK, (x + 1) * TOK)
        g = t * halves + x
        row = jnp.where(g % tiles_per_sample == 0, ctx_row, g // tiles_per_sample)
        st[x]["mod"] = mod_ref[row]
        o = of_ref[rows, :] + ob_ref[rows, :]
        ms = _split_mm(o * o, gm)
        gla = o * lax.rsqrt(ms + LN_EPS) * gng_ref[...] * _silu(og_ref[rows, :])
        st[x]["y"] = (_mm(gla, wo_ref[0:GLA_W, :])
                      + _mm(diff_ref[rows, :], wo_ref[GLA_W:GLA_W + DIFF_W, :])
                      + _mm(swa_ref[rows, :], wo_ref[GLA_W + DIFF_W:, :]))

    def mid_norm(x):
        rows = slice(x * TOK, (x + 1) * TOK)
        mod = st[x]["mod"]
        z = alpha * h_ref[rows, :] + mod[2:3, :] * st[x].pop("y")
        hm = _ln_plain(z) * lng_ref[0:1, :] + lnb_ref[0:1, :]
        st[x]["hm"] = hm
        st[x]["u"] = (_ln_plain(hm) * (1.0 + mod[4:5, :]) + mod[3:4, :]).astype(MXU_DTYPE)

    def gate_up(x, c):
        u = st[x]["u"]
        c0, c1 = bounds[c]
        st[x]["g", c] = jnp.dot(u, wg_ref[:, c0:c1], preferred_element_type=F32)
        st[x]["p", c] = jnp.dot(u, wu_ref[:, c0:c1], preferred_element_type=F32)

    def act(x, c):
        st[x]["a", c] = (_silu(st[x].pop(("g", c))) * st[x].pop(("p", c))).astype(MXU_DTYPE)

    def down(x, c):
        c0, c1 = bounds[c]
        f = jnp.dot(st[x].pop(("a", c)), wd_ref[c0:c1, :], preferred_element_type=F32)
        st[x]["f"] = f if c == 0 else st[x]["f"] + f

    def final(x):
        rows = slice(x * TOK, (x + 1) * TOK)
        z = alpha * st[x]["hm"] + st[x]["mod"][5:6, :] * st[x]["f"]
        o_ref[rows, :] = _ln_plain(z) * lng_ref[1:2, :] + lnb_ref[1:2, :]

    a_, b_ = 0, 1
    out_proj(a_)
    out_proj(b_)
    mid_norm(a_)
    gate_up(a_, 0)
    mid_norm(b_)
    gate_up(a_, 1)
    act(a_, 0)
    gate_up(b_, 0)
    act(a_, 1)
    down(a_, 0)
    gate_up(b_, 1)
    act(b_, 0)
    down(a_, 1)
    act(b_, 1)
    down(b_, 0)
    final(a_)
    down(b_, 1)
    final(b_)


def _post_call(o_f, o_b, gla_in, diff_o, swa_o, h, mods, layer, ctx_row, w_out, w_g, w_u, w_d, gng, lng, lnb, alpha):
    bsz, s, d = h.shape
    rows = bsz * s
    assert rows % POST_TOK == 0
    hid = w_g.shape[1]
    flat = lambda a: a.reshape(rows, a.shape[-1])
    tile = lambda w: pl.BlockSpec((POST_TOK, w), lambda t: (t, 0))
    single = pl.Buffered(1)
    const = lambda shape: pl.BlockSpec(shape, lambda t: (0,) * len(shape), pipeline_mode=single)
    out = pl.pallas_call(
        functools.partial(_post_kernel, alpha=alpha, tiles_per_sample=s // TOK, ctx_row=ctx_row),
        grid=(rows // POST_TOK,),
        in_specs=[
            tile(GLA_W), tile(GLA_W),
            pl.BlockSpec((POST_TOK, GLA_W), lambda t: (t, 3)),
            tile(DIFF_W), tile(SWA_Q_W), tile(d),
            pl.BlockSpec((None,) + mods.shape[1:], lambda t: (layer, 0, 0, 0)),
            const((d, d)), const((d, hid)), const((d, hid)), const((hid, d)),
            pl.BlockSpec((1, GLA_W), lambda t: (0, 0)),
            pl.BlockSpec((2, d), lambda t: (0, 0)), pl.BlockSpec((2, d), lambda t: (0, 0)),
        ],
        out_specs=tile(d),
        out_shape=jax.ShapeDtypeStruct((rows, d), F32),
        compiler_params=pltpu.CompilerParams(
            dimension_semantics=("arbitrary",), vmem_limit_bytes=POST_VMEM_LIMIT),
        name="post_attention",
    )(flat(o_f), flat(o_b), flat(gla_in), flat(diff_o), flat(swa_o), flat(h), mods,
      w_out, w_g, w_u, w_d, gng, lng, lnb)
    return out.reshape(bsz, s, d)


def _rope_tables(rows, n_ctx, dim):
    row = jnp.repeat(jnp.arange(rows, dtype=F32), GRID_W)
    col = jnp.tile(jnp.arange(GRID_W, dtype=F32), rows)
    n_freq = dim // 4
    inv = jnp.power(ROPE_BASE, -jnp.arange(n_freq, dtype=F32) / n_freq)
    ang = jnp.concatenate([row[:, None] * inv, col[:, None] * inv], axis=-1)
    cos, sin = jnp.cos(ang), jnp.sin(ang)
    reps = LANES // dim
    cos_t = jnp.tile(jnp.concatenate([cos, cos], axis=-1), (1, reps))
    sin_t = jnp.tile(jnp.concatenate([-sin, sin], axis=-1), (1, reps))
    cos_t = jnp.concatenate([jnp.ones((n_ctx, LANES), F32), cos_t], axis=0)
    sin_t = jnp.concatenate([jnp.zeros((n_ctx, LANES), F32), sin_t], axis=0)
    return cos_t, sin_t


def _reorder_w_in(w_in):
    o = 0
    parts = {}
    for name, width in (("gq", GLA_W), ("gk", GLA_W), ("gv", GLA_W), ("go", GLA_W),
                        ("zf", GLA_GATE_RANK), ("zb", GLA_GATE_RANK),
                        ("dq", DIFF_W), ("dk", DIFF_W), ("dv", DIFF_W),
                        ("sq", SWA_Q_W), ("sk", SWA_KV_W), ("sv", SWA_KV_W)):
        parts[name] = w_in[:, o:o + width]
        o += width
    cols = [parts[n] for n in ("gq", "gk", "gv", "go", "dq", "dk", "dv", "sq", "sk", "sv", "zf", "zb")]
    pad = IN_PAD_W - sum(c.shape[1] for c in cols)
    cols.append(jnp.zeros((w_in.shape[0], pad), w_in.dtype))
    return jnp.concatenate(cols, axis=1)


def kernel(x, c, ctx, c_ctx, w_ada, b_ada, w_in, w_gla_gate, b_gla_gate, gla_norm_g, diff_lambda,
           diff_norm_g, swa_sink, w_out, ln_g, ln_b, w_ffn_gate, w_ffn_up, w_ffn_down):
    bsz, n_lat, d = x.shape
    n_ctx = ctx.shape[1]
    depth = w_ada.shape[0]
    assert n_ctx == TOK and n_lat % TOK == 0 and n_lat % GRID_W == 0
    alpha = (2.0 * depth) ** 0.25
    wdt = MXU_DTYPE

    cond_rows = -(-(bsz + 1) // 8) * 8
    cond =jnp.concatenate([c, c_ctx[None, :], jnp.zeros((cond_rows - bsz - 1, d), F32)], axis=0)
    mods = _mods_call(cond, w_ada, b_ada).reshape(depth, cond_rows, 6, d)
    ctx_row = bsz

    rows = n_lat // GRID_W
    tabs = _rope_tables(rows, n_ctx, DIFF_DH) + _rope_tables(rows, n_ctx, SWA_DH)

    h = jnp.concatenate([ctx, x], axis=1)
    for layer in range(depth):
        last = layer == depth - 1
        lam_init = 0.8 - 0.6 * math.exp(-0.3 * layer)
        w_in_p = _reorder_w_in(w_in[layer]).astype(wdt)
        wg = jnp.zeros((LANES, 2 * GLA_W), F32)
        wg = wg.at[0:GLA_GATE_RANK, 0:GLA_W].set(w_gla_gate[layer, 0])
        wg = wg.at[GLA_GATE_RANK:2 * GLA_GATE_RANK, GLA_W:].set(w_gla_gate[layer, 1])
        bg = b_gla_gate[layer].reshape(1, 2 * GLA_W)

        gla_in, gates, dqk, dvt, sq, sk, svt = _inproj_call(h, mods, layer, ctx_row, w_in_p, wg.astype(wdt), bg, tabs)
        o_f, o_b = _gla_call(gla_in, gates)
        diff_o = _diff_call(dqk, dvt, diff_lambda[layer], jnp.full((1, 1), lam_init, F32),
                            (jnp.tile(diff_norm_g[layer], DIFF_HEADS) * (1.0 - lam_init)).reshape(1, DIFF_W))
        swa_o = _swa_call(sq, sk, svt,
                          jnp.broadcast_to(swa_sink[layer][:, None] * LOG2_E, (SWA_HEADS, LANES)), n_ctx)
        gng = jnp.tile(gla_norm_g[layer], GLA_HEADS).reshape(1, GLA_W)
        w_o, w_g, w_u, w_d = (w_out[layer].astype(wdt), w_ffn_gate[layer].astype(wdt),
                              w_ffn_up[layer].astype(wdt), w_ffn_down[layer].astype(wdt))
        if last:
            h_mid = _outproj_call(o_f, o_b, gla_in, diff_o, swa_o, h, mods, layer, ctx_row, w_o, gng,
                                  ln_g[layer, 0].reshape(1, d), ln_b[layer, 0].reshape(1, d), alpha)
            h = _ffn_call(h_mid, mods, layer, ctx_row, w_g, w_u, w_d,
                          ln_g[layer, 1].reshape(1, d), ln_b[layer, 1].reshape(1, d), alpha, True)
        else:
            h = _post_call(o_f, o_b, gla_in, diff_o, swa_o, h, mods, layer, ctx_row, w_o, w_g, w_u, w_d,
                           gng, ln_g[layer], ln_b[layer], alpha)
    return h
```

```python
import functools
import math

import jax
import jax.numpy as jnp
from jax import lax
from jax.experimental import pallas as pl
from jax.experimental.pallas import tpu as pltpu

F32 = jnp.float32
MXU_DTYPE = jnp.bfloat16

GRID_W = 64
GLA_HEADS, GLA_DK, GLA_DV = 4, 64, 64
GLA_GATE_RANK = 16
GLA_GATE_NORM = 16.0
GLA_CHUNK = 64
GLA_SUB = 16
DIFF_HEADS, DIFF_DH = 4, 32
DIFF_DV = 2 * DIFF_DH
SWA_HEADS, SWA_KV_HEADS, SWA_DH = 8, 2, 64
WINDOW = 128
ROPE_BASE = 10000.0
LN_EPS = 1e-6

GLA_W = GLA_HEADS * GLA_DK
DIFF_W = DIFF_HEADS * 2 * DIFF_DH
SWA_Q_W = SWA_HEADS * SWA_DH
SWA_KV_W = SWA_KV_HEADS * SWA_DH
LANES = 128
TOK = 256
SWA_TQ = 128
DIFF_VROWS = DIFF_DV + 16
SWA_VROWS = SWA_DH + 16
LOG2_E = math.log2(math.e)
VMEM_LIMIT = 52 * 1024 * 1024
POST_VMEM_LIMIT = 58 * 1024 * 1024


def _mm(a, b):
    return jnp.dot(a.astype(MXU_DTYPE), b.astype(MXU_DTYPE), preferred_element_type=F32)


def _mm_nt(a, b):
    return lax.dot_general(a.astype(MXU_DTYPE), b.astype(MXU_DTYPE),
                           (((1,), (1,)), ((), ())), preferred_element_type=F32)


def _mm_tn(a, b):
    return lax.dot_general(a.astype(MXU_DTYPE), b.astype(MXU_DTYPE),
                           (((0,), (0,)), ((), ())), preferred_element_type=F32)


def _split_mm(a, b_exact):
    if MXU_DTYPE == F32:
        return jnp.dot(a, b_exact.astype(F32), preferred_element_type=F32)
    hi = a.astype(jnp.bfloat16)
    r1 = a - hi.astype(F32)
    mid = r1.astype(jnp.bfloat16)
    lo = (r1 - mid.astype(F32)).astype(jnp.bfloat16)
    b = b_exact.astype(jnp.bfloat16)
    return (jnp.dot(hi, b, preferred_element_type=F32)
            + jnp.dot(mid, b, preferred_element_type=F32)
            + jnp.dot(lo, b, preferred_element_type=F32))


def _split_mm_t(b_exact, a):
    if MXU_DTYPE == F32:
        return jnp.dot(b_exact.astype(F32), a, preferred_element_type=F32)
    hi = a.astype(jnp.bfloat16)
    r1 = a - hi.astype(F32)
    mid = r1.astype(jnp.bfloat16)
    lo = (r1 - mid.astype(F32)).astype(jnp.bfloat16)
    b = b_exact.astype(jnp.bfloat16)
    return (jnp.dot(b, hi, preferred_element_type=F32)
            + jnp.dot(b, mid, preferred_element_type=F32)
            + jnp.dot(b, lo, preferred_element_type=F32))


def _ln_plain(x):
    mu = jnp.mean(x, axis=-1, keepdims=True)
    xc = x - mu
    var = jnp.mean(xc * xc, axis=-1, keepdims=True)
    return xc * lax.rsqrt(var + LN_EPS)


def _silu(x):
    return x / (1.0 + jnp.exp(-x))


def _group_mean_matrix(width, group):
    r = lax.broadcasted_iota(jnp.int32, (width, width), 0) // group
    c = lax.broadcasted_iota(jnp.int32, (width, width), 1) // group
    return jnp.where(r == c, 1.0 / group, 0.0).astype(F32)


def _mods_kernel(a_ref, w_ref, b_ref, o_ref):
    a = a_ref[...]
    o_ref[...] = jnp.dot(_silu(a), w_ref[...], preferred_element_type=F32,
                         precision=lax.Precision.HIGHEST) + b_ref[...]


def _mods_call(cond, w_ada, b_ada):
    depth, d, width = w_ada.shape
    rows = cond.shape[0]
    bn = 1536
    return pl.pallas_call(
        _mods_kernel,
        grid=(depth, width // bn),
        in_specs=[
            pl.BlockSpec((rows, d), lambda l, n: (0, 0)),
            pl.BlockSpec((None, d, bn), lambda l, n: (l, 0, n)),
            pl.BlockSpec((None, 1, bn), lambda l, n: (l, 0, n)),
        ],
        out_specs=pl.BlockSpec((None, rows, bn), lambda l, n: (l, 0, n)),
        out_shape=jax.ShapeDtypeStruct((depth, rows, width), F32),
        compiler_params=pltpu.CompilerParams(vmem_limit_bytes=VMEM_LIMIT),
        name="adaln_mods",
    )(cond, w_ada, b_ada.reshape(depth, 1, width))


C_GQ, C_GK, C_GV, C_GO = 0, 256, 512, 768
C_DQ, C_DK, C_DV = 1024, 1280, 1536
C_SQ, C_SK, C_SV = 1792, 2304, 2432
C_Z = 2560
IN_PAD_W = 2688


def _rope(x, cos, sin_signed, half):
    outs = []
    lane = lax.broadcasted_iota(jnp.int32, (1, LANES), 1)
    first = (lane % (2 * half)) < half
    for s in range(x.shape[1] // LANES):
        xs = x[:, s * LANES:(s + 1) * LANES]
        up = pltpu.roll(xs, LANES - half, axis=1)
        dn = pltpu.roll(xs, half, axis=1)
        outs.append(xs * cos + jnp.where(first, up, dn) * sin_signed)
    return outs[0] if len(outs) == 1 else jnp.concatenate(outs, axis=1)


def _inproj_kernel(*refs, split_input):
    j = pl.program_id(1)
    if split_input:
        x_ref, ctx_ref = refs[0:2]
        refs = refs[2:]
        hout_ref = refs[-1]
        refs = refs[:-1]
        sps = x_ref.shape[0]
        for n in range(sps):
            hout_ref[n] = jnp.where(j == 0, ctx_ref[n], x_ref[n])
        h_ref = hout_ref
    else:
        h_ref = refs[0]
        refs = refs[1:]
        sps = h_ref.shape[0]
    (mod_ref, w_ref, wg_ref, bg_ref, cd_ref, sd_ref, cs_ref, ss_ref,
     gla_ref, gate_ref, dqk_ref, dvt_ref, sq_ref, sk_ref, svt_ref) = refs
    cd, sd, cs, ss = cd_ref[...], sd_ref[...], cs_ref[...], ss_ref[...]
    lane = lax.broadcasted_iota(jnp.int32, (1, LANES), 1)
    low = lane < SWA_DH
    ones_d = jnp.where(lax.broadcasted_iota(jnp.int32, (DIFF_VROWS - DIFF_DV, TOK), 0) == 0, 1.0, 0.0)
    ones_s = jnp.where(lax.broadcasted_iota(jnp.int32, (SWA_VROWS - SWA_DH, SWA_TQ), 0) == 0, 1.0, 0.0)
    st = [dict() for _ in range(sps)]

    def norm(n):
        mod = mod_ref[jnp.where(j == 0, 0, n)]
        st[n]["u"] = (_ln_plain(h_ref[n]) * (1.0 + mod[1:2, :]) + mod[0:1, :]).astype(MXU_DTYPE)

    def project(n, name, c0, c1):
        st[n][name] = jnp.dot(st[n]["u"], w_ref[:, c0:c1], preferred_element_type=F32)

    def gate_out(n):
        gpre = _mm(st[n].pop("z"), wg_ref[...]) + bg_ref[...]
        logsig = jnp.minimum(gpre, 0.0) - jnp.log(1.0 + jnp.exp(-jnp.abs(gpre)))
        gate_ref[n] = logsig * (1.0 / GLA_GATE_NORM)

    def diff_out(n):
        pd = st[n].pop("d")
        dq = _rope(pd[:, 0:DIFF_W], cd, sd, DIFF_DH // 2) * (DIFF_DH ** -0.5 * LOG2_E)
        dk = _rope(pd[:, DIFF_W:2 * DIFF_W], cd, sd, DIFF_DH // 2)
        dqk_ref[n, :, 0:DIFF_W] = dq.astype(dqk_ref.dtype)
        dqk_ref[n, :, DIFF_W:2 * DIFF_W] = dk.astype(dqk_ref.dtype)
        vt = pd[:, 2 * DIFF_W:3 * DIFF_W].T
        for hh in range(DIFF_HEADS):
            dvt_ref[n, hh, 0:DIFF_DV, :] = vt[hh * DIFF_DV:(hh + 1) * DIFF_DV, :].astype(dvt_ref.dtype)
            dvt_ref[n, hh, DIFF_DV:DIFF_VROWS, :] = ones_d.astype(dvt_ref.dtype)

    def swa_out(n):
        ps = st[n].pop("s")
        sq = _rope(ps[:, 0:SWA_Q_W], cs, ss, SWA_DH // 2) * (SWA_DH ** -0.5 * LOG2_E)
        sq_ref[n] = sq.astype(sq_ref.dtype)
        sk = _rope(ps[:, SWA_Q_W:SWA_Q_W + SWA_KV_W], cs, ss, SWA_DH // 2)
        sw = pltpu.roll(sk, SWA_DH, axis=1)
        sk_ref[n, :, 0:LANES] = jnp.where(low, sk, sw).astype(sk_ref.dtype)
        sk_ref[n, :, LANES:2 * LANES] = jnp.where(low, sw, sk).astype(sk_ref.dtype)
        svt = ps[:, SWA_Q_W + SWA_KV_W:SWA_Q_W + 2 * SWA_KV_W].T
        for grp in range(SWA_KV_HEADS):
            for i in range(TOK // SWA_TQ):
                svt_ref[n, grp, i, 0:SWA_DH, :] = svt[grp * SWA_DH:(grp + 1) * SWA_DH,
                                                      i * SWA_TQ:(i + 1) * SWA_TQ].astype(svt_ref.dtype)
                svt_ref[n, grp, i, SWA_DH:SWA_VROWS, :] = ones_s.astype(svt_ref.dtype)

    def gla_out(n):
        pg = st[n].pop("g")
        gla_ref[n, :, 0:GLA_W] = pg[:, 0:GLA_W] * (GLA_DK ** -0.5)
        gla_ref[n, :, GLA_W:4 * GLA_W] = pg[:, GLA_W:4 * GLA_W]

    def steps(n):
        return [
            lambda: norm(n),
            lambda: project(n, "z", C_Z, C_Z + LANES),
            lambda: project(n, "d", C_DQ, C_DV + DIFF_W),
            lambda: gate_out(n),
            lambda: project(n, "s", C_SQ, C_SV + SWA_KV_W),
            lambda: diff_out(n),
            lambda: project(n, "g", C_GQ, C_GO + GLA_W),
            lambda: swa_out(n),
            lambda: gla_out(n),
        ]

    lag = 4
    plans = [steps(n) for n in range(sps)]
    for tick in range(len(plans[0]) + lag * (sps - 1)):
        for n in range(sps):
            i = tick - lag * n
            if 0 <= i < len(plans[n]):
                plans[n][i]()


def _inproj_call(h, mods, layer, ctx_row, w_in_p, wg, bg, tabs):
    split_input = isinstance(h, tuple)
    if split_input:
        x, ctx = h
        bsz, n_lat, d = x.shape
        assert ctx.shape[1] == TOK
        s = n_lat + TOK
    else:
        bsz, s, d = h.shape
    nt = s // TOK
    act = MXU_DTYPE
    sps = 2 if (bsz % 2 == 0 and ctx_row % 2 == 0) else 1
    tab_spec = pl.BlockSpec((TOK, LANES), lambda b, j: (j, 0))

    def mod_index(b, j):
        return (layer, jnp.where(j == 0, ctx_row // sps, b), 0, 0)

    if split_input:
        stream_in = [x, ctx]
        stream_specs = [pl.BlockSpec((sps, TOK, d), lambda b, j: (b, jnp.maximum(j - 1, 0), 0)),
                        pl.BlockSpec((sps, TOK, d), lambda b, j: (b, 0, 0))]
        extra_specs = [pl.BlockSpec((sps, TOK, d), lambda b, j: (b, j, 0))]
        extra_shapes = [jax.ShapeDtypeStruct((bsz, s, d), F32)]
    else:
        stream_in = [h]
        stream_specs = [pl.BlockSpec((sps, TOK, d), lambda b, j: (b, j, 0))]
        extra_specs, extra_shapes = [], []

    return pl.pallas_call(
        functools.partial(_inproj_kernel, split_input=split_input),
        grid=(bsz // sps, nt),
        in_specs=stream_specs + [
            pl.BlockSpec((None, sps, 6, d), mod_index),
            pl.BlockSpec((d, IN_PAD_W), lambda b, j: (0, 0)),
            pl.BlockSpec((LANES, 2 * GLA_W), lambda b, j: (0, 0)),
            pl.BlockSpec((1, 2 * GLA_W), lambda b, j: (0, 0)),
            tab_spec, tab_spec, tab_spec, tab_spec,
        ],
        out_specs=[
            pl.BlockSpec((sps, TOK, 4 * GLA_W), lambda b, j: (b, j, 0)),
            pl.BlockSpec((sps, TOK, 2 * GLA_W), lambda b, j: (b, j, 0)),
            pl.BlockSpec((sps, TOK, 2 * DIFF_W), lambda b, j: (b, j, 0)),
            pl.BlockSpec((sps, DIFF_HEADS, DIFF_VROWS, TOK), lambda b, j: (b, 0, 0, j)),
            pl.BlockSpec((sps, TOK, SWA_Q_W), lambda b, j: (b, j, 0)),
            pl.BlockSpec((sps, TOK, 2 * LANES), lambda b, j: (b, j, 0)),
            pl.BlockSpec((sps, SWA_KV_HEADS, TOK // SWA_TQ, SWA_VROWS, SWA_TQ), lambda b, j: (b, 0, j, 0, 0)),
        ] + extra_specs,
        out_shape=[
            jax.ShapeDtypeStruct((bsz, s, 4 * GLA_W), F32),
            jax.ShapeDtypeStruct((bsz, s, 2 * GLA_W), F32),
            jax.ShapeDtypeStruct((bsz, s, 2 * DIFF_W), act),
            jax.ShapeDtypeStruct((bsz, DIFF_HEADS, DIFF_VROWS, s), act),
            jax.ShapeDtypeStruct((bsz, s, SWA_Q_W), act),
            jax.ShapeDtypeStruct((bsz, s, 2 * LANES), act),
            jax.ShapeDtypeStruct((bsz, SWA_KV_HEADS, s // SWA_TQ, SWA_VROWS, SWA_TQ), act),
        ] + extra_shapes,
        compiler_params=pltpu.CompilerParams(
            dimension_semantics=("arbitrary", "arbitrary"), vmem_limit_bytes=VMEM_LIMIT),
        name="in_projection",
    )(*stream_in, mods, w_in_p, wg, bg, *tabs)


def _gla_direction(gla_ref, gate_ref, gate_col, o_ref, st_ref, reverse, consts):
    tri, head_lane, blockdiag, sub_masks = consts
    c_, sub = GLA_CHUNK, GLA_SUB
    nsub = c_ // sub
    nchunk = TOK // c_
    g_all = gate_ref[:, gate_col:gate_col + GLA_W]
    b_all = _split_mm_t(tri, g_all)
    order = range(nchunk - 1, -1, -1) if reverse else range(nchunk)
    zero = jnp.zeros((), F32)
    chunks = []
    for c in order:
        r0 = c * c_
        q = gla_ref[r0:r0 + c_, 0:GLA_W]
        k = gla_ref[r0:r0 + c_, GLA_W:2 * GLA_W]
        v = gla_ref[r0:r0 + c_, 2 * GLA_W:3 * GLA_W]
        b = b_all[r0:r0 + c_, :]
        b_end = b[0:1, :] if reverse else b[c_ - 1:c_, :]
        atts = []
        for i in range(nsub):
            t0, t1 = i * sub, (i + 1) * sub
            if reverse:
                ref_b = b[t1:t1 + 1, :] if i < nsub - 1 else jnp.zeros((1, GLA_W), F32)
                k0, k1 = t0, c_
            else:
                ref_b = b[t0 - 1:t0, :] if i > 0 else jnp.zeros((1, GLA_W), F32)
                k0, k1 = 0, t1
            qd = q[t0:t1, :] * jnp.exp(b[t0:t1, :] - ref_b)
            lhs = jnp.concatenate([jnp.where(head_lane[hh], qd, zero) for hh in range(GLA_HEADS)], axis=0)
            kk = k[k0:k1, :] * jnp.exp(ref_b - b[k0:k1, :])
            atts.append((_mm_nt(lhs, kk), k0, k1))
        upd = _mm_tn(v, k * jnp.exp(b_end - b))
        chunks.append(dict(r0=r0, v=v, atts=atts, upd=upd, qs=q * jnp.exp(b), decay=jnp.exp(b_end)))
    for ch in chunks:
        o_parts = []
        for i, (att, k0, k1) in enumerate(ch["atts"]):
            res = _mm(jnp.where(sub_masks[(reverse, i)], att, zero), ch["v"][k0:k1, :])
            o_i = res[(GLA_HEADS - 1) * sub:GLA_HEADS * sub, :]
            for hh in range(GLA_HEADS - 2, -1, -1):
                o_i = jnp.where(head_lane[hh], res[hh * sub:(hh + 1) * sub, :], o_i)
            o_parts.append(o_i)
        ch["o_intra"] = jnp.concatenate(o_parts, axis=0)
    st = st_ref[...]
    for ch in chunks:
        o_ref[ch["r0"]:ch["r0"] + c_, :] = ch["o_intra"] + _mm_nt(ch["qs"], st)
        st = st * ch["decay"] + jnp.where(blockdiag, ch["upd"], zero)
    st_ref[...] = st


def _gla_consts():
    c_, sub = GLA_CHUNK, GLA_SUB
    nsub = c_ // sub
    r = lax.broadcasted_iota(jnp.int32, (TOK, TOK), 0)
    cc = lax.broadcasted_iota(jnp.int32, (TOK, TOK), 1)
    same_chunk = (r // c_) == (cc // c_)
    tri_f = jnp.where(same_chunk & (cc <= r), 1.0, 0.0).astype(F32)
    tri_b = jnp.where(same_chunk & (cc >= r), 1.0, 0.0).astype(F32)
    lane = lax.broadcasted_iota(jnp.int32, (1, GLA_W), 1)
    head_lane = [(lane // GLA_DK) == hh for hh in range(GLA_HEADS)]
    blockdiag = (r // GLA_DV) == (cc // GLA_DK)
    sub_masks = {}
    for reverse in (False, True):
        for i in range(nsub):
            nk = (nsub - i) * sub if reverse else (i + 1) * sub
            rr = lax.broadcasted_iota(jnp.int32, (GLA_HEADS * sub, nk), 0) % sub
            kc = lax.broadcasted_iota(jnp.int32, (GLA_HEADS * sub, nk), 1)
            if reverse:
                sub_masks[(reverse, i)] = kc >= rr
            else:
                sub_masks[(reverse, i)] = (kc - (nk - sub)) <= rr
    return tri_f, tri_b, head_lane, blockdiag, sub_masks


def _gla_kernel(gla_f_ref, gate_f_ref, gla_b_ref, gate_b_ref, of_ref, ob_ref, stf_ref, stb_ref):
    @pl.when(pl.program_id(1) == 0)
    def _():
        stf_ref[...] = jnp.zeros_like(stf_ref)
        stb_ref[...] = jnp.zeros_like(stb_ref)

    tri_f, tri_b, head_lane, blockdiag, sub_masks = _gla_consts()
    for smp in range(gla_f_ref.shape[0]):
        _gla_direction(gla_f_ref.at[smp], gate_f_ref.at[smp], 0, of_ref.at[smp], stf_ref.at[smp], False,
                       (tri_f, head_lane, blockdiag, sub_masks))
        _gla_direction(gla_b_ref.at[smp], gate_b_ref.at[smp], GLA_W, ob_ref.at[smp], stb_ref.at[smp], True,
                       (tri_b, head_lane, blockdiag, sub_masks))


def _gla_call(gla_in, gates):
    bsz, s, _ = gla_in.shape
    nt = s // TOK
    sps = 2 if bsz % 2 == 0 else 1

    def fwd(b, j):
        return (b, j, 0)

    def bwd(b, j):
        return (b, jnp.where(j == 0, 0, nt - j), 0)

    return pl.pallas_call(
        _gla_kernel,
        grid=(bsz // sps, nt),
        in_specs=[
            pl.BlockSpec((sps, TOK, 4 * GLA_W), fwd),
            pl.BlockSpec((sps, TOK, 2 * GLA_W), fwd),
            pl.BlockSpec((sps, TOK, 4 * GLA_W), bwd),
            pl.BlockSpec((sps, TOK, 2 * GLA_W), bwd),
        ],
        out_specs=[
            pl.BlockSpec((sps, TOK, GLA_W), fwd),
            pl.BlockSpec((sps, TOK, GLA_W), bwd),
        ],
        out_shape=[jax.ShapeDtypeStruct((bsz, s, GLA_W), F32)] * 2,
        scratch_shapes=[pltpu.VMEM((sps, GLA_W, GLA_W), F32), pltpu.VMEM((sps, GLA_W, GLA_W), F32)],
        compiler_params=pltpu.CompilerParams(
            dimension_semantics=("arbitrary", "arbitrary"), vmem_limit_bytes=VMEM_LIMIT),
        name="gla_scan",
    )(gla_in, gates, gla_in, gates)


def _diff_kernel(q_ref, k_ref, vt_ref, lam_ref, lam0_ref, g_ref, o_ref, *, n_ctx):
    lam_p = lam_ref[...]
    lam = (jnp.exp(jnp.sum(lam_p[0:1, :] * lam_p[1:2, :], axis=-1, keepdims=True))
           - jnp.exp(jnp.sum(lam_p[2:3, :] * lam_p[3:4, :], axis=-1, keepdims=True))
           + lam0_ref[...])
    lane = lax.broadcasted_iota(jnp.int32, (1, DIFF_W), 1)

    def attend(nk, parts):
        q = q_ref[...]
        step = nk // parts

        def scores(idx):
            lo = idx * DIFF_DH
            qm = jnp.where((lane >= lo) & (lane < lo + DIFF_DH), q, jnp.zeros_like(q))
            return [_mm_nt(k_ref[i * step:(i + 1) * step, :], qm).astype(MXU_DTYPE)
                    for i in range(parts)]

        def probs(sts):
            m = _col_max(sts[0])
            for st in sts[1:]:
                m = jnp.maximum(m, _col_max(st))
            return [jnp.exp2(st - m) for st in sts]

        def values(idx, ps):
            acc = None
            for i, p in enumerate(ps):
                pv = _mm(vt_ref[idx // 2, :, i * step:(i + 1) * step], p)
                acc = pv if acc is None else acc + pv
            return acc[0:DIFF_DV, :] / acc[DIFF_DV:DIFF_DV + 1, :]

        n_str = 2 * DIFF_HEADS
        sts = {0: scores(0), 1: scores(1)}
        ps = {0: probs(sts.pop(0))}
        comps = []
        for idx in range(n_str):
            if idx + 2 < n_str:
                sts[idx + 2] = scores(idx + 2)
            if idx + 1 < n_str:
                ps[idx + 1] = probs(sts.pop(idx + 1))
            comps.append(values(idx, ps.pop(idx)))
        heads = []
        for hh in range(DIFF_HEADS):
            o_h = comps[2 * hh] - lam * comps[2 * hh + 1]
            ms = jnp.mean(o_h * o_h, axis=0, keepdims=True)
            heads.append(o_h * lax.rsqrt(ms + LN_EPS))
        out = jnp.concatenate(heads, axis=0).T
        o_ref[...] = (out * g_ref[...]).astype(o_ref.dtype)

    j = pl.program_id(1)

    @pl.when(j == 0)
    def _():
        attend(n_ctx, 1)

    @pl.when(j > 0)
    def _():
        attend(k_ref.shape[0], 2)


def _col_max(x):
    r = x.shape[0]
    slab = 16
    while r % (2 * slab) == 0 and r // slab > 32:
        slab *= 2
    acc = x[0:slab, :]
    for i in range(1, r // slab):
        acc = jnp.maximum(acc, x[i * slab:(i + 1) * slab, :])
    return jnp.max(acc, axis=0, keepdims=True)


def _diff_call(dqk, dvt, lam_p, lam0, g_eff):
    bsz, s, _ = dqk.shape
    nt = s // TOK
    return pl.pallas_call(
        functools.partial(_diff_kernel, n_ctx=TOK),
        grid=(bsz, nt),
        in_specs=[
            pl.BlockSpec((None, TOK, DIFF_W), lambda b, j: (b, j, 0)),
            pl.BlockSpec((None, s, DIFF_W), lambda b, j: (b, 0, 1)),
            pl.BlockSpec((None, DIFF_HEADS, DIFF_VROWS, s), lambda b, j: (b, 0, 0, 0)),
            pl.BlockSpec((4, DIFF_DH), lambda b, j: (0, 0)),
            pl.BlockSpec((1, 1), lambda b, j: (0, 0)),
            pl.BlockSpec((1, DIFF_W), lambda b, j: (0, 0)),
        ],
        out_specs=pl.BlockSpec((None, TOK, DIFF_W), lambda b, j: (b, j, 0)),
        out_shape=jax.ShapeDtypeStruct((bsz, s, DIFF_W), MXU_DTYPE),
        compiler_params=pltpu.CompilerParams(
            dimension_semantics=("arbitrary", "arbitrary"), vmem_limit_bytes=VMEM_LIMIT),
        name="diff_attention",
    )(dqk, dqk, dvt, lam_p, lam0, g_eff)


def _swa_kernel(q_ref, k_ref, vt_ref, sink_ref, o_ref, *, n_ctx, n_lat):
    j = pl.program_id(1)
    tq = SWA_TQ
    n_win = 3
    n_ctx_t = n_ctx // tq
    s_rows = n_ctx + n_lat
    lane = lax.broadcasted_iota(jnp.int32, (1, LANES), 1)
    low = lane < SWA_DH
    q_per_kv = SWA_HEADS // SWA_KV_HEADS
    chains = [(sub, grp) for sub in range(TOK // tq) for grp in range(SWA_KV_HEADS)]

    def window(sub):
        p0 = j * TOK + sub * tq - n_ctx
        start = pl.multiple_of(jnp.clip(n_ctx + p0 - tq, 0, s_rows - n_win * tq), tq)
        return p0, start

    def scores(idx):
        sub, grp = chains[idx]
        _, start = window(sub)
        blocks = []
        for r in range(q_per_kv):
            hh = grp * q_per_kv + r
            qp = q_ref[sub * tq:(sub + 1) * tq, (hh // 2) * LANES:(hh // 2 + 1) * LANES]
            blocks.append(jnp.where(low if hh % 2 == 0 else ~low, qp, jnp.zeros_like(qp)))
        lhs = jnp.concatenate(blocks, axis=0)
        st_c = _mm_nt(k_ref[0:n_ctx, grp * LANES:(grp + 1) * LANES], lhs)
        st_l = _mm_nt(k_ref[pl.ds(start, n_win * tq), grp * LANES:(grp + 1) * LANES], lhs)
        return st_c, st_l

    def probs(idx, st):
        sub, grp = chains[idx]
        st_c, st_l = st
        p0, start = window(sub)
        pos_q = p0 + lax.broadcasted_iota(jnp.int32, (1, tq), 1)
        pos_k = start - n_ctx + lax.broadcasted_iota(jnp.int32, (n_win * tq, 1), 0)
        valid = (jnp.abs(pos_k - pos_q) <= WINDOW) & (pos_k >= 0) & (p0 >= 0)
        bias = jnp.where(valid, 0.0, -jnp.inf).astype(F32)
        st_l = st_l + jnp.concatenate([bias] * q_per_kv, axis=1)
        st_c = st_c.astype(MXU_DTYPE)
        st_l = st_l.astype(MXU_DTYPE)
        snk = jnp.concatenate([sink_ref[grp * q_per_kv + r:grp * q_per_kv + r + 1, :]
                               for r in range(q_per_kv)], axis=1)
        m = jnp.maximum(jnp.maximum(_col_max(st_c), _col_max(st_l)).astype(F32), snk)
        mb = m.astype(MXU_DTYPE)
        return jnp.exp2(st_c - mb), jnp.exp2(st_l - mb), jnp.exp2(snk - mb.astype(F32))

    def values(idx, pr):
        sub, grp = chains[idx]
        p_c, p_l, p_snk = pr
        _, start = window(sub)
        t0 = start // tq
        acc = None
        for i in range(n_ctx_t):
            pv = _mm(vt_ref[grp, i], p_c[i * tq:(i + 1) * tq, :])
            acc = pv if acc is None else acc + pv
        for i in range(n_win):
            acc = acc + _mm(vt_ref[grp, t0 + i], p_l[i * tq:(i + 1) * tq, :])
        o = acc[0:SWA_DH, :] / (acc[SWA_DH:SWA_DH + 1, :] + p_snk)
        for hp in range(q_per_kv // 2):
            pair = grp * (q_per_kv // 2) + hp
            both = jnp.concatenate([o[:, (2 * hp) * tq:(2 * hp + 1) * tq],
                                    o[:, (2 * hp + 1) * tq:(2 * hp + 2) * tq]], axis=0)
            o_ref[sub * tq:(sub + 1) * tq, pair * LANES:(pair + 1) * LANES] = both.T.astype(o_ref.dtype)

    n_ch = len(chains)
    sts = {0: scores(0), 1: scores(1)}
    prs = {0: probs(0, sts.pop(0))}
    for idx in range(n_ch):
        if idx + 2 < n_ch:
            sts[idx + 2] = scores(idx + 2)
        if idx + 1 < n_ch:
            prs[idx + 1] = probs(idx + 1, sts.pop(idx + 1))
        values(idx, prs.pop(idx))


def _swa_call(sq, sk, svt, sink_tile, n_ctx):
    bsz, s, _ = sq.shape
    return pl.pallas_call(
        functools.partial(_swa_kernel, n_ctx=n_ctx, n_lat=s - n_ctx),
        grid=(bsz, s // TOK),
        in_specs=[
            pl.BlockSpec((None, TOK, SWA_Q_W), lambda b, j: (b, j, 0)),
            pl.BlockSpec((None, s, 2 * LANES), lambda b, j: (b, 0, 0)),
            pl.BlockSpec((None, SWA_KV_HEADS, s // SWA_TQ, SWA_VROWS, SWA_TQ), lambda b, j: (b, 0, 0, 0, 0)),
            pl.BlockSpec((SWA_HEADS, LANES), lambda b, j: (0, 0)),
        ],
        out_specs=pl.BlockSpec((None, TOK, SWA_Q_W), lambda b, j: (b, j, 0)),
        out_shape=jax.ShapeDtypeStruct((bsz, s, SWA_Q_W), MXU_DTYPE),
        compiler_params=pltpu.CompilerParams(
            dimension_semantics=("arbitrary", "arbitrary"), vmem_limit_bytes=VMEM_LIMIT),
        name="windowed_gqa",
    )(sq, sk, svt, sink_tile)


POST_TOK = 2 * TOK
MXU_TILE = 256


def _ffn_chunk_bounds(hid):
    cut = -(-(hid // MXU_TILE) // 2) * MXU_TILE
    return ((0, cut), (cut, hid))


def _post_kernel(*refs, alpha, tiles_per_sample, ctx_row, paired):
    halves = POST_TOK // TOK
    n_act = 6
    if paired:
        acts = [refs[halves * i:halves * (i + 1)] for i in range(n_act)]
        load = lambda i, x: acts[i][x][...]
    else:
        acts = refs[:n_act]
        load = lambda i, x: acts[i][x * TOK:(x + 1) * TOK, :]
    (mod_ref, wo_ref, wg_ref, wu_ref, wd_ref, gng_ref, lng_ref, lnb_ref, o_ref) = refs[len(refs) - 9:]
    hid = wg_ref.shape[1]
    bounds = _ffn_chunk_bounds(hid)
    gm = _group_mean_matrix(GLA_W, GLA_DV)
    st = [dict() for _ in range(halves)]

    def out_proj(x):
        if paired:
            row = pl.program_id(0)
        else:
            g = pl.program_id(0) * halves + x
            row = jnp.where(g % tiles_per_sample == 0, ctx_row, g // tiles_per_sample)
        st[x]["mod"] = mod_ref[row]
        o = load(0, x) + load(1, x)
        ms = _split_mm(o * o, gm)
        gla = o * lax.rsqrt(ms + LN_EPS) * gng_ref[...] * _silu(load(2, x))
        st[x]["y"] = (_mm(gla, wo_ref[0:GLA_W, :])
                      + _mm(load(3, x), wo_ref[GLA_W:GLA_W + DIFF_W, :])
                      + _mm(load(4, x), wo_ref[GLA_W + DIFF_W:, :]))

    def mid_norm(x):
        mod = st[x]["mod"]
        z = alpha * load(5, x) + mod[2:3, :] * st[x].pop("y")
        hm = _ln_plain(z) * lng_ref[0:1, :] + lnb_ref[0:1, :]
        st[x]["hm"] = hm
        st[x]["u"] = (_ln_plain(hm) * (1.0 + mod[4:5, :]) + mod[3:4, :]).astype(MXU_DTYPE)

    def gate_up(x, c):
        u = st[x]["u"]
        c0, c1 = bounds[c]
        st[x]["g", c] = jnp.dot(u, wg_ref[:, c0:c1], preferred_element_type=F32)
        st[x]["p", c] = jnp.dot(u, wu_ref[:, c0:c1], preferred_element_type=F32)

    def act(x, c):
        st[x]["a", c] = (_silu(st[x].pop(("g", c))) * st[x].pop(("p", c))).astype(MXU_DTYPE)

    def down(x, c):
        c0, c1 = bounds[c]
        f = jnp.dot(st[x].pop(("a", c)), wd_ref[c0:c1, :], preferred_element_type=F32)
        st[x]["f"] = f if c == 0 else st[x]["f"] + f

    def final(x):
        rows = slice(x * TOK, (x + 1) * TOK)
        z = alpha * st[x]["hm"] + st[x]["mod"][5:6, :] * st[x]["f"]
        o_ref[rows, :] = _ln_plain(z) * lng_ref[1:2, :] + lnb_ref[1:2, :]

    a_, b_ = 0, 1
    out_proj(a_)
    out_proj(b_)
    mid_norm(a_)
    gate_up(a_, 0)
    mid_norm(b_)
    gate_up(a_, 1)
    act(a_, 0)
    gate_up(b_, 0)
    act(a_, 1)
    down(a_, 0)
    gate_up(b_, 1)
    act(b_, 0)
    down(a_, 1)
    act(b_, 1)
    down(b_, 0)
    final(a_)
    down(b_, 1)
    final(b_)


def _post_call(o_f, o_b, gla_in, diff_o, swa_o, h, mods, layer, ctx_row, w_out, w_g, w_u, w_d, gng, lng, lnb, alpha):
    bsz, s, d = h.shape
    rows = bsz * s
    assert rows % POST_TOK == 0
    hid = w_g.shape[1]
    flat = lambda a: a.reshape(rows, a.shape[-1])
    tile = lambda w: pl.BlockSpec((POST_TOK, w), lambda t: (t, 0))
    single = pl.Buffered(1)
    const = lambda shape: pl.BlockSpec(shape, lambda t: (0,) * len(shape), pipeline_mode=single)
    out = pl.pallas_call(
        functools.partial(_post_kernel, alpha=alpha, tiles_per_sample=s // TOK, ctx_row=ctx_row, paired=False),
        grid=(rows // POST_TOK,),
        in_specs=[
            tile(GLA_W), tile(GLA_W),
            pl.BlockSpec((POST_TOK, GLA_W), lambda t: (t, 3)),
            tile(DIFF_W), tile(SWA_Q_W), tile(d),
            pl.BlockSpec((None,) + mods.shape[1:], lambda t: (layer, 0, 0, 0)),
            const((d, d)), const((d, hid)), const((d, hid)), const((hid, d)),
            pl.BlockSpec((1, GLA_W), lambda t: (0, 0)),
            pl.BlockSpec((2, d), lambda t: (0, 0)), pl.BlockSpec((2, d), lambda t: (0, 0)),
        ],
        out_specs=tile(d),
        out_shape=jax.ShapeDtypeStruct((rows, d), F32),
        compiler_params=pltpu.CompilerParams(
            dimension_semantics=("arbitrary",), vmem_limit_bytes=POST_VMEM_LIMIT),
        name="post_attention",
    )(flat(o_f), flat(o_b), flat(gla_in), flat(diff_o), flat(swa_o), flat(h), mods,
      w_out, w_g, w_u, w_d, gng, lng, lnb)
    return out.reshape(bsz, s, d)


def _post_latent_call(o_f, o_b, gla_in, diff_o, swa_o, h, mods, layer, w_out, w_g, w_u, w_d, gng, lng, lnb,
                      alpha, n_ctx):
    bsz, s, d = h.shape
    n_lat = s - n_ctx
    assert n_ctx == TOK and n_lat % POST_TOK == 0
    hid = w_g.shape[1]
    halves = POST_TOK // TOK
    first = n_ctx // TOK

    def pair(w, col=0):
        return [pl.BlockSpec((None, TOK, w), functools.partial(lambda b, i, x: (b, first + halves * i + x, col), x=x))
                for x in range(halves)]

    single = pl.Buffered(1)
    const = lambda shape: pl.BlockSpec(shape, lambda b, i: (0,) * len(shape), pipeline_mode=single)
    acts = [o_f, o_b, gla_in, diff_o, swa_o, h]
    specs = pair(GLA_W) + pair(GLA_W) + pair(GLA_W, 3) + pair(DIFF_W) + pair(SWA_Q_W) + pair(d)
    return pl.pallas_call(
        functools.partial(_post_kernel, alpha=alpha, tiles_per_sample=s // TOK, ctx_row=0, paired=True),
        grid=(bsz, n_lat // POST_TOK),
        in_specs=specs + [
            pl.BlockSpec((None,) + mods.shape[1:], lambda b, i: (layer, 0, 0, 0)),
            const((d, d)), const((d, hid)), const((d, hid)), const((hid, d)),
            pl.BlockSpec((1, GLA_W), lambda b, i: (0, 0)),
            pl.BlockSpec((2, d), lambda b, i: (0, 0)), pl.BlockSpec((2, d), lambda b, i: (0, 0)),
        ],
        out_specs=pl.BlockSpec((None, POST_TOK, d), lambda b, i: (b, i, 0)),
        out_shape=jax.ShapeDtypeStruct((bsz, n_lat, d), F32),
        compiler_params=pltpu.CompilerParams(
            dimension_semantics=("arbitrary", "arbitrary"), vmem_limit_bytes=POST_VMEM_LIMIT),
        name="post_attention_latent",
    )(*[a for a in acts for _ in range(halves)], mods, w_out, w_g, w_u, w_d, gng, lng, lnb)


def _rope_tables(rows, n_ctx, dim):
    row = jnp.repeat(jnp.arange(rows, dtype=F32), GRID_W)
    col = jnp.tile(jnp.arange(GRID_W, dtype=F32), rows)
    n_freq = dim // 4
    inv = jnp.power(ROPE_BASE, -jnp.arange(n_freq, dtype=F32) / n_freq)
    ang = jnp.concatenate([row[:, None] * inv, col[:, None] * inv], axis=-1)
    cos, sin = jnp.cos(ang), jnp.sin(ang)
    reps = LANES // dim
    cos_t = jnp.tile(jnp.concatenate([cos, cos], axis=-1), (1, reps))
    sin_t = jnp.tile(jnp.concatenate([-sin, sin], axis=-1), (1, reps))
    cos_t = jnp.concatenate([jnp.ones((n_ctx, LANES), F32), cos_t], axis=0)
    sin_t = jnp.concatenate([jnp.zeros((n_ctx, LANES), F32), sin_t], axis=0)
    return cos_t, sin_t


def _reorder_w_in(w_in):
    o = 0
    parts = {}
    for name, width in (("gq", GLA_W), ("gk", GLA_W), ("gv", GLA_W), ("go", GLA_W),
                        ("zf", GLA_GATE_RANK), ("zb", GLA_GATE_RANK),
                        ("dq", DIFF_W), ("dk", DIFF_W), ("dv", DIFF_W),
                        ("sq", SWA_Q_W), ("sk", SWA_KV_W), ("sv", SWA_KV_W)):
        parts[name] = w_in[:, o:o + width]
        o += width
    cols = [parts[n] for n in ("gq", "gk", "gv", "go", "dq", "dk", "dv", "sq", "sk", "sv", "zf", "zb")]
    pad = IN_PAD_W - sum(c.shape[1] for c in cols)
    cols.append(jnp.zeros((w_in.shape[0], pad), w_in.dtype))
    return jnp.concatenate(cols, axis=1)


def kernel(x, c, ctx, c_ctx, w_ada, b_ada, w_in, w_gla_gate, b_gla_gate, gla_norm_g, diff_lambda,
           diff_norm_g, swa_sink, w_out, ln_g, ln_b, w_ffn_gate, w_ffn_up, w_ffn_down):
    bsz, n_lat, d = x.shape
    n_ctx = ctx.shape[1]
    depth = w_ada.shape[0]
    assert n_ctx == TOK and n_lat % TOK == 0 and n_lat % GRID_W == 0
    alpha = (2.0 * depth) ** 0.25
    wdt = MXU_DTYPE

    cond_rows = -(-(bsz + 1) // 8) * 8
    cond =jnp.concatenate([c, c_ctx[None, :], jnp.zeros((cond_rows - bsz - 1, d), F32)], axis=0)
    mods = _mods_call(cond, w_ada, b_ada).reshape(depth, cond_rows, 6, d)
    ctx_row = bsz

    rows = n_lat // GRID_W
    tabs = _rope_tables(rows, n_ctx, DIFF_DH) + _rope_tables(rows, n_ctx, SWA_DH)

    h = (x, ctx)
    for layer in range(depth):
        last = layer == depth - 1
        lam_init = 0.8 - 0.6 * math.exp(-0.3 * layer)
        w_in_p = _reorder_w_in(w_in[layer]).astype(wdt)
        wg = jnp.zeros((LANES, 2 * GLA_W), F32)
        wg = wg.at[0:GLA_GATE_RANK, 0:GLA_W].set(w_gla_gate[layer, 0])
        wg = wg.at[GLA_GATE_RANK:2 * GLA_GATE_RANK, GLA_W:].set(w_gla_gate[layer, 1])
        bg = b_gla_gate[layer].reshape(1, 2 * GLA_W)

        proj = _inproj_call(h, mods, layer, ctx_row, w_in_p, wg.astype(wdt), bg, tabs)
        if layer == 0:
            h = proj[-1]
        gla_in, gates, dqk, dvt, sq, sk, svt = proj[:7]
        o_f, o_b = _gla_call(gla_in, gates)
        diff_o = _diff_call(dqk, dvt, diff_lambda[layer], jnp.full((1, 1), lam_init, F32),
                            (jnp.tile(diff_norm_g[layer], DIFF_HEADS) * (1.0 - lam_init)).reshape(1, DIFF_W))
        swa_o = _swa_call(sq, sk, svt,
                          jnp.broadcast_to(swa_sink[layer][:, None] * LOG2_E, (SWA_HEADS, LANES)), n_ctx)
        gng = jnp.tile(gla_norm_g[layer], GLA_HEADS).reshape(1, GLA_W)
        w_o, w_g, w_u, w_d = (w_out[layer].astype(wdt), w_ffn_gate[layer].astype(wdt),
                              w_ffn_up[layer].astype(wdt), w_ffn_down[layer].astype(wdt))
        if last:
            h = _post_latent_call(o_f, o_b, gla_in, diff_o, swa_o, h, mods, layer, w_o, w_g, w_u, w_d,
                                  gng, ln_g[layer], ln_b[layer], alpha, n_ctx)
        else:
            h = _post_call(o_f, o_b, gla_in, diff_o, swa_o, h, mods, layer, ctx_row, w_o, w_g, w_u, w_d,
                           gng, ln_g[layer], ln_b[layer], alpha)
    return h
```

```python
import functools
import math

import jax
import jax.numpy as jnp
from jax import lax
from jax.experimental import pallas as pl
from jax.experimental.pallas import tpu as pltpu

F32 = jnp.float32
MXU_DTYPE = jnp.bfloat16

GRID_W = 64
GLA_HEADS, GLA_DK, GLA_DV = 4, 64, 64
GLA_GATE_RANK = 16
GLA_GATE_NORM = 16.0
GLA_CHUNK = 64
GLA_SUB = 16
DIFF_HEADS, DIFF_DH = 4, 32
DIFF_DV = 2 * DIFF_DH
SWA_HEADS, SWA_KV_HEADS, SWA_DH = 8, 2, 64
WINDOW = 128
ROPE_BASE = 10000.0
LN_EPS = 1e-6

GLA_W = GLA_HEADS * GLA_DK
DIFF_W = DIFF_HEADS * 2 * DIFF_DH
SWA_Q_W = SWA_HEADS * SWA_DH
SWA_KV_W = SWA_KV_HEADS * SWA_DH
LANES = 128
TOK = 256
SWA_TQ = 128
DIFF_VROWS = DIFF_DV + 16
SWA_VROWS = SWA_DH + 16
LOG2_E = math.log2(math.e)
VMEM_LIMIT = 52 * 1024 * 1024
POST_VMEM_LIMIT = 58 * 1024 * 1024


def _mm(a, b):
    return jnp.dot(a.astype(MXU_DTYPE), b.astype(MXU_DTYPE), preferred_element_type=F32)


def _mm_nt(a, b):
    return lax.dot_general(a.astype(MXU_DTYPE), b.astype(MXU_DTYPE),
                           (((1,), (1,)), ((), ())), preferred_element_type=F32)


def _mm_tn(a, b):
    return lax.dot_general(a.astype(MXU_DTYPE), b.astype(MXU_DTYPE),
                           (((0,), (0,)), ((), ())), preferred_element_type=F32)


def _split_mm(a, b_exact):
    if MXU_DTYPE == F32:
        return jnp.dot(a, b_exact.astype(F32), preferred_element_type=F32)
    hi = a.astype(jnp.bfloat16)
    r1 = a - hi.astype(F32)
    mid = r1.astype(jnp.bfloat16)
    lo = (r1 - mid.astype(F32)).astype(jnp.bfloat16)
    b = b_exact.astype(jnp.bfloat16)
    return (jnp.dot(hi, b, preferred_element_type=F32)
            + jnp.dot(mid, b, preferred_element_type=F32)
            + jnp.dot(lo, b, preferred_element_type=F32))


def _split_mm_t(b_exact, a):
    if MXU_DTYPE == F32:
        return jnp.dot(b_exact.astype(F32), a, preferred_element_type=F32)
    hi = a.astype(jnp.bfloat16)
    r1 = a - hi.astype(F32)
    mid = r1.astype(jnp.bfloat16)
    lo = (r1 - mid.astype(F32)).astype(jnp.bfloat16)
    b = b_exact.astype(jnp.bfloat16)
    return (jnp.dot(b, hi, preferred_element_type=F32)
            + jnp.dot(b, mid, preferred_element_type=F32)
            + jnp.dot(b, lo, preferred_element_type=F32))


def _ln_plain(x):
    mu = jnp.mean(x, axis=-1, keepdims=True)
    xc = x - mu
    var = jnp.mean(xc * xc, axis=-1, keepdims=True)
    return xc * lax.rsqrt(var + LN_EPS)


def _silu(x):
    return x / (1.0 + jnp.exp(-x))


def _group_mean_matrix(width, group):
    r = lax.broadcasted_iota(jnp.int32, (width, width), 0) // group
    c = lax.broadcasted_iota(jnp.int32, (width, width), 1) // group
    return jnp.where(r == c, 1.0 / group, 0.0).astype(F32)


def _mods_kernel(a_ref, w_ref, b_ref, o_ref):
    a = a_ref[...]
    o_ref[...] = jnp.dot(_silu(a), w_ref[...], preferred_element_type=F32,
                         precision=lax.Precision.HIGHEST) + b_ref[...]


def _mods_call(cond, w_ada, b_ada):
    depth, d, width = w_ada.shape
    rows = cond.shape[0]
    bn = 1536
    return pl.pallas_call(
        _mods_kernel,
        grid=(depth, width // bn),
        in_specs=[
            pl.BlockSpec((rows, d), lambda l, n: (0, 0)),
            pl.BlockSpec((None, d, bn), lambda l, n: (l, 0, n)),
            pl.BlockSpec((None, 1, bn), lambda l, n: (l, 0, n)),
        ],
        out_specs=pl.BlockSpec((None, rows, bn), lambda l, n: (l, 0, n)),
        out_shape=jax.ShapeDtypeStruct((depth, rows, width), F32),
        compiler_params=pltpu.CompilerParams(vmem_limit_bytes=VMEM_LIMIT),
        name="adaln_mods",
    )(cond, w_ada, b_ada.reshape(depth, 1, width))


C_GQ, C_GK, C_GV, C_GO = 0, 256, 512, 768
C_DQ, C_DK, C_DV = 1024, 1280, 1536
C_SQ, C_SK, C_SV = 1792, 2304, 2432
C_Z = 2560
IN_PAD_W = 2688


def _rope(x, cos, sin_signed, half):
    outs = []
    lane = lax.broadcasted_iota(jnp.int32, (1, LANES), 1)
    first = (lane % (2 * half)) < half
    for s in range(x.shape[1] // LANES):
        xs = x[:, s * LANES:(s + 1) * LANES]
        up = pltpu.roll(xs, LANES - half, axis=1)
        dn = pltpu.roll(xs, half, axis=1)
        outs.append(xs * cos + jnp.where(first, up, dn) * sin_signed)
    return outs[0] if len(outs) == 1 else jnp.concatenate(outs, axis=1)


def _inproj_kernel(*refs, split_input):
    j = pl.program_id(1)
    if split_input:
        x_ref, ctx_ref = refs[0:2]
        refs = refs[2:]
        hout_ref = refs[-1]
        refs = refs[:-1]
        sps = x_ref.shape[0]
        for n in range(sps):
            hout_ref[n] = jnp.where(j == 0, ctx_ref[n], x_ref[n])
        h_ref = hout_ref
    else:
        h_ref = refs[0]
        refs = refs[1:]
        sps = h_ref.shape[0]
    (mod_ref, w_ref, wg_ref, bg_ref, cd_ref, sd_ref, cs_ref, ss_ref,
     gla_ref, gate_ref, dqk_ref, dvt_ref, sq_ref, sk_ref, svt_ref) = refs
    cd, sd, cs, ss = cd_ref[...], sd_ref[...], cs_ref[...], ss_ref[...]
    lane = lax.broadcasted_iota(jnp.int32, (1, LANES), 1)
    low = lane < SWA_DH
    ones_d = jnp.where(lax.broadcasted_iota(jnp.int32, (DIFF_VROWS - DIFF_DV, TOK), 0) == 0, 1.0, 0.0)
    ones_s = jnp.where(lax.broadcasted_iota(jnp.int32, (SWA_VROWS - SWA_DH, SWA_TQ), 0) == 0, 1.0, 0.0)
    st = [dict() for _ in range(sps)]

    def norm(n):
        mod = mod_ref[jnp.where(j == 0, 0, n)]
        st[n]["u"] = (_ln_plain(h_ref[n]) * (1.0 + mod[1:2, :]) + mod[0:1, :]).astype(MXU_DTYPE)

    def project(n, name, c0, c1):
        st[n][name] = jnp.dot(st[n]["u"], w_ref[:, c0:c1], preferred_element_type=F32)

    def gate_out(n):
        gpre = _mm(st[n].pop("z"), wg_ref[...]) + bg_ref[...]
        logsig = jnp.minimum(gpre, 0.0) - jnp.log(1.0 + jnp.exp(-jnp.abs(gpre)))
        gate_ref[n] = logsig * (1.0 / GLA_GATE_NORM)

    def diff_out(n):
        pd = st[n].pop("d")
        dq = _rope(pd[:, 0:DIFF_W], cd, sd, DIFF_DH // 2) * (DIFF_DH ** -0.5 * LOG2_E)
        dk = _rope(pd[:, DIFF_W:2 * DIFF_W], cd, sd, DIFF_DH // 2)
        dqk_ref[n, :, 0:DIFF_W] = dq.astype(dqk_ref.dtype)
        dqk_ref[n, :, DIFF_W:2 * DIFF_W] = dk.astype(dqk_ref.dtype)
        vt = pd[:, 2 * DIFF_W:3 * DIFF_W].T
        for hh in range(DIFF_HEADS):
            dvt_ref[n, hh, 0:DIFF_DV, :] = vt[hh * DIFF_DV:(hh + 1) * DIFF_DV, :].astype(dvt_ref.dtype)
            dvt_ref[n, hh, DIFF_DV:DIFF_VROWS, :] = ones_d.astype(dvt_ref.dtype)

    def swa_out(n):
        ps = st[n].pop("s")
        sq = _rope(ps[:, 0:SWA_Q_W], cs, ss, SWA_DH // 2) * (SWA_DH ** -0.5 * LOG2_E)
        sq_ref[n] = sq.astype(sq_ref.dtype)
        sk = _rope(ps[:, SWA_Q_W:SWA_Q_W + SWA_KV_W], cs, ss, SWA_DH // 2)
        sw = pltpu.roll(sk, SWA_DH, axis=1)
        sk_ref[n, :, 0:LANES] = jnp.where(low, sk, sw).astype(sk_ref.dtype)
        sk_ref[n, :, LANES:2 * LANES] = jnp.where(low, sw, sk).astype(sk_ref.dtype)
        svt = ps[:, SWA_Q_W + SWA_KV_W:SWA_Q_W + 2 * SWA_KV_W].T
        for grp in range(SWA_KV_HEADS):
            for i in range(TOK // SWA_TQ):
                svt_ref[n, grp, i, 0:SWA_DH, :] = svt[grp * SWA_DH:(grp + 1) * SWA_DH,
                                                      i * SWA_TQ:(i + 1) * SWA_TQ].astype(svt_ref.dtype)
                svt_ref[n, grp, i, SWA_DH:SWA_VROWS, :] = ones_s.astype(svt_ref.dtype)

    def gla_out(n):
        pg = st[n].pop("g")
        gla_ref[n, :, 0:GLA_W] = pg[:, 0:GLA_W] * (GLA_DK ** -0.5)
        gla_ref[n, :, GLA_W:4 * GLA_W] = pg[:, GLA_W:4 * GLA_W]

    def steps(n):
        return [
            lambda: norm(n),
            lambda: project(n, "z", C_Z, C_Z + LANES),
            lambda: project(n, "d", C_DQ, C_DV + DIFF_W),
            lambda: gate_out(n),
            lambda: project(n, "s", C_SQ, C_SV + SWA_KV_W),
            lambda: diff_out(n),
            lambda: project(n, "g", C_GQ, C_GO + GLA_W),
            lambda: swa_out(n),
            lambda: gla_out(n),
        ]

    lag = 4
    plans = [steps(n) for n in range(sps)]
    for tick in range(len(plans[0]) + lag * (sps - 1)):
        for n in range(sps):
            i = tick - lag * n
            if 0 <= i < len(plans[n]):
                plans[n][i]()


def _inproj_call(h, mods, layer, ctx_row, w_in_p, wg, bg, tabs):
    split_input = isinstance(h, tuple)
    if split_input:
        x, ctx = h
        bsz, n_lat, d = x.shape
        assert ctx.shape[1] == TOK
        s = n_lat + TOK
    else:
        bsz, s, d = h.shape
    nt = s // TOK
    act = MXU_DTYPE
    sps = 2 if (bsz % 2 == 0 and ctx_row % 2 == 0) else 1
    tab_spec = pl.BlockSpec((TOK, LANES), lambda b, j: (j, 0))

    def mod_index(b, j):
        return (layer, jnp.where(j == 0, ctx_row // sps, b), 0, 0)

    if split_input:
        stream_in = [x, ctx]
        stream_specs = [pl.BlockSpec((sps, TOK, d), lambda b, j: (b, jnp.maximum(j - 1, 0), 0)),
                        pl.BlockSpec((sps, TOK, d), lambda b, j: (b, 0, 0))]
        extra_specs = [pl.BlockSpec((sps, TOK, d), lambda b, j: (b, j, 0))]
        extra_shapes = [jax.ShapeDtypeStruct((bsz, s, d), F32)]
    else:
        stream_in = [h]
        stream_specs = [pl.BlockSpec((sps, TOK, d), lambda b, j: (b, j, 0))]
        extra_specs, extra_shapes = [], []

    return pl.pallas_call(
        functools.partial(_inproj_kernel, split_input=split_input),
        grid=(bsz // sps, nt),
        in_specs=stream_specs + [
            pl.BlockSpec((None, sps, 6, d), mod_index),
            pl.BlockSpec((d, IN_PAD_W), lambda b, j: (0, 0)),
            pl.BlockSpec((LANES, 2 * GLA_W), lambda b, j: (0, 0)),
            pl.BlockSpec((1, 2 * GLA_W), lambda b, j: (0, 0)),
            tab_spec, tab_spec, tab_spec, tab_spec,
        ],
        out_specs=[
            pl.BlockSpec((sps, TOK, 4 * GLA_W), lambda b, j: (b, j, 0)),
            pl.BlockSpec((sps, TOK, 2 * GLA_W), lambda b, j: (b, j, 0)),
            pl.BlockSpec((sps, TOK, 2 * DIFF_W), lambda b, j: (b, j, 0)),
            pl.BlockSpec((sps, DIFF_HEADS, DIFF_VROWS, TOK), lambda b, j: (b, 0, 0, j)),
            pl.BlockSpec((sps, TOK, SWA_Q_W), lambda b, j: (b, j, 0)),
            pl.BlockSpec((sps, TOK, 2 * LANES), lambda b, j: (b, j, 0)),
            pl.BlockSpec((sps, SWA_KV_HEADS, TOK // SWA_TQ, SWA_VROWS, SWA_TQ), lambda b, j: (b, 0, j, 0, 0)),
        ] + extra_specs,
        out_shape=[
            jax.ShapeDtypeStruct((bsz, s, 4 * GLA_W), F32),
            jax.ShapeDtypeStruct((bsz, s, 2 * GLA_W), F32),
            jax.ShapeDtypeStruct((bsz, s, 2 * DIFF_W), act),
            jax.ShapeDtypeStruct((bsz, DIFF_HEADS, DIFF_VROWS, s), act),
            jax.ShapeDtypeStruct((bsz, s, SWA_Q_W), act),
            jax.ShapeDtypeStruct((bsz, s, 2 * LANES), act),
            jax.ShapeDtypeStruct((bsz, SWA_KV_HEADS, s // SWA_TQ, SWA_VROWS, SWA_TQ), act),
        ] + extra_shapes,
        compiler_params=pltpu.CompilerParams(
            dimension_semantics=("arbitrary", "arbitrary"), vmem_limit_bytes=VMEM_LIMIT),
        name="in_projection",
    )(*stream_in, mods, w_in_p, wg, bg, *tabs)


def _gla_direction(gla_ref, gate_ref, gate_col, o_ref, st_ref, reverse, consts):
    tri, head_lane, blockdiag, sub_masks = consts
    c_, sub = GLA_CHUNK, GLA_SUB
    nsub = c_ // sub
    nchunk = TOK // c_
    g_all = gate_ref[:, gate_col:gate_col + GLA_W]
    b_all = _split_mm_t(tri, g_all)
    order = range(nchunk - 1, -1, -1) if reverse else range(nchunk)
    zero = jnp.zeros((), F32)
    chunks = []
    for c in order:
        r0 = c * c_
        q = gla_ref[r0:r0 + c_, 0:GLA_W]
        k = gla_ref[r0:r0 + c_, GLA_W:2 * GLA_W]
        v = gla_ref[r0:r0 + c_, 2 * GLA_W:3 * GLA_W]
        b = b_all[r0:r0 + c_, :]
        b_end = b[0:1, :] if reverse else b[c_ - 1:c_, :]
        atts = []
        for i in range(nsub):
            t0, t1 = i * sub, (i + 1) * sub
            if reverse:
                ref_b = b[t1:t1 + 1, :] if i < nsub - 1 else jnp.zeros((1, GLA_W), F32)
                k0, k1 = t0, c_
            else:
                ref_b = b[t0 - 1:t0, :] if i > 0 else jnp.zeros((1, GLA_W), F32)
                k0, k1 = 0, t1
            qd = q[t0:t1, :] * jnp.exp(b[t0:t1, :] - ref_b)
            lhs = jnp.concatenate([jnp.where(head_lane[hh], qd, zero) for hh in range(GLA_HEADS)], axis=0)
            kk = k[k0:k1, :] * jnp.exp(ref_b - b[k0:k1, :])
            atts.append((_mm_nt(lhs, kk), k0, k1))
        upd = _mm_tn(v, k * jnp.exp(b_end - b))
        chunks.append(dict(r0=r0, v=v, atts=atts, upd=upd, qs=q * jnp.exp(b), decay=jnp.exp(b_end)))
    for ch in chunks:
        o_parts = []
        for i, (att, k0, k1) in enumerate(ch["atts"]):
            res = _mm(jnp.where(sub_masks[(reverse, i)], att, zero), ch["v"][k0:k1, :])
            o_i = res[(GLA_HEADS - 1) * sub:GLA_HEADS * sub, :]
            for hh in range(GLA_HEADS - 2, -1, -1):
                o_i = jnp.where(head_lane[hh], res[hh * sub:(hh + 1) * sub, :], o_i)
            o_parts.append(o_i)
        ch["o_intra"] = jnp.concatenate(o_parts, axis=0)
    st = st_ref[...]
    for ch in chunks:
        o_ref[ch["r0"]:ch["r0"] + c_, :] = ch["o_intra"] + _mm_nt(ch["qs"], st)
        st = st * ch["decay"] + jnp.where(blockdiag, ch["upd"], zero)
    st_ref[...] = st


def _gla_consts():
    c_, sub = GLA_CHUNK, GLA_SUB
    nsub = c_ // sub
    r = lax.broadcasted_iota(jnp.int32, (TOK, TOK), 0)
    cc = lax.broadcasted_iota(jnp.int32, (TOK, TOK), 1)
    same_chunk = (r // c_) == (cc // c_)
    tri_f = jnp.where(same_chunk & (cc <= r), 1.0, 0.0).astype(F32)
    tri_b = jnp.where(same_chunk & (cc >= r), 1.0, 0.0).astype(F32)
    lane = lax.broadcasted_iota(jnp.int32, (1, GLA_W), 1)
    head_lane = [(lane // GLA_DK) == hh for hh in range(GLA_HEADS)]
    blockdiag = (r // GLA_DV) == (cc // GLA_DK)
    sub_masks = {}
    for reverse in (False, True):
        for i in range(nsub):
            nk = (nsub - i) * sub if reverse else (i + 1) * sub
            rr = lax.broadcasted_iota(jnp.int32, (GLA_HEADS * sub, nk), 0) % sub
            kc = lax.broadcasted_iota(jnp.int32, (GLA_HEADS * sub, nk), 1)
            if reverse:
                sub_masks[(reverse, i)] = kc >= rr
            else:
                sub_masks[(reverse, i)] = (kc - (nk - sub)) <= rr
    return tri_f, tri_b, head_lane, blockdiag, sub_masks


def _gla_kernel(gla_f_ref, gate_f_ref, gla_b_ref, gate_b_ref, of_ref, ob_ref, stf_ref, stb_ref):
    @pl.when(pl.program_id(1) == 0)
    def _():
        stf_ref[...] = jnp.zeros_like(stf_ref)
        stb_ref[...] = jnp.zeros_like(stb_ref)

    tri_f, tri_b, head_lane, blockdiag, sub_masks = _gla_consts()
    for smp in range(gla_f_ref.shape[0]):
        _gla_direction(gla_f_ref.at[smp], gate_f_ref.at[smp], 0, of_ref.at[smp], stf_ref.at[smp], False,
                       (tri_f, head_lane, blockdiag, sub_masks))
        _gla_direction(gla_b_ref.at[smp], gate_b_ref.at[smp], GLA_W, ob_ref.at[smp], stb_ref.at[smp], True,
                       (tri_b, head_lane, blockdiag, sub_masks))


def _gla_call(gla_in, gates):
    bsz, s, _ = gla_in.shape
    nt = s // TOK
    sps = 2 if bsz % 2 == 0 else 1

    def fwd(b, j):
        return (b, j, 0)

    def bwd(b, j):
        return (b, jnp.where(j == 0, 0, nt - j), 0)

    return pl.pallas_call(
        _gla_kernel,
        grid=(bsz // sps, nt),
        in_specs=[
            pl.BlockSpec((sps, TOK, 4 * GLA_W), fwd),
            pl.BlockSpec((sps, TOK, 2 * GLA_W), fwd),
            pl.BlockSpec((sps, TOK, 4 * GLA_W), bwd),
            pl.BlockSpec((sps, TOK, 2 * GLA_W), bwd),
        ],
        out_specs=[
            pl.BlockSpec((sps, TOK, GLA_W), fwd),
            pl.BlockSpec((sps, TOK, GLA_W), bwd),
        ],
        out_shape=[jax.ShapeDtypeStruct((bsz, s, GLA_W), F32)] * 2,
        scratch_shapes=[pltpu.VMEM((sps, GLA_W, GLA_W), F32), pltpu.VMEM((sps, GLA_W, GLA_W), F32)],
        compiler_params=pltpu.CompilerParams(
            dimension_semantics=("arbitrary", "arbitrary"), vmem_limit_bytes=VMEM_LIMIT),
        name="gla_scan",
    )(gla_in, gates, gla_in, gates)


def _diff_kernel(q_ref, k_ref, vt_ref, lam_ref, lam0_ref, g_ref, o_ref, *, n_ctx):
    lam_p = lam_ref[...]
    lam = (jnp.exp(jnp.sum(lam_p[0:1, :] * lam_p[1:2, :], axis=-1, keepdims=True))
           - jnp.exp(jnp.sum(lam_p[2:3, :] * lam_p[3:4, :], axis=-1, keepdims=True))
           + lam0_ref[...])
    lane = lax.broadcasted_iota(jnp.int32, (1, DIFF_W), 1)

    def attend(nk, parts):
        step = nk // parts
        per_smp = 2 * DIFF_HEADS
        n_str = q_ref.shape[0] * per_smp

        def scores(idx):
            smp, hc = divmod(idx, per_smp)
            q = q_ref[smp]
            lo = hc * DIFF_DH
            qm = jnp.where((lane >= lo) & (lane < lo + DIFF_DH), q, jnp.zeros_like(q))
            return [_mm_nt(k_ref[smp, i * step:(i + 1) * step, :], qm).astype(MXU_DTYPE)
                    for i in range(parts)]

        def probs(sts):
            m = _col_max(sts[0])
            for st in sts[1:]:
                m = jnp.maximum(m, _col_max(st))
            return [jnp.exp2(st - m) for st in sts]

        def values(idx, ps):
            smp, hc = divmod(idx, per_smp)
            acc = None
            for i, p in enumerate(ps):
                pv = _mm(vt_ref[smp, hc // 2, :, i * step:(i + 1) * step], p)
                acc = pv if acc is None else acc + pv
            return acc[0:DIFF_DV, :] / acc[DIFF_DV:DIFF_DV + 1, :]

        def finish(smp, comps):
            heads = []
            for hh in range(DIFF_HEADS):
                o_h = comps[2 * hh] - lam * comps[2 * hh + 1]
                ms = jnp.mean(o_h * o_h, axis=0, keepdims=True)
                heads.append(o_h * lax.rsqrt(ms + LN_EPS))
            out = jnp.concatenate(heads, axis=0).T
            o_ref[smp] = (out * g_ref[...]).astype(o_ref.dtype)

        sts = {0: scores(0), 1: scores(1)}
        ps = {0: probs(sts.pop(0))}
        comps = []
        for idx in range(n_str):
            if idx + 2 < n_str:
                sts[idx + 2] = scores(idx + 2)
            if idx + 1 < n_str:
                ps[idx + 1] = probs(sts.pop(idx + 1))
            comps.append(values(idx, ps.pop(idx)))
            if len(comps) == per_smp:
                finish(idx // per_smp, comps)
                comps = []

    j = pl.program_id(1)

    @pl.when(j == 0)
    def _():
        attend(n_ctx, 1)

    @pl.when(j > 0)
    def _():
        attend(k_ref.shape[1], 2)


def _col_max(x):
    r = x.shape[0]
    slab = 16
    while r % (2 * slab) == 0 and r // slab > 32:
        slab *= 2
    acc = x[0:slab, :]
    for i in range(1, r // slab):
        acc = jnp.maximum(acc, x[i * slab:(i + 1) * slab, :])
    return jnp.max(acc, axis=0, keepdims=True)


def _diff_call(dqk, dvt, lam_p, lam0, g_eff):
    bsz, s, _ = dqk.shape
    nt = s // TOK
    sps = 2 if bsz % 2 == 0 else 1
    return pl.pallas_call(
        functools.partial(_diff_kernel, n_ctx=TOK),
        grid=(bsz // sps, nt),
        in_specs=[
            pl.BlockSpec((sps, TOK, DIFF_W), lambda b, j: (b, j, 0)),
            pl.BlockSpec((sps, s, DIFF_W), lambda b, j: (b, 0, 1)),
            pl.BlockSpec((sps, DIFF_HEADS, DIFF_VROWS, s), lambda b, j: (b, 0, 0, 0)),
            pl.BlockSpec((4, DIFF_DH), lambda b, j: (0, 0)),
            pl.BlockSpec((1, 1), lambda b, j: (0, 0)),
            pl.BlockSpec((1, DIFF_W), lambda b, j: (0, 0)),
        ],
        out_specs=pl.BlockSpec((sps, TOK, DIFF_W), lambda b, j: (b, j, 0)),
        out_shape=jax.ShapeDtypeStruct((bsz, s, DIFF_W), MXU_DTYPE),
        compiler_params=pltpu.CompilerParams(
            dimension_semantics=("arbitrary", "arbitrary"), vmem_limit_bytes=VMEM_LIMIT),
        name="diff_attention",
    )(dqk, dqk, dvt, lam_p, lam0, g_eff)


def _swa_kernel(q_ref, k_ref, vt_ref, sink_ref, o_ref, *, n_ctx, n_lat):
    j = pl.program_id(1)
    tq = SWA_TQ
    n_win = 3
    n_ctx_t = n_ctx // tq
    s_rows = n_ctx + n_lat
    lane = lax.broadcasted_iota(jnp.int32, (1, LANES), 1)
    low = lane < SWA_DH
    q_per_kv = SWA_HEADS // SWA_KV_HEADS
    chains = [(smp, sub, grp) for smp in range(q_ref.shape[0]) for sub in range(TOK // tq)
              for grp in range(SWA_KV_HEADS)]

    def window(sub):
        p0 = j * TOK + sub * tq - n_ctx
        start = pl.multiple_of(jnp.clip(n_ctx + p0 - tq, 0, s_rows - n_win * tq), tq)
        return p0, start

    def scores(idx):
        smp, sub, grp = chains[idx]
        _, start = window(sub)
        blocks = []
        for r in range(q_per_kv):
            hh = grp * q_per_kv + r
            qp = q_ref[smp, sub * tq:(sub + 1) * tq, (hh // 2) * LANES:(hh // 2 + 1) * LANES]
            blocks.append(jnp.where(low if hh % 2 == 0 else ~low, qp, jnp.zeros_like(qp)))
        lhs = jnp.concatenate(blocks, axis=0)
        st_c = _mm_nt(k_ref[smp, 0:n_ctx, grp * LANES:(grp + 1) * LANES], lhs)
        st_l = _mm_nt(k_ref[smp, pl.ds(start, n_win * tq), grp * LANES:(grp + 1) * LANES], lhs)
        return st_c, st_l

    def probs(idx, st):
        _, sub, grp = chains[idx]
        st_c, st_l = st
        p0, start = window(sub)
        pos_q = p0 + lax.broadcasted_iota(jnp.int32, (1, tq), 1)
        pos_k = start - n_ctx + lax.broadcasted_iota(jnp.int32, (n_win * tq, 1), 0)
        valid = (jnp.abs(pos_k - pos_q) <= WINDOW) & (pos_k >= 0) & (p0 >= 0)
        bias = jnp.where(valid, 0.0, -jnp.inf).astype(F32)
        st_l = st_l + jnp.concatenate([bias] * q_per_kv, axis=1)
        st_c = st_c.astype(MXU_DTYPE)
        st_l = st_l.astype(MXU_DTYPE)
        snk = jnp.concatenate([sink_ref[grp * q_per_kv + r:grp * q_per_kv + r + 1, :]
                               for r in range(q_per_kv)], axis=1)
        m = jnp.maximum(jnp.maximum(_col_max(st_c), _col_max(st_l)).astype(F32), snk)
        mb = m.astype(MXU_DTYPE)
        return jnp.exp2(st_c - mb), jnp.exp2(st_l - mb), jnp.exp2(snk - mb.astype(F32))

    def values(idx, pr):
        smp, sub, grp = chains[idx]
        p_c, p_l, p_snk = pr
        _, start = window(sub)
        t0 = start // tq
        acc = None
        for i in range(n_ctx_t):
            pv = _mm(vt_ref[smp, grp, i], p_c[i * tq:(i + 1) * tq, :])
            acc = pv if acc is None else acc + pv
        for i in range(n_win):
            acc = acc + _mm(vt_ref[smp, grp, t0 + i], p_l[i * tq:(i + 1) * tq, :])
        o = acc[0:SWA_DH, :] / (acc[SWA_DH:SWA_DH + 1, :] + p_snk)
        for hp in range(q_per_kv // 2):
            pair = grp * (q_per_kv // 2) + hp
            both = jnp.concatenate([o[:, (2 * hp) * tq:(2 * hp + 1) * tq],
                                    o[:, (2 * hp + 1) * tq:(2 * hp + 2) * tq]], axis=0)
            o_ref[smp, sub * tq:(sub + 1) * tq, pair * LANES:(pair + 1) * LANES] = both.T.astype(o_ref.dtype)

    n_ch = len(chains)
    sts = {0: scores(0), 1: scores(1)}
    prs = {0: probs(0, sts.pop(0))}
    for idx in range(n_ch):
        if idx + 2 < n_ch:
            sts[idx + 2] = scores(idx + 2)
        if idx + 1 < n_ch:
            prs[idx + 1] = probs(idx + 1, sts.pop(idx + 1))
        values(idx, prs.pop(idx))


def _swa_call(sq, sk, svt, sink_tile, n_ctx):
    bsz, s, _ = sq.shape
    sps = 2 if bsz % 2 == 0 else 1
    return pl.pallas_call(
        functools.partial(_swa_kernel, n_ctx=n_ctx, n_lat=s - n_ctx),
        grid=(bsz // sps, s // TOK),
        in_specs=[
            pl.BlockSpec((sps, TOK, SWA_Q_W), lambda b, j: (b, j, 0)),
            pl.BlockSpec((sps, s, 2 * LANES), lambda b, j: (b, 0, 0)),
            pl.BlockSpec((sps, SWA_KV_HEADS, s // SWA_TQ, SWA_VROWS, SWA_TQ), lambda b, j: (b, 0, 0, 0, 0)),
            pl.BlockSpec((SWA_HEADS, LANES), lambda b, j: (0, 0)),
        ],
        out_specs=pl.BlockSpec((sps, TOK, SWA_Q_W), lambda b, j: (b, j, 0)),
        out_shape=jax.ShapeDtypeStruct((bsz, s, SWA_Q_W), MXU_DTYPE),
        compiler_params=pltpu.CompilerParams(
            dimension_semantics=("arbitrary", "arbitrary"), vmem_limit_bytes=VMEM_LIMIT),
        name="windowed_gqa",
    )(sq, sk, svt, sink_tile)


POST_TOK = 2 * TOK
MXU_TILE = 256


def _ffn_chunk_bounds(hid):
    cut = -(-(hid // MXU_TILE) // 2) * MXU_TILE
    return ((0, cut), (cut, hid))


def _post_kernel(*refs, alpha, tiles_per_sample, ctx_row, paired):
    halves = POST_TOK // TOK
    n_act = 6
    if paired:
        acts = [refs[halves * i:halves * (i + 1)] for i in range(n_act)]
        load = lambda i, x: acts[i][x][...]
    else:
        acts = refs[:n_act]
        load = lambda i, x: acts[i][x * TOK:(x + 1) * TOK, :]
    (mod_ref, wo_ref, wg_ref, wu_ref, wd_ref, gng_ref, lng_ref, lnb_ref, o_ref) = refs[len(refs) - 9:]
    hid = wg_ref.shape[1]
    bounds = _ffn_chunk_bounds(hid)
    gm = _group_mean_matrix(GLA_W, GLA_DV)
    st = [dict() for _ in range(halves)]

    def out_proj(x):
        if paired:
            row = pl.program_id(0)
        else:
            g = pl.program_id(0) * halves + x
            row = jnp.where(g % tiles_per_sample == 0, ctx_row, g // tiles_per_sample)
        st[x]["mod"] = mod_ref[row]
        o = load(0, x) + load(1, x)
        ms = _split_mm(o * o, gm)
        gla = o * lax.rsqrt(ms + LN_EPS) * gng_ref[...] * _silu(load(2, x))
        st[x]["y"] = (_mm(gla, wo_ref[0:GLA_W, :])
                      + _mm(load(3, x), wo_ref[GLA_W:GLA_W + DIFF_W, :])
                      + _mm(load(4, x), wo_ref[GLA_W + DIFF_W:, :]))

    def mid_norm(x):
        mod = st[x]["mod"]
        z = alpha * load(5, x) + mod[2:3, :] * st[x].pop("y")
        hm = _ln_plain(z) * lng_ref[0:1, :] + lnb_ref[0:1, :]
        st[x]["hm"] = hm
        st[x]["u"] = (_ln_plain(hm) * (1.0 + mod[4:5, :]) + mod[3:4, :]).astype(MXU_DTYPE)

    def gate_up(x, c):
        u = st[x]["u"]
        c0, c1 = bounds[c]
        st[x]["g", c] = jnp.dot(u, wg_ref[:, c0:c1], preferred_element_type=F32)
        st[x]["p", c] = jnp.dot(u, wu_ref[:, c0:c1], preferred_element_type=F32)

    def act(x, c):
        st[x]["a", c] = (_silu(st[x].pop(("g", c))) * st[x].pop(("p", c))).astype(MXU_DTYPE)

    def down(x, c):
        c0, c1 = bounds[c]
        f = jnp.dot(st[x].pop(("a", c)), wd_ref[c0:c1, :], preferred_element_type=F32)
        st[x]["f"] = f if c == 0 else st[x]["f"] + f

    def final(x):
        rows = slice(x * TOK, (x + 1) * TOK)
        z = alpha * st[x]["hm"] + st[x]["mod"][5:6, :] * st[x]["f"]
        o_ref[rows, :] = _ln_plain(z) * lng_ref[1:2, :] + lnb_ref[1:2, :]

    a_, b_ = 0, 1
    out_proj(a_)
    out_proj(b_)
    mid_norm(a_)
    gate_up(a_, 0)
    mid_norm(b_)
    gate_up(a_, 1)
    act(a_, 0)
    gate_up(b_, 0)
    act(a_, 1)
    down(a_, 0)
    gate_up(b_, 1)
    act(b_, 0)
    down(a_, 1)
    act(b_, 1)
    down(b_, 0)
    final(a_)
    down(b_, 1)
    final(b_)


def _post_call(o_f, o_b, gla_in, diff_o, swa_o, h, mods, layer, ctx_row, w_out, w_g, w_u, w_d, gng, lng, lnb, alpha):
    bsz, s, d = h.shape
    rows = bsz * s
    assert rows % POST_TOK == 0
    hid = w_g.shape[1]
    flat = lambda a: a.reshape(rows, a.shape[-1])
    tile = lambda w: pl.BlockSpec((POST_TOK, w), lambda t: (t, 0))
    single = pl.Buffered(1)
    const = lambda shape: pl.BlockSpec(shape, lambda t: (0,) * len(shape), pipeline_mode=single)
    out = pl.pallas_call(
        functools.partial(_post_kernel, alpha=alpha, tiles_per_sample=s // TOK, ctx_row=ctx_row, paired=False),
        grid=(rows // POST_TOK,),
        in_specs=[
            tile(GLA_W), tile(GLA_W),
            pl.BlockSpec((POST_TOK, GLA_W), lambda t: (t, 3)),
            tile(DIFF_W), tile(SWA_Q_W), tile(d),
            pl.BlockSpec((None,) + mods.shape[1:], lambda t: (layer, 0, 0, 0)),
            const((d, d)), const((d, hid)), const((d, hid)), const((hid, d)),
            pl.BlockSpec((1, GLA_W), lambda t: (0, 0)),
            pl.BlockSpec((2, d), lambda t: (0, 0)), pl.BlockSpec((2, d), lambda t: (0, 0)),
        ],
        out_specs=tile(d),
        out_shape=jax.ShapeDtypeStruct((rows, d), F32),
        compiler_params=pltpu.CompilerParams(
            dimension_semantics=("arbitrary",), vmem_limit_bytes=POST_VMEM_LIMIT),
        name="post_attention",
    )(flat(o_f), flat(o_b), flat(gla_in), flat(diff_o), flat(swa_o), flat(h), mods,
      w_out, w_g, w_u, w_d, gng, lng, lnb)
    return out.reshape(bsz, s, d)


def _post_latent_call(o_f, o_b, gla_in, diff_o, swa_o, h, mods, layer, w_out, w_g, w_u, w_d, gng, lng, lnb,
                      alpha, n_ctx):
    bsz, s, d = h.shape
    n_lat = s - n_ctx
    assert n_ctx == TOK and n_lat % POST_TOK == 0
    hid = w_g.shape[1]
    halves = POST_TOK // TOK
    first = n_ctx // TOK

    def pair(w, col=0):
        return [pl.BlockSpec((None, TOK, w), functools.partial(lambda b, i, x: (b, first + halves * i + x, col), x=x))
                for x in range(halves)]

    single = pl.Buffered(1)
    const = lambda shape: pl.BlockSpec(shape, lambda b, i: (0,) * len(shape), pipeline_mode=single)
    acts = [o_f, o_b, gla_in, diff_o, swa_o, h]
    specs = pair(GLA_W) + pair(GLA_W) + pair(GLA_W, 3) + pair(DIFF_W) + pair(SWA_Q_W) + pair(d)
    return pl.pallas_call(
        functools.partial(_post_kernel, alpha=alpha, tiles_per_sample=s // TOK, ctx_row=0, paired=True),
        grid=(bsz, n_lat // POST_TOK),
        in_specs=specs + [
            pl.BlockSpec((None,) + mods.shape[1:], lambda b, i: (layer, 0, 0, 0)),
            const((d, d)), const((d, hid)), const((d, hid)), const((hid, d)),
            pl.BlockSpec((1, GLA_W), lambda b, i: (0, 0)),
            pl.BlockSpec((2, d), lambda b, i: (0, 0)), pl.BlockSpec((2, d), lambda b, i: (0, 0)),
        ],
        out_specs=pl.BlockSpec((None, POST_TOK, d), lambda b, i: (b, i, 0)),
        out_shape=jax.ShapeDtypeStruct((bsz, n_lat, d), F32),
        compiler_params=pltpu.CompilerParams(
            dimension_semantics=("arbitrary", "arbitrary"), vmem_limit_bytes=POST_VMEM_LIMIT),
        name="post_attention_latent",
    )(*[a for a in acts for _ in range(halves)], mods, w_out, w_g, w_u, w_d, gng, lng, lnb)


def _rope_tables(rows, n_ctx, dim):
    row = jnp.repeat(jnp.arange(rows, dtype=F32), GRID_W)
    col = jnp.tile(jnp.arange(GRID_W, dtype=F32), rows)
    n_freq = dim // 4
    inv = jnp.power(ROPE_BASE, -jnp.arange(n_freq, dtype=F32) / n_freq)
    ang = jnp.concatenate([row[:, None] * inv, col[:, None] * inv], axis=-1)
    cos, sin = jnp.cos(ang), jnp.sin(ang)
    reps = LANES // dim
    cos_t = jnp.tile(jnp.concatenate([cos, cos], axis=-1), (1, reps))
    sin_t = jnp.tile(jnp.concatenate([-sin, sin], axis=-1), (1, reps))
    cos_t = jnp.concatenate([jnp.ones((n_ctx, LANES), F32), cos_t], axis=0)
    sin_t = jnp.concatenate([jnp.zeros((n_ctx, LANES), F32), sin_t], axis=0)
    return cos_t, sin_t


def _reorder_w_in(w_in):
    o = 0
    parts = {}
    for name, width in (("gq", GLA_W), ("gk", GLA_W), ("gv", GLA_W), ("go", GLA_W),
                        ("zf", GLA_GATE_RANK), ("zb", GLA_GATE_RANK),
                        ("dq", DIFF_W), ("dk", DIFF_W), ("dv", DIFF_W),
                        ("sq", SWA_Q_W), ("sk", SWA_KV_W), ("sv", SWA_KV_W)):
        parts[name] = w_in[:, o:o + width]
        o += width
    cols = [parts[n] for n in ("gq", "gk", "gv", "go", "dq", "dk", "dv", "sq", "sk", "sv", "zf", "zb")]
    pad = IN_PAD_W - sum(c.shape[1] for c in cols)
    cols.append(jnp.zeros((w_in.shape[0], pad), w_in.dtype))
    return jnp.concatenate(cols, axis=1)


def kernel(x, c, ctx, c_ctx, w_ada, b_ada, w_in, w_gla_gate, b_gla_gate, gla_norm_g, diff_lambda,
           diff_norm_g, swa_sink, w_out, ln_g, ln_b, w_ffn_gate, w_ffn_up, w_ffn_down):
    bsz, n_lat, d = x.shape
    n_ctx = ctx.shape[1]
    depth = w_ada.shape[0]
    assert n_ctx == TOK and n_lat % TOK == 0 and n_lat % GRID_W == 0
    alpha = (2.0 * depth) ** 0.25
    wdt = MXU_DTYPE

    cond_rows = -(-(bsz + 1) // 8) * 8
    cond =jnp.concatenate([c, c_ctx[None, :], jnp.zeros((cond_rows - bsz - 1, d), F32)], axis=0)
    mods = _mods_call(cond, w_ada, b_ada).reshape(depth, cond_rows, 6, d)
    ctx_row = bsz

    rows = n_lat // GRID_W
    tabs = _rope_tables(rows, n_ctx, DIFF_DH) + _rope_tables(rows, n_ctx, SWA_DH)

    h = (x, ctx)
    for layer in range(depth):
        last = layer == depth - 1
        lam_init = 0.8 - 0.6 * math.exp(-0.3 * layer)
        w_in_p = _reorder_w_in(w_in[layer]).astype(wdt)
        wg = jnp.zeros((LANES, 2 * GLA_W), F32)
        wg = wg.at[0:GLA_GATE_RANK, 0:GLA_W].set(w_gla_gate[layer, 0])
        wg = wg.at[GLA_GATE_RANK:2 * GLA_GATE_RANK, GLA_W:].set(w_gla_gate[layer, 1])
        bg = b_gla_gate[layer].reshape(1, 2 * GLA_W)

        proj = _inproj_call(h, mods, layer, ctx_row, w_in_p, wg.astype(wdt), bg, tabs)
        if layer == 0:
            h = proj[-1]
        gla_in, gates, dqk, dvt, sq, sk, svt = proj[:7]
        o_f, o_b = _gla_call(gla_in, gates)
        diff_o = _diff_call(dqk, dvt, diff_lambda[layer], jnp.full((1, 1), lam_init, F32),
                            (jnp.tile(diff_norm_g[layer], DIFF_HEADS) * (1.0 - lam_init)).reshape(1, DIFF_W))
        swa_o = _swa_call(sq, sk, svt,
                          jnp.broadcast_to(swa_sink[layer][:, None] * LOG2_E, (SWA_HEADS, LANES)), n_ctx)
        gng = jnp.tile(gla_norm_g[layer], GLA_HEADS).reshape(1, GLA_W)
        w_o, w_g, w_u, w_d = (w_out[layer].astype(wdt), w_ffn_gate[layer].astype(wdt),
                              w_ffn_up[layer].astype(wdt), w_ffn_down[layer].astype(wdt))
        if last:
            h = _post_latent_call(o_f, o_b, gla_in, diff_o, swa_o, h, mods, layer, w_o, w_g, w_u, w_d,
                                  gng, ln_g[layer], ln_b[layer], alpha, n_ctx)
        else:
            h = _post_call(o_f, o_b, gla_in, diff_o, swa_o, h, mods, layer, ctx_row, w_o, w_g, w_u, w_d,
                           gng, ln_g[layer], ln_b[layer], alpha)
    return h
```

```python
import functools
import math

import jax
import jax.numpy as jnp
from jax import lax
from jax.experimental import pallas as pl
from jax.experimental.pallas import tpu as pltpu

F32 = jnp.float32
MXU_DTYPE = jnp.bfloat16

GRID_W = 64
GLA_HEADS, GLA_DK, GLA_DV = 4, 64, 64
GLA_GATE_RANK = 16
GLA_GATE_NORM = 16.0
GLA_CHUNK = 64
GLA_SUB = 16
DIFF_HEADS, DIFF_DH = 4, 32
DIFF_DV = 2 * DIFF_DH
SWA_HEADS, SWA_KV_HEADS, SWA_DH = 8, 2, 64
WINDOW = 128
ROPE_BASE = 10000.0
LN_EPS = 1e-6

GLA_W = GLA_HEADS * GLA_DK
DIFF_W = DIFF_HEADS * 2 * DIFF_DH
SWA_Q_W = SWA_HEADS * SWA_DH
SWA_KV_W = SWA_KV_HEADS * SWA_DH
LANES = 128
TOK = 256
SWA_TQ = 128
DIFF_VROWS = DIFF_DV + 16
SWA_VROWS = SWA_DH + 16
LOG2_E = math.log2(math.e)
VMEM_LIMIT = 52 * 1024 * 1024
POST_VMEM_LIMIT = 58 * 1024 * 1024


def _mm(a, b):
    return jnp.dot(a.astype(MXU_DTYPE), b.astype(MXU_DTYPE), preferred_element_type=F32)


def _mm_nt(a, b):
    return lax.dot_general(a.astype(MXU_DTYPE), b.astype(MXU_DTYPE),
                           (((1,), (1,)), ((), ())), preferred_element_type=F32)


def _mm_tn(a, b):
    return lax.dot_general(a.astype(MXU_DTYPE), b.astype(MXU_DTYPE),
                           (((0,), (0,)), ((), ())), preferred_element_type=F32)


def _split_mm(a, b_exact):
    if MXU_DTYPE == F32:
        return jnp.dot(a, b_exact.astype(F32), preferred_element_type=F32)
    hi = a.astype(jnp.bfloat16)
    r1 = a - hi.astype(F32)
    mid = r1.astype(jnp.bfloat16)
    lo = (r1 - mid.astype(F32)).astype(jnp.bfloat16)
    b = b_exact.astype(jnp.bfloat16)
    return (jnp.dot(hi, b, preferred_element_type=F32)
            + jnp.dot(mid, b, preferred_element_type=F32)
            + jnp.dot(lo, b, preferred_element_type=F32))


def _split_mm_t(b_exact, a):
    if MXU_DTYPE == F32:
        return jnp.dot(b_exact.astype(F32), a, preferred_element_type=F32)
    hi = a.astype(jnp.bfloat16)
    r1 = a - hi.astype(F32)
    mid = r1.astype(jnp.bfloat16)
    lo = (r1 - mid.astype(F32)).astype(jnp.bfloat16)
    b = b_exact.astype(jnp.bfloat16)
    return (jnp.dot(b, hi, preferred_element_type=F32)
            + jnp.dot(b, mid, preferred_element_type=F32)
            + jnp.dot(b, lo, preferred_element_type=F32))


def _ln_plain(x):
    mu = jnp.mean(x, axis=-1, keepdims=True)
    xc = x - mu
    var = jnp.mean(xc * xc, axis=-1, keepdims=True)
    return xc * lax.rsqrt(var + LN_EPS)


def _silu(x):
    return x / (1.0 + jnp.exp(-x))


def _group_mean_matrix(width, group):
    r = lax.broadcasted_iota(jnp.int32, (width, width), 0) // group
    c = lax.broadcasted_iota(jnp.int32, (width, width), 1) // group
    return jnp.where(r == c, 1.0 / group, 0.0).astype(F32)


def _mods_kernel(a_ref, w_ref, b_ref, o_ref):
    a = a_ref[...]
    o_ref[...] = jnp.dot(_silu(a), w_ref[...], preferred_element_type=F32,
                         precision=lax.Precision.HIGHEST) + b_ref[...]


def _mods_call(cond, w_ada, b_ada):
    depth, d, width = w_ada.shape
    rows = cond.shape[0]
    bn = 1536
    return pl.pallas_call(
        _mods_kernel,
        grid=(depth, width // bn),
        in_specs=[
            pl.BlockSpec((rows, d), lambda l, n: (0, 0)),
            pl.BlockSpec((None, d, bn), lambda l, n: (l, 0, n)),
            pl.BlockSpec((None, 1, bn), lambda l, n: (l, 0, n)),
        ],
        out_specs=pl.BlockSpec((None, rows, bn), lambda l, n: (l, 0, n)),
        out_shape=jax.ShapeDtypeStruct((depth, rows, width), F32),
        compiler_params=pltpu.CompilerParams(vmem_limit_bytes=VMEM_LIMIT),
        name="adaln_mods",
    )(cond, w_ada, b_ada.reshape(depth, 1, width))


C_GQ, C_GK, C_GV, C_GO = 0, 256, 512, 768
C_DQ, C_DK, C_DV = 1024, 1280, 1536
C_SQ, C_SK, C_SV = 1792, 2304, 2432
C_Z = 2560
IN_PAD_W = 2688


def _rope(x, cos, sin_signed, half):
    outs = []
    lane = lax.broadcasted_iota(jnp.int32, (1, LANES), 1)
    first = (lane % (2 * half)) < half
    for s in range(x.shape[1] // LANES):
        xs = x[:, s * LANES:(s + 1) * LANES]
        up = pltpu.roll(xs, LANES - half, axis=1)
        dn = pltpu.roll(xs, half, axis=1)
        outs.append(xs * cos + jnp.where(first, up, dn) * sin_signed)
    return outs[0] if len(outs) == 1 else jnp.concatenate(outs, axis=1)


def _inproj_kernel(*refs, split_input):
    j = pl.program_id(1)
    if split_input:
        x_ref, ctx_ref = refs[0:2]
        refs = refs[2:]
        hout_ref = refs[-1]
        refs = refs[:-1]
        sps = x_ref.shape[0]
        for n in range(sps):
            hout_ref[n] = jnp.where(j == 0, ctx_ref[n], x_ref[n])
        h_ref = hout_ref
    else:
        h_ref = refs[0]
        refs = refs[1:]
        sps = h_ref.shape[0]
    (mod_ref, w_ref, wg_ref, bg_ref, cd_ref, sd_ref, cs_ref, ss_ref,
     gla_ref, gate_ref, dqk_ref, dvt_ref, sq_ref, sk_ref, svt_ref) = refs
    cd, sd, cs, ss = cd_ref[...], sd_ref[...], cs_ref[...], ss_ref[...]
    lane = lax.broadcasted_iota(jnp.int32, (1, LANES), 1)
    low = lane < SWA_DH
    ones_d = jnp.where(lax.broadcasted_iota(jnp.int32, (DIFF_VROWS - DIFF_DV, TOK), 0) == 0, 1.0, 0.0)
    ones_s = jnp.where(lax.broadcasted_iota(jnp.int32, (SWA_VROWS - SWA_DH, SWA_TQ), 0) == 0, 1.0, 0.0)
    st = [dict() for _ in range(sps)]

    def norm(n):
        mod = mod_ref[jnp.where(j == 0, 0, n)]
        st[n]["u"] = (_ln_plain(h_ref[n]) * (1.0 + mod[1:2, :]) + mod[0:1, :]).astype(MXU_DTYPE)

    def project(n, name, c0, c1):
        st[n][name] = jnp.dot(st[n]["u"], w_ref[:, c0:c1], preferred_element_type=F32)

    def gate_out(n):
        gpre = _mm(st[n].pop("z"), wg_ref[...]) + bg_ref[...]
        logsig = jnp.minimum(gpre, 0.0) - jnp.log(1.0 + jnp.exp(-jnp.abs(gpre)))
        gate_ref[n] = logsig * (1.0 / GLA_GATE_NORM)

    def diff_out(n):
        pd = st[n].pop("d")
        dq = _rope(pd[:, 0:DIFF_W], cd, sd, DIFF_DH // 2) * (DIFF_DH ** -0.5 * LOG2_E)
        dk = _rope(pd[:, DIFF_W:2 * DIFF_W], cd, sd, DIFF_DH // 2)
        dqk_ref[n, :, 0:DIFF_W] = dq.astype(dqk_ref.dtype)
        dqk_ref[n, :, DIFF_W:2 * DIFF_W] = dk.astype(dqk_ref.dtype)
        vt = pd[:, 2 * DIFF_W:3 * DIFF_W].T
        for hh in range(DIFF_HEADS):
            dvt_ref[n, hh, 0:DIFF_DV, :] = vt[hh * DIFF_DV:(hh + 1) * DIFF_DV, :].astype(dvt_ref.dtype)
            dvt_ref[n, hh, DIFF_DV:DIFF_VROWS, :] = ones_d.astype(dvt_ref.dtype)

    def swa_out(n):
        ps = st[n].pop("s")
        sq = _rope(ps[:, 0:SWA_Q_W], cs, ss, SWA_DH // 2) * (SWA_DH ** -0.5 * LOG2_E)
        sq_ref[n] = sq.astype(sq_ref.dtype)
        sk = _rope(ps[:, SWA_Q_W:SWA_Q_W + SWA_KV_W], cs, ss, SWA_DH // 2)
        sw = pltpu.roll(sk, SWA_DH, axis=1)
        sk_ref[n, :, 0:LANES] = jnp.where(low, sk, sw).astype(sk_ref.dtype)
        sk_ref[n, :, LANES:2 * LANES] = jnp.where(low, sw, sk).astype(sk_ref.dtype)
        svt = ps[:, SWA_Q_W + SWA_KV_W:SWA_Q_W + 2 * SWA_KV_W].T
        for grp in range(SWA_KV_HEADS):
            for i in range(TOK // SWA_TQ):
                svt_ref[n, grp, i, 0:SWA_DH, :] = svt[grp * SWA_DH:(grp + 1) * SWA_DH,
                                                      i * SWA_TQ:(i + 1) * SWA_TQ].astype(svt_ref.dtype)
                svt_ref[n, grp, i, SWA_DH:SWA_VROWS, :] = ones_s.astype(svt_ref.dtype)

    def gla_out(n):
        pg = st[n].pop("g")
        gla_ref[n, :, 0:GLA_W] = pg[:, 0:GLA_W] * (GLA_DK ** -0.5)
        gla_ref[n, :, GLA_W:4 * GLA_W] = pg[:, GLA_W:4 * GLA_W]

    def steps(n):
        return [
            lambda: norm(n),
            lambda: project(n, "z", C_Z, C_Z + LANES),
            lambda: project(n, "d", C_DQ, C_DV + DIFF_W),
            lambda: gate_out(n),
            lambda: project(n, "s", C_SQ, C_SV + SWA_KV_W),
            lambda: diff_out(n),
            lambda: project(n, "g", C_GQ, C_GO + GLA_W),
            lambda: swa_out(n),
            lambda: gla_out(n),
        ]

    lag = 4
    plans = [steps(n) for n in range(sps)]
    for tick in range(len(plans[0]) + lag * (sps - 1)):
        for n in range(sps):
            i = tick - lag * n
            if 0 <= i < len(plans[n]):
                plans[n][i]()


def _inproj_call(h, mods, layer, ctx_row, w_in_p, wg, bg, tabs):
    split_input = isinstance(h, tuple)
    if split_input:
        x, ctx = h
        bsz, n_lat, d = x.shape
        assert ctx.shape[1] == TOK
        s = n_lat + TOK
    else:
        bsz, s, d = h.shape
    nt = s // TOK
    act = MXU_DTYPE
    sps = 2 if (bsz % 2 == 0 and ctx_row % 2 == 0) else 1
    tab_spec = pl.BlockSpec((TOK, LANES), lambda b, j: (j, 0))

    def mod_index(b, j):
        return (layer, jnp.where(j == 0, ctx_row // sps, b), 0, 0)

    if split_input:
        stream_in = [x, ctx]
        stream_specs = [pl.BlockSpec((sps, TOK, d), lambda b, j: (b, jnp.maximum(j - 1, 0), 0)),
                        pl.BlockSpec((sps, TOK, d), lambda b, j: (b, 0, 0))]
        extra_specs = [pl.BlockSpec((sps, TOK, d), lambda b, j: (b, j, 0))]
        extra_shapes = [jax.ShapeDtypeStruct((bsz, s, d), F32)]
    else:
        stream_in = [h]
        stream_specs = [pl.BlockSpec((sps, TOK, d), lambda b, j: (b, j, 0))]
        extra_specs, extra_shapes = [], []

    return pl.pallas_call(
        functools.partial(_inproj_kernel, split_input=split_input),
        grid=(bsz // sps, nt),
        in_specs=stream_specs + [
            pl.BlockSpec((None, sps, 6, d), mod_index),
            pl.BlockSpec((d, IN_PAD_W), lambda b, j: (0, 0)),
            pl.BlockSpec((LANES, 2 * GLA_W), lambda b, j: (0, 0)),
            pl.BlockSpec((1, 2 * GLA_W), lambda b, j: (0, 0)),
            tab_spec, tab_spec, tab_spec, tab_spec,
        ],
        out_specs=[
            pl.BlockSpec((sps, TOK, 4 * GLA_W), lambda b, j: (b, j, 0)),
            pl.BlockSpec((sps, TOK, 2 * GLA_W), lambda b, j: (b, j, 0)),
            pl.BlockSpec((sps, TOK, 2 * DIFF_W), lambda b, j: (b, j, 0)),
            pl.BlockSpec((sps, DIFF_HEADS, DIFF_VROWS, TOK), lambda b, j: (b, 0, 0, j)),
            pl.BlockSpec((sps, TOK, SWA_Q_W), lambda b, j: (b, j, 0)),
            pl.BlockSpec((sps, TOK, 2 * LANES), lambda b, j: (b, j, 0)),
            pl.BlockSpec((sps, SWA_KV_HEADS, TOK // SWA_TQ, SWA_VROWS, SWA_TQ), lambda b, j: (b, 0, j, 0, 0)),
        ] + extra_specs,
        out_shape=[
            jax.ShapeDtypeStruct((bsz, s, 4 * GLA_W), F32),
            jax.ShapeDtypeStruct((bsz, s, 2 * GLA_W), F32),
            jax.ShapeDtypeStruct((bsz, s, 2 * DIFF_W), act),
            jax.ShapeDtypeStruct((bsz, DIFF_HEADS, DIFF_VROWS, s), act),
            jax.ShapeDtypeStruct((bsz, s, SWA_Q_W), act),
            jax.ShapeDtypeStruct((bsz, s, 2 * LANES), act),
            jax.ShapeDtypeStruct((bsz, SWA_KV_HEADS, s // SWA_TQ, SWA_VROWS, SWA_TQ), act),
        ] + extra_shapes,
        compiler_params=pltpu.CompilerParams(
            dimension_semantics=("arbitrary", "arbitrary"), vmem_limit_bytes=VMEM_LIMIT),
        name="in_projection",
    )(*stream_in, mods, w_in_p, wg, bg, *tabs)


def _gla_direction(gla_ref, gate_ref, gate_col, o_ref, st_ref, reverse, consts):
    tri, head_lane, blockdiag, sub_masks = consts
    c_, sub = GLA_CHUNK, GLA_SUB
    nsub = c_ // sub
    nchunk = TOK // c_
    g_all = gate_ref[:, gate_col:gate_col + GLA_W]
    b_all = _split_mm_t(tri, g_all)
    order = range(nchunk - 1, -1, -1) if reverse else range(nchunk)
    zero = jnp.zeros((), F32)
    chunks = []
    for c in order:
        r0 = c * c_
        q = gla_ref[r0:r0 + c_, 0:GLA_W]
        k = gla_ref[r0:r0 + c_, GLA_W:2 * GLA_W]
        v = gla_ref[r0:r0 + c_, 2 * GLA_W:3 * GLA_W]
        b = b_all[r0:r0 + c_, :]
        b_end = b[0:1, :] if reverse else b[c_ - 1:c_, :]
        atts = []
        for i in range(nsub):
            t0, t1 = i * sub, (i + 1) * sub
            if reverse:
                ref_b = b[t1:t1 + 1, :] if i < nsub - 1 else jnp.zeros((1, GLA_W), F32)
                k0, k1 = t0, c_
            else:
                ref_b = b[t0 - 1:t0, :] if i > 0 else jnp.zeros((1, GLA_W), F32)
                k0, k1 = 0, t1
            qd = q[t0:t1, :] * jnp.exp(b[t0:t1, :] - ref_b)
            lhs = jnp.concatenate([jnp.where(head_lane[hh], qd, zero) for hh in range(GLA_HEADS)], axis=0)
            kk = k[k0:k1, :] * jnp.exp(ref_b - b[k0:k1, :])
            atts.append((_mm_nt(lhs, kk), k0, k1))
        upd = _mm_tn(v, k * jnp.exp(b_end - b))
        chunks.append(dict(r0=r0, v=v, atts=atts, upd=upd, qs=q * jnp.exp(b), decay=jnp.exp(b_end)))
    for ch in chunks:
        o_parts = []
        for i, (att, k0, k1) in enumerate(ch["atts"]):
            res = _mm(jnp.where(sub_masks[(reverse, i)], att, zero), ch["v"][k0:k1, :])
            o_i = res[(GLA_HEADS - 1) * sub:GLA_HEADS * sub, :]
            for hh in range(GLA_HEADS - 2, -1, -1):
                o_i = jnp.where(head_lane[hh], res[hh * sub:(hh + 1) * sub, :], o_i)
            o_parts.append(o_i)
        ch["o_intra"] = jnp.concatenate(o_parts, axis=0)
    st = st_ref[...]
    for ch in chunks:
        o_ref[ch["r0"]:ch["r0"] + c_, :] = ch["o_intra"] + _mm_nt(ch["qs"], st)
        st = st * ch["decay"] + jnp.where(blockdiag, ch["upd"], zero)
    st_ref[...] = st


def _gla_consts():
    c_, sub = GLA_CHUNK, GLA_SUB
    nsub = c_ // sub
    r = lax.broadcasted_iota(jnp.int32, (TOK, TOK), 0)
    cc = lax.broadcasted_iota(jnp.int32, (TOK, TOK), 1)
    same_chunk = (r // c_) == (cc // c_)
    tri_f = jnp.where(same_chunk & (cc <= r), 1.0, 0.0).astype(F32)
    tri_b = jnp.where(same_chunk & (cc >= r), 1.0, 0.0).astype(F32)
    lane = lax.broadcasted_iota(jnp.int32, (1, GLA_W), 1)
    head_lane = [(lane // GLA_DK) == hh for hh in range(GLA_HEADS)]
    blockdiag = (r // GLA_DV) == (cc // GLA_DK)
    sub_masks = {}
    for reverse in (False, True):
        for i in range(nsub):
            nk = (nsub - i) * sub if reverse else (i + 1) * sub
            rr = lax.broadcasted_iota(jnp.int32, (GLA_HEADS * sub, nk), 0) % sub
            kc = lax.broadcasted_iota(jnp.int32, (GLA_HEADS * sub, nk), 1)
            if reverse:
                sub_masks[(reverse, i)] = kc >= rr
            else:
                sub_masks[(reverse, i)] = (kc - (nk - sub)) <= rr
    return tri_f, tri_b, head_lane, blockdiag, sub_masks


def _gla_kernel(gla_f_ref, gate_f_ref, gla_b_ref, gate_b_ref, of_ref, ob_ref, stf_ref, stb_ref):
    @pl.when(pl.program_id(1) == 0)
    def _():
        stf_ref[...] = jnp.zeros_like(stf_ref)
        stb_ref[...] = jnp.zeros_like(stb_ref)

    tri_f, tri_b, head_lane, blockdiag, sub_masks = _gla_consts()
    for smp in range(gla_f_ref.shape[0]):
        _gla_direction(gla_f_ref.at[smp], gate_f_ref.at[smp], 0, of_ref.at[smp], stf_ref.at[smp], False,
                       (tri_f, head_lane, blockdiag, sub_masks))
        _gla_direction(gla_b_ref.at[smp], gate_b_ref.at[smp], GLA_W, ob_ref.at[smp], stb_ref.at[smp], True,
                       (tri_b, head_lane, blockdiag, sub_masks))


def _gla_call(gla_in, gates):
    bsz, s, _ = gla_in.shape
    nt = s // TOK
    sps = 2 if bsz % 2 == 0 else 1

    def fwd(b, j):
        return (b, j, 0)

    def bwd(b, j):
        return (b, jnp.where(j == 0, 0, nt - j), 0)

    return pl.pallas_call(
        _gla_kernel,
        grid=(bsz // sps, nt),
        in_specs=[
            pl.BlockSpec((sps, TOK, 4 * GLA_W), fwd),
            pl.BlockSpec((sps, TOK, 2 * GLA_W), fwd),
            pl.BlockSpec((sps, TOK, 4 * GLA_W), bwd),
            pl.BlockSpec((sps, TOK, 2 * GLA_W), bwd),
        ],
        out_specs=[
            pl.BlockSpec((sps, TOK, GLA_W), fwd),
            pl.BlockSpec((sps, TOK, GLA_W), bwd),
        ],
        out_shape=[jax.ShapeDtypeStruct((bsz, s, GLA_W), F32)] * 2,
        scratch_shapes=[pltpu.VMEM((sps, GLA_W, GLA_W), F32), pltpu.VMEM((sps, GLA_W, GLA_W), F32)],
        compiler_params=pltpu.CompilerParams(
            dimension_semantics=("arbitrary", "arbitrary"), vmem_limit_bytes=VMEM_LIMIT),
        name="gla_scan",
    )(gla_in, gates, gla_in, gates)


def _diff_kernel(q_ref, k_ref, vt_ref, lam_ref, lam0_ref, g_ref, o_ref, *, n_ctx):
    lam_p = lam_ref[...]
    lam = (jnp.exp(jnp.sum(lam_p[0:1, :] * lam_p[1:2, :], axis=-1, keepdims=True))
           - jnp.exp(jnp.sum(lam_p[2:3, :] * lam_p[3:4, :], axis=-1, keepdims=True))
           + lam0_ref[...])
    lane = lax.broadcasted_iota(jnp.int32, (1, DIFF_W), 1)

    def attend(edges):
        parts = len(edges) - 1
        per_smp = 2 * DIFF_HEADS
        n_str = q_ref.shape[0] * per_smp

        def scores(idx):
            smp, hc = divmod(idx, per_smp)
            q = q_ref[smp]
            lo = hc * DIFF_DH
            qm = jnp.where((lane >= lo) & (lane < lo + DIFF_DH), q, jnp.zeros_like(q))
            return [_mm_nt(k_ref[smp, edges[i]:edges[i + 1], :], qm).astype(MXU_DTYPE)
                    for i in range(parts)]

        def probs(sts):
            m = _col_max(sts[0])
            for st in sts[1:]:
                m = jnp.maximum(m, _col_max(st))
            return [jnp.exp2(st - m) for st in sts]

        def values(idx, ps):
            smp, hc = divmod(idx, per_smp)
            acc = None
            for i, p in enumerate(ps):
                pv = _mm(vt_ref[smp, hc // 2, :, edges[i]:edges[i + 1]], p)
                acc = pv if acc is None else acc + pv
            return acc[0:DIFF_DV, :] / acc[DIFF_DV:DIFF_DV + 1, :]

        def finish(smp, comps):
            heads = []
            for hh in range(DIFF_HEADS):
                o_h = comps[2 * hh] - lam * comps[2 * hh + 1]
                ms = jnp.mean(o_h * o_h, axis=0, keepdims=True)
                heads.append(o_h * lax.rsqrt(ms + LN_EPS))
            out = jnp.concatenate(heads, axis=0).T
            o_ref[smp] = (out * g_ref[...]).astype(o_ref.dtype)

        sts = {0: scores(0), 1: scores(1)}
        ps = {0: probs(sts.pop(0))}
        comps = []
        for idx in range(n_str):
            if idx + 2 < n_str:
                sts[idx + 2] = scores(idx + 2)
            if idx + 1 < n_str:
                ps[idx + 1] = probs(sts.pop(idx + 1))
            comps.append(values(idx, ps.pop(idx)))
            if len(comps) == per_smp:
                finish(idx // per_smp, comps)
                comps = []

    j = pl.program_id(1)

    @pl.when(j == 0)
    def _():
        attend((0, n_ctx))

    @pl.when(j > 0)
    def _():
        n_keys = k_ref.shape[1]
        cut = LANES * ((n_keys // LANES) * 13 // 17)
        attend((0, cut, n_keys) if 0 < cut < n_keys else (0, n_keys))


def _col_max(x):
    r = x.shape[0]
    slab = 16
    while r % (2 * slab) == 0 and r // slab > 32:
        slab *= 2
    acc = x[0:slab, :]
    for i in range(1, r // slab):
        acc = jnp.maximum(acc, x[i * slab:(i + 1) * slab, :])
    return jnp.max(acc, axis=0, keepdims=True)


def _diff_call(dqk, dvt, lam_p, lam0, g_eff):
    bsz, s, _ = dqk.shape
    nt = s // TOK
    sps = 2 if bsz % 2 == 0 else 1
    return pl.pallas_call(
        functools.partial(_diff_kernel, n_ctx=TOK),
        grid=(bsz // sps, nt),
        in_specs=[
            pl.BlockSpec((sps, TOK, DIFF_W), lambda b, j: (b, j, 0)),
            pl.BlockSpec((sps, s, DIFF_W), lambda b, j: (b, 0, 1)),
            pl.BlockSpec((sps, DIFF_HEADS, DIFF_VROWS, s), lambda b, j: (b, 0, 0, 0)),
            pl.BlockSpec((4, DIFF_DH), lambda b, j: (0, 0)),
            pl.BlockSpec((1, 1), lambda b, j: (0, 0)),
            pl.BlockSpec((1, DIFF_W), lambda b, j: (0, 0)),
        ],
        out_specs=pl.BlockSpec((sps, TOK, DIFF_W), lambda b, j: (b, j, 0)),
        out_shape=jax.ShapeDtypeStruct((bsz, s, DIFF_W), MXU_DTYPE),
        compiler_params=pltpu.CompilerParams(
            dimension_semantics=("arbitrary", "arbitrary"), vmem_limit_bytes=VMEM_LIMIT),
        name="diff_attention",
    )(dqk, dqk, dvt, lam_p, lam0, g_eff)


def _swa_kernel(q_ref, k_ref, vt_ref, sink_ref, o_ref, *, n_ctx, n_lat):
    j = pl.program_id(1)
    tq = SWA_TQ
    n_win = 3
    n_ctx_t = n_ctx // tq
    s_rows = n_ctx + n_lat
    lane = lax.broadcasted_iota(jnp.int32, (1, LANES), 1)
    low = lane < SWA_DH
    q_per_kv = SWA_HEADS // SWA_KV_HEADS
    chains = [(smp, sub, grp) for smp in range(q_ref.shape[0]) for sub in range(TOK // tq)
              for grp in range(SWA_KV_HEADS)]

    def window(sub):
        p0 = j * TOK + sub * tq - n_ctx
        start = pl.multiple_of(jnp.clip(n_ctx + p0 - tq, 0, s_rows - n_win * tq), tq)
        return p0, start

    def scores(idx):
        smp, sub, grp = chains[idx]
        _, start = window(sub)
        blocks = []
        for r in range(q_per_kv):
            hh = grp * q_per_kv + r
            qp = q_ref[smp, sub * tq:(sub + 1) * tq, (hh // 2) * LANES:(hh // 2 + 1) * LANES]
            blocks.append(jnp.where(low if hh % 2 == 0 else ~low, qp, jnp.zeros_like(qp)))
        lhs = jnp.concatenate(blocks, axis=0)
        st_c = _mm_nt(k_ref[smp, 0:n_ctx, grp * LANES:(grp + 1) * LANES], lhs)
        st_l = _mm_nt(k_ref[smp, pl.ds(start, n_win * tq), grp * LANES:(grp + 1) * LANES], lhs)
        return st_c, st_l

    def probs(idx, st):
        _, sub, grp = chains[idx]
        st_c, st_l = st
        p0, start = window(sub)
        pos_q = p0 + lax.broadcasted_iota(jnp.int32, (1, tq), 1)
        pos_k = start - n_ctx + lax.broadcasted_iota(jnp.int32, (n_win * tq, 1), 0)
        valid = (jnp.abs(pos_k - pos_q) <= WINDOW) & (pos_k >= 0) & (p0 >= 0)
        bias = jnp.where(valid, 0.0, -jnp.inf).astype(F32)
        st_l = st_l + jnp.concatenate([bias] * q_per_kv, axis=1)
        st_c = st_c.astype(MXU_DTYPE)
        st_l = st_l.astype(MXU_DTYPE)
        snk = jnp.concatenate([sink_ref[grp * q_per_kv + r:grp * q_per_kv + r + 1, :]
                               for r in range(q_per_kv)], axis=1)
        m = jnp.maximum(jnp.maximum(_col_max(st_c), _col_max(st_l)).astype(F32), snk)
        mb = m.astype(MXU_DTYPE)
        return jnp.exp2(st_c - mb), jnp.exp2(st_l - mb), jnp.exp2(snk - mb.astype(F32))

    def values(idx, pr):
        smp, sub, grp = chains[idx]
        p_c, p_l, p_snk = pr
        _, start = window(sub)
        t0 = start // tq
        acc = None
        for i in range(n_ctx_t):
            pv = _mm(vt_ref[smp, grp, i], p_c[i * tq:(i + 1) * tq, :])
            acc = pv if acc is None else acc + pv
        for i in range(n_win):
            acc = acc + _mm(vt_ref[smp, grp, t0 + i], p_l[i * tq:(i + 1) * tq, :])
        o = acc[0:SWA_DH, :] / (acc[SWA_DH:SWA_DH + 1, :] + p_snk)
        for hp in range(q_per_kv // 2):
            pair = grp * (q_per_kv // 2) + hp
            both = jnp.concatenate([o[:, (2 * hp) * tq:(2 * hp + 1) * tq],
                                    o[:, (2 * hp + 1) * tq:(2 * hp + 2) * tq]], axis=0)
            o_ref[smp, sub * tq:(sub + 1) * tq, pair * LANES:(pair + 1) * LANES] = both.T.astype(o_ref.dtype)

    n_ch = len(chains)
    sts = {0: scores(0), 1: scores(1)}
    prs = {0: probs(0, sts.pop(0))}
    for idx in range(n_ch):
        if idx + 2 < n_ch:
            sts[idx + 2] = scores(idx + 2)
        if idx + 1 < n_ch:
            prs[idx + 1] = probs(idx + 1, sts.pop(idx + 1))
        values(idx, prs.pop(idx))


def _swa_call(sq, sk, svt, sink_tile, n_ctx):
    bsz, s, _ = sq.shape
    sps = 2 if bsz % 2 == 0 else 1
    return pl.pallas_call(
        functools.partial(_swa_kernel, n_ctx=n_ctx, n_lat=s - n_ctx),
        grid=(bsz // sps, s // TOK),
        in_specs=[
            pl.BlockSpec((sps, TOK, SWA_Q_W), lambda b, j: (b, j, 0)),
            pl.BlockSpec((sps, s, 2 * LANES), lambda b, j: (b, 0, 0)),
            pl.BlockSpec((sps, SWA_KV_HEADS, s // SWA_TQ, SWA_VROWS, SWA_TQ), lambda b, j: (b, 0, 0, 0, 0)),
            pl.BlockSpec((SWA_HEADS, LANES), lambda b, j: (0, 0)),
        ],
        out_specs=pl.BlockSpec((sps, TOK, SWA_Q_W), lambda b, j: (b, j, 0)),
        out_shape=jax.ShapeDtypeStruct((bsz, s, SWA_Q_W), MXU_DTYPE),
        compiler_params=pltpu.CompilerParams(
            dimension_semantics=("arbitrary", "arbitrary"), vmem_limit_bytes=VMEM_LIMIT),
        name="windowed_gqa",
    )(sq, sk, svt, sink_tile)


POST_TOK = 2 * TOK
MXU_TILE = 256


def _ffn_chunk_bounds(hid):
    cut = -(-(hid // MXU_TILE) // 2) * MXU_TILE
    return ((0, cut), (cut, hid))


def _post_kernel(*refs, alpha, tiles_per_sample, ctx_row, paired):
    halves = POST_TOK // TOK
    n_act = 6
    if paired:
        acts = [refs[halves * i:halves * (i + 1)] for i in range(n_act)]
        load = lambda i, x: acts[i][x][...]
    else:
        acts = refs[:n_act]
        load = lambda i, x: acts[i][x * TOK:(x + 1) * TOK, :]
    (mod_ref, wo_ref, wg_ref, wu_ref, wd_ref, gng_ref, lng_ref, lnb_ref, o_ref) = refs[len(refs) - 9:]
    hid = wg_ref.shape[1]
    bounds = _ffn_chunk_bounds(hid)
    gm = _group_mean_matrix(GLA_W, GLA_DV)
    st = [dict() for _ in range(halves)]

    def out_proj(x):
        if paired:
            row = pl.program_id(0)
        else:
            g = pl.program_id(0) * halves + x
            row = jnp.where(g % tiles_per_sample == 0, ctx_row, g // tiles_per_sample)
        st[x]["mod"] = mod_ref[row]
        o = load(0, x) + load(1, x)
        ms = _split_mm(o * o, gm)
        gla = o * lax.rsqrt(ms + LN_EPS) * gng_ref[...] * _silu(load(2, x))
        st[x]["y"] = (_mm(gla, wo_ref[0:GLA_W, :])
                      + _mm(load(3, x), wo_ref[GLA_W:GLA_W + DIFF_W, :])
                      + _mm(load(4, x), wo_ref[GLA_W + DIFF_W:, :]))

    def mid_norm(x):
        mod = st[x]["mod"]
        z = alpha * load(5, x) + mod[2:3, :] * st[x].pop("y")
        hm = _ln_plain(z) * lng_ref[0:1, :] + lnb_ref[0:1, :]
        st[x]["hm"] = hm
        st[x]["u"] = (_ln_plain(hm) * (1.0 + mod[4:5, :]) + mod[3:4, :]).astype(MXU_DTYPE)

    def gate_up(x, c):
        u = st[x]["u"]
        c0, c1 = bounds[c]
        st[x]["g", c] = jnp.dot(u, wg_ref[:, c0:c1], preferred_element_type=F32)
        st[x]["p", c] = jnp.dot(u, wu_ref[:, c0:c1], preferred_element_type=F32)

    def act(x, c):
        st[x]["a", c] = (_silu(st[x].pop(("g", c))) * st[x].pop(("p", c))).astype(MXU_DTYPE)

    def down(x, c):
        c0, c1 = bounds[c]
        f = jnp.dot(st[x].pop(("a", c)), wd_ref[c0:c1, :], preferred_element_type=F32)
        st[x]["f"] = f if c == 0 else st[x]["f"] + f

    def final(x):
        rows = slice(x * TOK, (x + 1) * TOK)
        z = alpha * st[x]["hm"] + st[x]["mod"][5:6, :] * st[x]["f"]
        o_ref[rows, :] = _ln_plain(z) * lng_ref[1:2, :] + lnb_ref[1:2, :]

    a_, b_ = 0, 1
    out_proj(a_)
    out_proj(b_)
    mid_norm(a_)
    gate_up(a_, 0)
    mid_norm(b_)
    gate_up(a_, 1)
    act(a_, 0)
    gate_up(b_, 0)
    act(a_, 1)
    down(a_, 0)
    gate_up(b_, 1)
    act(b_, 0)
    down(a_, 1)
    act(b_, 1)
    down(b_, 0)
    final(a_)
    down(b_, 1)
    final(b_)


def _post_call(o_f, o_b, gla_in, diff_o, swa_o, h, mods, layer, ctx_row, w_out, w_g, w_u, w_d, gng, lng, lnb, alpha):
    bsz, s, d = h.shape
    rows = bsz * s
    assert rows % POST_TOK == 0
    hid = w_g.shape[1]
    flat = lambda a: a.reshape(rows, a.shape[-1])
    tile = lambda w: pl.BlockSpec((POST_TOK, w), lambda t: (t, 0))
    single = pl.Buffered(1)
    const = lambda shape: pl.BlockSpec(shape, lambda t: (0,) * len(shape), pipeline_mode=single)
    out = pl.pallas_call(
        functools.partial(_post_kernel, alpha=alpha, tiles_per_sample=s // TOK, ctx_row=ctx_row, paired=False),
        grid=(rows // POST_TOK,),
        in_specs=[
            tile(GLA_W), tile(GLA_W),
            pl.BlockSpec((POST_TOK, GLA_W), lambda t: (t, 3)),
            tile(DIFF_W), tile(SWA_Q_W), tile(d),
            pl.BlockSpec((None,) + mods.shape[1:], lambda t: (layer, 0, 0, 0)),
            const((d, d)), const((d, hid)), const((d, hid)), const((hid, d)),
            pl.BlockSpec((1, GLA_W), lambda t: (0, 0)),
            pl.BlockSpec((2, d), lambda t: (0, 0)), pl.BlockSpec((2, d), lambda t: (0, 0)),
        ],
        out_specs=tile(d),
        out_shape=jax.ShapeDtypeStruct((rows, d), F32),
        compiler_params=pltpu.CompilerParams(
            dimension_semantics=("arbitrary",), vmem_limit_bytes=POST_VMEM_LIMIT),
        name="post_attention",
    )(flat(o_f), flat(o_b), flat(gla_in), flat(diff_o), flat(swa_o), flat(h), mods,
      w_out, w_g, w_u, w_d, gng, lng, lnb)
    return out.reshape(bsz, s, d)


def _post_latent_call(o_f, o_b, gla_in, diff_o, swa_o, h, mods, layer, w_out, w_g, w_u, w_d, gng, lng, lnb,
                      alpha, n_ctx):
    bsz, s, d = h.shape
    n_lat = s - n_ctx
    assert n_ctx == TOK and n_lat % POST_TOK == 0
    hid = w_g.shape[1]
    halves = POST_TOK // TOK
    first = n_ctx // TOK

    def pair(w, col=0):
        return [pl.BlockSpec((None, TOK, w), functools.partial(lambda b, i, x: (b, first + halves * i + x, col), x=x))
                for x in range(halves)]

    single = pl.Buffered(1)
    const = lambda shape: pl.BlockSpec(shape, lambda b, i: (0,) * len(shape), pipeline_mode=single)
    acts = [o_f, o_b, gla_in, diff_o, swa_o, h]
    specs = pair(GLA_W) + pair(GLA_W) + pair(GLA_W, 3) + pair(DIFF_W) + pair(SWA_Q_W) + pair(d)
    return pl.pallas_call(
        functools.partial(_post_kernel, alpha=alpha, tiles_per_sample=s // TOK, ctx_row=0, paired=True),
        grid=(bsz, n_lat // POST_TOK),
        in_specs=specs + [
            pl.BlockSpec((None,) + mods.shape[1:], lambda b, i: (layer, 0, 0, 0)),
            const((d, d)), const((d, hid)), const((d, hid)), const((hid, d)),
            pl.BlockSpec((1, GLA_W), lambda b, i: (0, 0)),
            pl.BlockSpec((2, d), lambda b, i: (0, 0)), pl.BlockSpec((2, d), lambda b, i: (0, 0)),
        ],
        out_specs=pl.BlockSpec((None, POST_TOK, d), lambda b, i: (b, i, 0)),
        out_shape=jax.ShapeDtypeStruct((bsz, n_lat, d), F32),
        compiler_params=pltpu.CompilerParams(
            dimension_semantics=("arbitrary", "arbitrary"), vmem_limit_bytes=POST_VMEM_LIMIT),
        name="post_attention_latent",
    )(*[a for a in acts for _ in range(halves)], mods, w_out, w_g, w_u, w_d, gng, lng, lnb)


def _rope_tables(rows, n_ctx, dim):
    row = jnp.repeat(jnp.arange(rows, dtype=F32), GRID_W)
    col = jnp.tile(jnp.arange(GRID_W, dtype=F32), rows)
    n_freq = dim // 4
    inv = jnp.power(ROPE_BASE, -jnp.arange(n_freq, dtype=F32) / n_freq)
    ang = jnp.concatenate([row[:, None] * inv, col[:, None] * inv], axis=-1)
    cos, sin = jnp.cos(ang), jnp.sin(ang)
    reps = LANES // dim
    cos_t = jnp.tile(jnp.concatenate([cos, cos], axis=-1), (1, reps))
    sin_t = jnp.tile(jnp.concatenate([-sin, sin], axis=-1), (1, reps))
    cos_t = jnp.concatenate([jnp.ones((n_ctx, LANES), F32), cos_t], axis=0)
    sin_t = jnp.concatenate([jnp.zeros((n_ctx, LANES), F32), sin_t], axis=0)
    return cos_t, sin_t


def _reorder_w_in(w_in):
    o = 0
    parts = {}
    for name, width in (("gq", GLA_W), ("gk", GLA_W), ("gv", GLA_W), ("go", GLA_W),
                        ("zf", GLA_GATE_RANK), ("zb", GLA_GATE_RANK),
                        ("dq", DIFF_W), ("dk", DIFF_W), ("dv", DIFF_W),
                        ("sq", SWA_Q_W), ("sk", SWA_KV_W), ("sv", SWA_KV_W)):
        parts[name] = w_in[:, o:o + width]
        o += width
    cols = [parts[n] for n in ("gq", "gk", "gv", "go", "dq", "dk", "dv", "sq", "sk", "sv", "zf", "zb")]
    pad = IN_PAD_W - sum(c.shape[1] for c in cols)
    cols.append(jnp.zeros((w_in.shape[0], pad), w_in.dtype))
    return jnp.concatenate(cols, axis=1)


def kernel(x, c, ctx, c_ctx, w_ada, b_ada, w_in, w_gla_gate, b_gla_gate, gla_norm_g, diff_lambda,
           diff_norm_g, swa_sink, w_out, ln_g, ln_b, w_ffn_gate, w_ffn_up, w_ffn_down):
    bsz, n_lat, d = x.shape
    n_ctx = ctx.shape[1]
    depth = w_ada.shape[0]
    assert n_ctx == TOK and n_lat % TOK == 0 and n_lat % GRID_W == 0
    alpha = (2.0 * depth) ** 0.25
    wdt = MXU_DTYPE

    cond_rows = -(-(bsz + 1) // 8) * 8
    cond =jnp.concatenate([c, c_ctx[None, :], jnp.zeros((cond_rows - bsz - 1, d), F32)], axis=0)
    mods = _mods_call(cond, w_ada, b_ada).reshape(depth, cond_rows, 6, d)
    ctx_row = bsz

    rows = n_lat // GRID_W
    tabs = _rope_tables(rows, n_ctx, DIFF_DH) + _rope_tables(rows, n_ctx, SWA_DH)

    h = (x, ctx)
    for layer in range(depth):
        last = layer == depth - 1
        lam_init = 0.8 - 0.6 * math.exp(-0.3 * layer)
        w_in_p = _reorder_w_in(w_in[layer]).astype(wdt)
        wg = jnp.zeros((LANES, 2 * GLA_W), F32)
        wg = wg.at[0:GLA_GATE_RANK, 0:GLA_W].set(w_gla_gate[layer, 0])
        wg = wg.at[GLA_GATE_RANK:2 * GLA_GATE_RANK, GLA_W:].set(w_gla_gate[layer, 1])
        bg = b_gla_gate[layer].reshape(1, 2 * GLA_W)

        proj = _inproj_call(h, mods, layer, ctx_row, w_in_p, wg.astype(wdt), bg, tabs)
        if layer == 0:
            h = proj[-1]
        gla_in, gates, dqk, dvt, sq, sk, svt = proj[:7]
        o_f, o_b = _gla_call(gla_in, gates)
        diff_o = _diff_call(dqk, dvt, diff_lambda[layer], jnp.full((1, 1), lam_init, F32),
                            (jnp.tile(diff_norm_g[layer], DIFF_HEADS) * (1.0 - lam_init)).reshape(1, DIFF_W))
        swa_o = _swa_call(sq, sk, svt,
                          jnp.broadcast_to(swa_sink[layer][:, None] * LOG2_E, (SWA_HEADS, LANES)), n_ctx)
        gng = jnp.tile(gla_norm_g[layer], GLA_HEADS).reshape(1, GLA_W)
        w_o, w_g, w_u, w_d = (w_out[layer].astype(wdt), w_ffn_gate[layer].astype(wdt),
                              w_ffn_up[layer].astype(wdt), w_ffn_down[layer].astype(wdt))
        if last:
            h = _post_latent_call(o_f, o_b, gla_in, diff_o, swa_o, h, mods, layer, w_o, w_g, w_u, w_d,
                                  gng, ln_g[layer], ln_b[layer], alpha, n_ctx)
        else:
            h = _post_call(o_f, o_b, gla_in, diff_o, swa_o, h, mods, layer, ctx_row, w_o, w_g, w_u, w_d,
                           gng, ln_g[layer], ln_b[layer], alpha)
    return h
```

```python
import functools
import math

import jax
import jax.numpy as jnp
from jax import lax
from jax.experimental import pallas as pl
from jax.experimental.pallas import tpu as pltpu

F32 = jnp.float32
MXU_DTYPE = jnp.bfloat16

GRID_W = 64
GLA_HEADS, GLA_DK, GLA_DV = 4, 64, 64
GLA_GATE_RANK = 16
GLA_GATE_NORM = 16.0
GLA_CHUNK = 64
GLA_SUB = 16
DIFF_HEADS, DIFF_DH = 4, 32
DIFF_DV = 2 * DIFF_DH
SWA_HEADS, SWA_KV_HEADS, SWA_DH = 8, 2, 64
WINDOW = 128
ROPE_BASE = 10000.0
LN_EPS = 1e-6

GLA_W = GLA_HEADS * GLA_DK
DIFF_W = DIFF_HEADS * 2 * DIFF_DH
SWA_Q_W = SWA_HEADS * SWA_DH
SWA_KV_W = SWA_KV_HEADS * SWA_DH
LANES = 128
MXU_TILE = 256
TOK = 256
POST_TOK = 2 * TOK
SWA_TQ = 128
DIFF_VROWS = DIFF_DV + 16
DIFF_KEY_CUT = (13, 17)
INPROJ_LAG = 4
SWA_VROWS = SWA_DH + 16
LOG2_E = math.log2(math.e)
VMEM_LIMIT = 52 * 1024 * 1024
POST_VMEM_LIMIT = 58 * 1024 * 1024


def _mm(a, b):
    return jnp.dot(a.astype(MXU_DTYPE), b.astype(MXU_DTYPE), preferred_element_type=F32)


def _mm_nt(a, b):
    return lax.dot_general(a.astype(MXU_DTYPE), b.astype(MXU_DTYPE),
                           (((1,), (1,)), ((), ())), preferred_element_type=F32)


def _mm_tn(a, b):
    return lax.dot_general(a.astype(MXU_DTYPE), b.astype(MXU_DTYPE),
                           (((0,), (0,)), ((), ())), preferred_element_type=F32)


def _split_mm(a, b_exact):
    if MXU_DTYPE == F32:
        return jnp.dot(a, b_exact.astype(F32), preferred_element_type=F32)
    hi = a.astype(jnp.bfloat16)
    r1 = a - hi.astype(F32)
    mid = r1.astype(jnp.bfloat16)
    lo = (r1 - mid.astype(F32)).astype(jnp.bfloat16)
    b = b_exact.astype(jnp.bfloat16)
    return (jnp.dot(hi, b, preferred_element_type=F32)
            + jnp.dot(mid, b, preferred_element_type=F32)
            + jnp.dot(lo, b, preferred_element_type=F32))


def _split_mm_t(b_exact, a):
    if MXU_DTYPE == F32:
        return jnp.dot(b_exact.astype(F32), a, preferred_element_type=F32)
    hi = a.astype(jnp.bfloat16)
    r1 = a - hi.astype(F32)
    mid = r1.astype(jnp.bfloat16)
    lo = (r1 - mid.astype(F32)).astype(jnp.bfloat16)
    b = b_exact.astype(jnp.bfloat16)
    return (jnp.dot(b, hi, preferred_element_type=F32)
            + jnp.dot(b, mid, preferred_element_type=F32)
            + jnp.dot(b, lo, preferred_element_type=F32))


def _ln_plain(x):
    mu = jnp.mean(x, axis=-1, keepdims=True)
    xc = x - mu
    var = jnp.mean(xc * xc, axis=-1, keepdims=True)
    return xc * lax.rsqrt(var + LN_EPS)


def _silu(x):
    return x / (1.0 + jnp.exp(-x))


def _group_mean_matrix(width, group):
    r = lax.broadcasted_iota(jnp.int32, (width, width), 0) // group
    c = lax.broadcasted_iota(jnp.int32, (width, width), 1) // group
    return jnp.where(r == c, 1.0 / group, 0.0).astype(F32)


def _mods_kernel(a_ref, w_ref, b_ref, o_ref):
    a = a_ref[...]
    o_ref[...] = jnp.dot(_silu(a), w_ref[...], preferred_element_type=F32,
                         precision=lax.Precision.HIGHEST) + b_ref[...]


def _mods_call(cond, w_ada, b_ada):
    depth, d, width = w_ada.shape
    rows = cond.shape[0]
    bn = 1536
    return pl.pallas_call(
        _mods_kernel,
        grid=(depth, width // bn),
        in_specs=[
            pl.BlockSpec((rows, d), lambda l, n: (0, 0)),
            pl.BlockSpec((None, d, bn), lambda l, n: (l, 0, n)),
            pl.BlockSpec((None, 1, bn), lambda l, n: (l, 0, n)),
        ],
        out_specs=pl.BlockSpec((None, rows, bn), lambda l, n: (l, 0, n)),
        out_shape=jax.ShapeDtypeStruct((depth, rows, width), F32),
        compiler_params=pltpu.CompilerParams(vmem_limit_bytes=VMEM_LIMIT),
        name="adaln_mods",
    )(cond, w_ada, b_ada.reshape(depth, 1, width))


C_GQ, C_GK, C_GV, C_GO = 0, 256, 512, 768
C_DQ, C_DK, C_DV = 1024, 1280, 1536
C_SQ, C_SK, C_SV = 1792, 2304, 2432
C_Z = 2560
IN_PAD_W = 2688


def _rope(x, cos, sin_signed, half):
    outs = []
    lane = lax.broadcasted_iota(jnp.int32, (1, LANES), 1)
    first = (lane % (2 * half)) < half
    for s in range(x.shape[1] // LANES):
        xs = x[:, s * LANES:(s + 1) * LANES]
        up = pltpu.roll(xs, LANES - half, axis=1)
        dn = pltpu.roll(xs, half, axis=1)
        outs.append(xs * cos + jnp.where(first, up, dn) * sin_signed)
    return outs[0] if len(outs) == 1 else jnp.concatenate(outs, axis=1)


def _inproj_kernel(*refs, split_input):
    j = pl.program_id(1)
    if split_input:
        x_ref, ctx_ref = refs[0:2]
        refs = refs[2:]
        hout_ref = refs[-1]
        refs = refs[:-1]
        sps = x_ref.shape[0]
        for n in range(sps):
            hout_ref[n] = jnp.where(j == 0, ctx_ref[n], x_ref[n])
        h_ref = hout_ref
    else:
        h_ref = refs[0]
        refs = refs[1:]
        sps = h_ref.shape[0]
    (mod_ref, w_ref, wg_ref, bg_ref, cd_ref, sd_ref, cs_ref, ss_ref,
     gla_ref, gate_ref, dqk_ref, dvt_ref, sq_ref, sk_ref, svt_ref) = refs
    cd, sd, cs, ss = cd_ref[...], sd_ref[...], cs_ref[...], ss_ref[...]
    lane = lax.broadcasted_iota(jnp.int32, (1, LANES), 1)
    low = lane < SWA_DH
    ones_d = jnp.where(lax.broadcasted_iota(jnp.int32, (DIFF_VROWS - DIFF_DV, TOK), 0) == 0, 1.0, 0.0)
    ones_s = jnp.where(lax.broadcasted_iota(jnp.int32, (SWA_VROWS - SWA_DH, SWA_TQ), 0) == 0, 1.0, 0.0)
    st = [dict() for _ in range(sps)]

    def norm(n):
        mod = mod_ref[jnp.where(j == 0, 0, n)]
        st[n]["u"] = (_ln_plain(h_ref[n]) * (1.0 + mod[1:2, :]) + mod[0:1, :]).astype(MXU_DTYPE)

    def project(n, name, c0, c1):
        st[n][name] = jnp.dot(st[n]["u"], w_ref[:, c0:c1], preferred_element_type=F32)

    def gate_out(n):
        gpre = _mm(st[n].pop("z"), wg_ref[...]) + bg_ref[...]
        logsig = jnp.minimum(gpre, 0.0) - jnp.log(1.0 + jnp.exp(-jnp.abs(gpre)))
        gate_ref[n] = logsig * (1.0 / GLA_GATE_NORM)

    def diff_out(n):
        pd = st[n].pop("d")
        dq = _rope(pd[:, 0:DIFF_W], cd, sd, DIFF_DH // 2) * (DIFF_DH ** -0.5 * LOG2_E)
        dk = _rope(pd[:, DIFF_W:2 * DIFF_W], cd, sd, DIFF_DH // 2)
        dqk_ref[n, :, 0:DIFF_W] = dq.astype(dqk_ref.dtype)
        dqk_ref[n, :, DIFF_W:2 * DIFF_W] = dk.astype(dqk_ref.dtype)
        vt = pd[:, 2 * DIFF_W:3 * DIFF_W].T
        for hh in range(DIFF_HEADS):
            dvt_ref[n, hh, 0:DIFF_DV, :] = vt[hh * DIFF_DV:(hh + 1) * DIFF_DV, :].astype(dvt_ref.dtype)
            dvt_ref[n, hh, DIFF_DV:DIFF_VROWS, :] = ones_d.astype(dvt_ref.dtype)

    def swa_out(n):
        ps = st[n].pop("s")
        sq = _rope(ps[:, 0:SWA_Q_W], cs, ss, SWA_DH // 2) * (SWA_DH ** -0.5 * LOG2_E)
        sq_ref[n] = sq.astype(sq_ref.dtype)
        sk = _rope(ps[:, SWA_Q_W:SWA_Q_W + SWA_KV_W], cs, ss, SWA_DH // 2)
        sw = pltpu.roll(sk, SWA_DH, axis=1)
        sk_ref[n, :, 0:LANES] = jnp.where(low, sk, sw).astype(sk_ref.dtype)
        sk_ref[n, :, LANES:2 * LANES] = jnp.where(low, sw, sk).astype(sk_ref.dtype)
        svt = ps[:, SWA_Q_W + SWA_KV_W:SWA_Q_W + 2 * SWA_KV_W].T
        for grp in range(SWA_KV_HEADS):
            for i in range(TOK // SWA_TQ):
                svt_ref[n, grp, i, 0:SWA_DH, :] = svt[grp * SWA_DH:(grp + 1) * SWA_DH,
                                                      i * SWA_TQ:(i + 1) * SWA_TQ].astype(svt_ref.dtype)
                svt_ref[n, grp, i, SWA_DH:SWA_VROWS, :] = ones_s.astype(svt_ref.dtype)

    def gla_out(n):
        pg = st[n].pop("g")
        gla_ref[n, :, 0:GLA_W] = pg[:, 0:GLA_W] * (GLA_DK ** -0.5)
        gla_ref[n, :, GLA_W:4 * GLA_W] = pg[:, GLA_W:4 * GLA_W]

    def steps(n):
        return [
            lambda: norm(n),
            lambda: project(n, "z", C_Z, C_Z + LANES),
            lambda: project(n, "d", C_DQ, C_DV + DIFF_W),
            lambda: gate_out(n),
            lambda: project(n, "s", C_SQ, C_SV + SWA_KV_W),
            lambda: diff_out(n),
            lambda: project(n, "g", C_GQ, C_GO + GLA_W),
            lambda: swa_out(n),
            lambda: gla_out(n),
        ]

    lag = INPROJ_LAG
    plans = [steps(n) for n in range(sps)]
    for tick in range(len(plans[0]) + lag * (sps - 1)):
        for n in range(sps):
            i = tick - lag * n
            if 0 <= i < len(plans[n]):
                plans[n][i]()


def _inproj_call(h, mods, layer, ctx_row, w_in_p, wg, bg, tabs):
    split_input = isinstance(h, tuple)
    if split_input:
        x, ctx = h
        bsz, n_lat, d = x.shape
        assert ctx.shape[1] == TOK
        s = n_lat + TOK
    else:
        bsz, s, d = h.shape
    nt = s // TOK
    act = MXU_DTYPE
    sps = 2 if (bsz % 2 == 0 and ctx_row % 2 == 0) else 1
    tab_spec = pl.BlockSpec((TOK, LANES), lambda b, j: (j, 0))

    def mod_index(b, j):
        return (layer, jnp.where(j == 0, ctx_row // sps, b), 0, 0)

    if split_input:
        stream_in = [x, ctx]
        stream_specs = [pl.BlockSpec((sps, TOK, d), lambda b, j: (b, jnp.maximum(j - 1, 0), 0)),
                        pl.BlockSpec((sps, TOK, d), lambda b, j: (b, 0, 0))]
        extra_specs = [pl.BlockSpec((sps, TOK, d), lambda b, j: (b, j, 0))]
        extra_shapes = [jax.ShapeDtypeStruct((bsz, s, d), F32)]
    else:
        stream_in = [h]
        stream_specs = [pl.BlockSpec((sps, TOK, d), lambda b, j: (b, j, 0))]
        extra_specs, extra_shapes = [], []

    return pl.pallas_call(
        functools.partial(_inproj_kernel, split_input=split_input),
        grid=(bsz // sps, nt),
        in_specs=stream_specs + [
            pl.BlockSpec((None, sps, 6, d), mod_index),
            pl.BlockSpec((d, IN_PAD_W), lambda b, j: (0, 0)),
            pl.BlockSpec((LANES, 2 * GLA_W), lambda b, j: (0, 0)),
            pl.BlockSpec((1, 2 * GLA_W), lambda b, j: (0, 0)),
            tab_spec, tab_spec, tab_spec, tab_spec,
        ],
        out_specs=[
            pl.BlockSpec((sps, TOK, 4 * GLA_W), lambda b, j: (b, j, 0)),
            pl.BlockSpec((sps, TOK, 2 * GLA_W), lambda b, j: (b, j, 0)),
            pl.BlockSpec((sps, TOK, 2 * DIFF_W), lambda b, j: (b, j, 0)),
            pl.BlockSpec((sps, DIFF_HEADS, DIFF_VROWS, TOK), lambda b, j: (b, 0, 0, j)),
            pl.BlockSpec((sps, TOK, SWA_Q_W), lambda b, j: (b, j, 0)),
            pl.BlockSpec((sps, TOK, 2 * LANES), lambda b, j: (b, j, 0)),
            pl.BlockSpec((sps, SWA_KV_HEADS, TOK // SWA_TQ, SWA_VROWS, SWA_TQ), lambda b, j: (b, 0, j, 0, 0)),
        ] + extra_specs,
        out_shape=[
            jax.ShapeDtypeStruct((bsz, s, 4 * GLA_W), F32),
            jax.ShapeDtypeStruct((bsz, s, 2 * GLA_W), F32),
            jax.ShapeDtypeStruct((bsz, s, 2 * DIFF_W), act),
            jax.ShapeDtypeStruct((bsz, DIFF_HEADS, DIFF_VROWS, s), act),
            jax.ShapeDtypeStruct((bsz, s, SWA_Q_W), act),
            jax.ShapeDtypeStruct((bsz, s, 2 * LANES), act),
            jax.ShapeDtypeStruct((bsz, SWA_KV_HEADS, s // SWA_TQ, SWA_VROWS, SWA_TQ), act),
        ] + extra_shapes,
        compiler_params=pltpu.CompilerParams(
            dimension_semantics=("arbitrary", "arbitrary"), vmem_limit_bytes=VMEM_LIMIT),
        name="in_projection",
    )(*stream_in, mods, w_in_p, wg, bg, *tabs)


def _gla_direction(gla_ref, gate_ref, gate_col, o_ref, st_ref, reverse, consts):
    tri, head_lane, blockdiag, sub_masks = consts
    c_, sub = GLA_CHUNK, GLA_SUB
    nsub = c_ // sub
    nchunk = TOK // c_
    g_all = gate_ref[:, gate_col:gate_col + GLA_W]
    b_all = _split_mm_t(tri, g_all)
    order = range(nchunk - 1, -1, -1) if reverse else range(nchunk)
    zero = jnp.zeros((), F32)
    chunks = []
    for c in order:
        r0 = c * c_
        q = gla_ref[r0:r0 + c_, 0:GLA_W]
        k = gla_ref[r0:r0 + c_, GLA_W:2 * GLA_W]
        v = gla_ref[r0:r0 + c_, 2 * GLA_W:3 * GLA_W]
        b = b_all[r0:r0 + c_, :]
        b_end = b[0:1, :] if reverse else b[c_ - 1:c_, :]
        atts = []
        for i in range(nsub):
            t0, t1 = i * sub, (i + 1) * sub
            if reverse:
                ref_b = b[t1:t1 + 1, :] if i < nsub - 1 else jnp.zeros((1, GLA_W), F32)
                k0, k1 = t0, c_
            else:
                ref_b = b[t0 - 1:t0, :] if i > 0 else jnp.zeros((1, GLA_W), F32)
                k0, k1 = 0, t1
            qd = q[t0:t1, :] * jnp.exp(b[t0:t1, :] - ref_b)
            lhs = jnp.concatenate([jnp.where(head_lane[hh], qd, zero) for hh in range(GLA_HEADS)], axis=0)
            kk = k[k0:k1, :] * jnp.exp(ref_b - b[k0:k1, :])
            atts.append((_mm_nt(lhs, kk), k0, k1))
        upd = _mm_tn(v, k * jnp.exp(b_end - b))
        chunks.append(dict(r0=r0, v=v, atts=atts, upd=upd, qs=q * jnp.exp(b), decay=jnp.exp(b_end)))
    for ch in chunks:
        o_parts = []
        for i, (att, k0, k1) in enumerate(ch["atts"]):
            res = _mm(jnp.where(sub_masks[(reverse, i)], att, zero), ch["v"][k0:k1, :])
            o_i = res[(GLA_HEADS - 1) * sub:GLA_HEADS * sub, :]
            for hh in range(GLA_HEADS - 2, -1, -1):
                o_i = jnp.where(head_lane[hh], res[hh * sub:(hh + 1) * sub, :], o_i)
            o_parts.append(o_i)
        ch["o_intra"] = jnp.concatenate(o_parts, axis=0)
    st = st_ref[...]
    for ch in chunks:
        o_ref[ch["r0"]:ch["r0"] + c_, :] = ch["o_intra"] + _mm_nt(ch["qs"], st)
        st = st * ch["decay"] + jnp.where(blockdiag, ch["upd"], zero)
    st_ref[...] = st


def _gla_consts():
    c_, sub = GLA_CHUNK, GLA_SUB
    nsub = c_ // sub
    r = lax.broadcasted_iota(jnp.int32, (TOK, TOK), 0)
    cc = lax.broadcasted_iota(jnp.int32, (TOK, TOK), 1)
    same_chunk = (r // c_) == (cc // c_)
    tri_f = jnp.where(same_chunk & (cc <= r), 1.0, 0.0).astype(F32)
    tri_b = jnp.where(same_chunk & (cc >= r), 1.0, 0.0).astype(F32)
    lane = lax.broadcasted_iota(jnp.int32, (1, GLA_W), 1)
    head_lane = [(lane // GLA_DK) == hh for hh in range(GLA_HEADS)]
    blockdiag = (r // GLA_DV) == (cc // GLA_DK)
    sub_masks = {}
    for reverse in (False, True):
        for i in range(nsub):
            nk = (nsub - i) * sub if reverse else (i + 1) * sub
            rr = lax.broadcasted_iota(jnp.int32, (GLA_HEADS * sub, nk), 0) % sub
            kc = lax.broadcasted_iota(jnp.int32, (GLA_HEADS * sub, nk), 1)
            if reverse:
                sub_masks[(reverse, i)] = kc >= rr
            else:
                sub_masks[(reverse, i)] = (kc - (nk - sub)) <= rr
    return tri_f, tri_b, head_lane, blockdiag, sub_masks


def _gla_kernel(gla_f_ref, gate_f_ref, gla_b_ref, gate_b_ref, of_ref, ob_ref, stf_ref, stb_ref):
    @pl.when(pl.program_id(1) == 0)
    def _():
        stf_ref[...] = jnp.zeros_like(stf_ref)
        stb_ref[...] = jnp.zeros_like(stb_ref)

    tri_f, tri_b, head_lane, blockdiag, sub_masks = _gla_consts()
    for smp in range(gla_f_ref.shape[0]):
        _gla_direction(gla_f_ref.at[smp], gate_f_ref.at[smp], 0, of_ref.at[smp], stf_ref.at[smp], False,
                       (tri_f, head_lane, blockdiag, sub_masks))
        _gla_direction(gla_b_ref.at[smp], gate_b_ref.at[smp], GLA_W, ob_ref.at[smp], stb_ref.at[smp], True,
                       (tri_b, head_lane, blockdiag, sub_masks))


def _gla_call(gla_in, gates):
    bsz, s, _ = gla_in.shape
    nt = s // TOK
    sps = 2 if bsz % 2 == 0 else 1

    def fwd(b, j):
        return (b, j, 0)

    def bwd(b, j):
        return (b, jnp.where(j == 0, 0, nt - j), 0)

    return pl.pallas_call(
        _gla_kernel,
        grid=(bsz // sps, nt),
        in_specs=[
            pl.BlockSpec((sps, TOK, 4 * GLA_W), fwd),
            pl.BlockSpec((sps, TOK, 2 * GLA_W), fwd),
            pl.BlockSpec((sps, TOK, 4 * GLA_W), bwd),
            pl.BlockSpec((sps, TOK, 2 * GLA_W), bwd),
        ],
        out_specs=[
            pl.BlockSpec((sps, TOK, GLA_W), fwd),
            pl.BlockSpec((sps, TOK, GLA_W), bwd),
        ],
        out_shape=[jax.ShapeDtypeStruct((bsz, s, GLA_W), F32)] * 2,
        scratch_shapes=[pltpu.VMEM((sps, GLA_W, GLA_W), F32), pltpu.VMEM((sps, GLA_W, GLA_W), F32)],
        compiler_params=pltpu.CompilerParams(
            dimension_semantics=("arbitrary", "arbitrary"), vmem_limit_bytes=VMEM_LIMIT),
        name="gla_scan",
    )(gla_in, gates, gla_in, gates)


def _diff_kernel(q_ref, k_ref, vt_ref, lam_ref, lam0_ref, g_ref, o_ref, *, n_ctx):
    lam_p = lam_ref[...]
    lam = (jnp.exp(jnp.sum(lam_p[0:1, :] * lam_p[1:2, :], axis=-1, keepdims=True))
           - jnp.exp(jnp.sum(lam_p[2:3, :] * lam_p[3:4, :], axis=-1, keepdims=True))
           + lam0_ref[...])
    lane = lax.broadcasted_iota(jnp.int32, (1, DIFF_W), 1)

    def attend(edges):
        parts = len(edges) - 1
        per_smp = 2 * DIFF_HEADS
        n_str = q_ref.shape[0] * per_smp

        def scores(idx):
            smp, hc = divmod(idx, per_smp)
            q = q_ref[smp]
            lo = hc * DIFF_DH
            qm = jnp.where((lane >= lo) & (lane < lo + DIFF_DH), q, jnp.zeros_like(q))
            return [_mm_nt(k_ref[smp, edges[i]:edges[i + 1], :], qm).astype(MXU_DTYPE)
                    for i in range(parts)]

        def probs(sts):
            m = _col_max(sts[0])
            for st in sts[1:]:
                m = jnp.maximum(m, _col_max(st))
            return [jnp.exp2(st - m) for st in sts]

        def values(idx, ps):
            smp, hc = divmod(idx, per_smp)
            acc = None
            for i, p in enumerate(ps):
                pv = _mm(vt_ref[smp, hc // 2, :, edges[i]:edges[i + 1]], p)
                acc = pv if acc is None else acc + pv
            return acc[0:DIFF_DV, :] / acc[DIFF_DV:DIFF_DV + 1, :]

        def finish(smp, comps):
            heads = []
            for hh in range(DIFF_HEADS):
                o_h = comps[2 * hh] - lam * comps[2 * hh + 1]
                ms = jnp.mean(o_h * o_h, axis=0, keepdims=True)
                heads.append(o_h * lax.rsqrt(ms + LN_EPS))
            out = jnp.concatenate(heads, axis=0).T
            o_ref[smp] = (out * g_ref[...]).astype(o_ref.dtype)

        sts = {0: scores(0), 1: scores(1)}
        ps = {0: probs(sts.pop(0))}
        comps = []
        for idx in range(n_str):
            if idx + 2 < n_str:
                sts[idx + 2] = scores(idx + 2)
            if idx + 1 < n_str:
                ps[idx + 1] = probs(sts.pop(idx + 1))
            comps.append(values(idx, ps.pop(idx)))
            if len(comps) == per_smp:
                finish(idx // per_smp, comps)
                comps = []

    j = pl.program_id(1)

    @pl.when(j == 0)
    def _():
        attend((0, n_ctx))

    @pl.when(j > 0)
    def _():
        n_keys = k_ref.shape[1]
        cut = LANES * ((n_keys // LANES) * DIFF_KEY_CUT[0] // DIFF_KEY_CUT[1])
        attend((0, cut, n_keys) if 0 < cut < n_keys else (0, n_keys))


def _col_max(x):
    r = x.shape[0]
    slab = 16
    while r % (2 * slab) == 0 and r // slab > 32:
        slab *= 2
    acc = x[0:slab, :]
    for i in range(1, r // slab):
        acc = jnp.maximum(acc, x[i * slab:(i + 1) * slab, :])
    return jnp.max(acc, axis=0, keepdims=True)


def _diff_call(dqk, dvt, lam_p, lam0, g_eff):
    bsz, s, _ = dqk.shape
    nt = s // TOK
    sps = 2 if bsz % 2 == 0 else 1
    return pl.pallas_call(
        functools.partial(_diff_kernel, n_ctx=TOK),
        grid=(bsz // sps, nt),
        in_specs=[
            pl.BlockSpec((sps, TOK, DIFF_W), lambda b, j: (b, j, 0)),
            pl.BlockSpec((sps, s, DIFF_W), lambda b, j: (b, 0, 1)),
            pl.BlockSpec((sps, DIFF_HEADS, DIFF_VROWS, s), lambda b, j: (b, 0, 0, 0)),
            pl.BlockSpec((4, DIFF_DH), lambda b, j: (0, 0)),
            pl.BlockSpec((1, 1), lambda b, j: (0, 0)),
            pl.BlockSpec((1, DIFF_W), lambda b, j: (0, 0)),
        ],
        out_specs=pl.BlockSpec((sps, TOK, DIFF_W), lambda b, j: (b, j, 0)),
        out_shape=jax.ShapeDtypeStruct((bsz, s, DIFF_W), MXU_DTYPE),
        compiler_params=pltpu.CompilerParams(
            dimension_semantics=("arbitrary", "arbitrary"), vmem_limit_bytes=VMEM_LIMIT),
        name="diff_attention",
    )(dqk, dqk, dvt, lam_p, lam0, g_eff)


def _swa_kernel(q_ref, k_ref, vt_ref, sink_ref, o_ref, *, n_ctx, n_lat):
    j = pl.program_id(1)
    tq = SWA_TQ
    n_win = 3
    n_ctx_t = n_ctx // tq
    s_rows = n_ctx + n_lat
    lane = lax.broadcasted_iota(jnp.int32, (1, LANES), 1)
    low = lane < SWA_DH
    q_per_kv = SWA_HEADS // SWA_KV_HEADS
    chains = [(smp, sub, grp) for smp in range(q_ref.shape[0]) for sub in range(TOK // tq)
              for grp in range(SWA_KV_HEADS)]

    def window(sub):
        p0 = j * TOK + sub * tq - n_ctx
        start = pl.multiple_of(jnp.clip(n_ctx + p0 - tq, 0, s_rows - n_win * tq), tq)
        return p0, start

    def scores(idx):
        smp, sub, grp = chains[idx]
        _, start = window(sub)
        blocks = []
        for r in range(q_per_kv):
            hh = grp * q_per_kv + r
            qp = q_ref[smp, sub * tq:(sub + 1) * tq, (hh // 2) * LANES:(hh // 2 + 1) * LANES]
            blocks.append(jnp.where(low if hh % 2 == 0 else ~low, qp, jnp.zeros_like(qp)))
        lhs = jnp.concatenate(blocks, axis=0)
        st_c = _mm_nt(k_ref[smp, 0:n_ctx, grp * LANES:(grp + 1) * LANES], lhs)
        st_l = _mm_nt(k_ref[smp, pl.ds(start, n_win * tq), grp * LANES:(grp + 1) * LANES], lhs)
        return st_c, st_l

    def probs(idx, st):
        _, sub, grp = chains[idx]
        st_c, st_l = st
        p0, start = window(sub)
        pos_q = p0 + lax.broadcasted_iota(jnp.int32, (1, tq), 1)
        pos_k = start - n_ctx + lax.broadcasted_iota(jnp.int32, (n_win * tq, 1), 0)
        valid = (jnp.abs(pos_k - pos_q) <= WINDOW) & (pos_k >= 0) & (p0 >= 0)
        bias = jnp.where(valid, 0.0, -jnp.inf).astype(F32)
        st_l = st_l + jnp.concatenate([bias] * q_per_kv, axis=1)
        st_c = st_c.astype(MXU_DTYPE)
        st_l = st_l.astype(MXU_DTYPE)
        snk = jnp.concatenate([sink_ref[grp * q_per_kv + r:grp * q_per_kv + r + 1, :]
                               for r in range(q_per_kv)], axis=1)
        m = jnp.maximum(jnp.maximum(_col_max(st_c), _col_max(st_l)).astype(F32), snk)
        mb = m.astype(MXU_DTYPE)
        return jnp.exp2(st_c - mb), jnp.exp2(st_l - mb), jnp.exp2(snk - mb.astype(F32))

    def values(idx, pr):
        smp, sub, grp = chains[idx]
        p_c, p_l, p_snk = pr
        _, start = window(sub)
        t0 = start // tq
        acc = None
        for i in range(n_ctx_t):
            pv = _mm(vt_ref[smp, grp, i], p_c[i * tq:(i + 1) * tq, :])
            acc = pv if acc is None else acc + pv
        for i in range(n_win):
            acc = acc + _mm(vt_ref[smp, grp, t0 + i], p_l[i * tq:(i + 1) * tq, :])
        o = acc[0:SWA_DH, :] / (acc[SWA_DH:SWA_DH + 1, :] + p_snk)
        for hp in range(q_per_kv // 2):
            pair = grp * (q_per_kv // 2) + hp
            both = jnp.concatenate([o[:, (2 * hp) * tq:(2 * hp + 1) * tq],
                                    o[:, (2 * hp + 1) * tq:(2 * hp + 2) * tq]], axis=0)
            o_ref[smp, sub * tq:(sub + 1) * tq, pair * LANES:(pair + 1) * LANES] = both.T.astype(o_ref.dtype)

    n_ch = len(chains)
    sts = {0: scores(0), 1: scores(1)}
    prs = {0: probs(0, sts.pop(0))}
    for idx in range(n_ch):
        if idx + 2 < n_ch:
            sts[idx + 2] = scores(idx + 2)
        if idx + 1 < n_ch:
            prs[idx + 1] = probs(idx + 1, sts.pop(idx + 1))
        values(idx, prs.pop(idx))


def _swa_call(sq, sk, svt, sink_tile, n_ctx):
    bsz, s, _ = sq.shape
    sps = 2 if bsz % 2 == 0 else 1
    return pl.pallas_call(
        functools.partial(_swa_kernel, n_ctx=n_ctx, n_lat=s - n_ctx),
        grid=(bsz // sps, s // TOK),
        in_specs=[
            pl.BlockSpec((sps, TOK, SWA_Q_W), lambda b, j: (b, j, 0)),
            pl.BlockSpec((sps, s, 2 * LANES), lambda b, j: (b, 0, 0)),
            pl.BlockSpec((sps, SWA_KV_HEADS, s // SWA_TQ, SWA_VROWS, SWA_TQ), lambda b, j: (b, 0, 0, 0, 0)),
            pl.BlockSpec((SWA_HEADS, LANES), lambda b, j: (0, 0)),
        ],
        out_specs=pl.BlockSpec((sps, TOK, SWA_Q_W), lambda b, j: (b, j, 0)),
        out_shape=jax.ShapeDtypeStruct((bsz, s, SWA_Q_W), MXU_DTYPE),
        compiler_params=pltpu.CompilerParams(
            dimension_semantics=("arbitrary", "arbitrary"), vmem_limit_bytes=VMEM_LIMIT),
        name="windowed_gqa",
    )(sq, sk, svt, sink_tile)


def _ffn_chunk_bounds(hid):
    cut = -(-(hid // MXU_TILE) // 2) * MXU_TILE
    return ((0, cut), (cut, hid))


def _post_kernel(*refs, alpha, tiles_per_sample, ctx_row, paired):
    halves = POST_TOK // TOK
    n_act = 6
    if paired:
        acts = [refs[halves * i:halves * (i + 1)] for i in range(n_act)]
        load = lambda i, x: acts[i][x][...]
    else:
        acts = refs[:n_act]
        load = lambda i, x: acts[i][x * TOK:(x + 1) * TOK, :]
    (mod_ref, wo_ref, wg_ref, wu_ref, wd_ref, gng_ref, lng_ref, lnb_ref, o_ref) = refs[len(refs) - 9:]
    hid = wg_ref.shape[1]
    bounds = _ffn_chunk_bounds(hid)
    gm = _group_mean_matrix(GLA_W, GLA_DV)
    st = [dict() for _ in range(halves)]

    def out_proj(x):
        if paired:
            row = pl.program_id(0)
        else:
            g = pl.program_id(0) * halves + x
            row = jnp.where(g % tiles_per_sample == 0, ctx_row, g // tiles_per_sample)
        st[x]["mod"] = mod_ref[row]
        o = load(0, x) + load(1, x)
        ms = _split_mm(o * o, gm)
        gla = o * lax.rsqrt(ms + LN_EPS) * gng_ref[...] * _silu(load(2, x))
        st[x]["y"] = (_mm(gla, wo_ref[0:GLA_W, :])
                      + _mm(load(3, x), wo_ref[GLA_W:GLA_W + DIFF_W, :])
                      + _mm(load(4, x), wo_ref[GLA_W + DIFF_W:, :]))

    def mid_norm(x):
        mod = st[x]["mod"]
        z = alpha * load(5, x) + mod[2:3, :] * st[x].pop("y")
        hm = _ln_plain(z) * lng_ref[0:1, :] + lnb_ref[0:1, :]
        st[x]["hm"] = hm
        st[x]["u"] = (_ln_plain(hm) * (1.0 + mod[4:5, :]) + mod[3:4, :]).astype(MXU_DTYPE)

    def gate_up(x, c):
        u = st[x]["u"]
        c0, c1 = bounds[c]
        st[x]["g", c] = jnp.dot(u, wg_ref[:, c0:c1], preferred_element_type=F32)
        st[x]["p", c] = jnp.dot(u, wu_ref[:, c0:c1], preferred_element_type=F32)

    def act(x, c):
        st[x]["a", c] = (_silu(st[x].pop(("g", c))) * st[x].pop(("p", c))).astype(MXU_DTYPE)

    def down(x, c):
        c0, c1 = bounds[c]
        f = jnp.dot(st[x].pop(("a", c)), wd_ref[c0:c1, :], preferred_element_type=F32)
        st[x]["f"] = f if c == 0 else st[x]["f"] + f

    def final(x):
        rows = slice(x * TOK, (x + 1) * TOK)
        z = alpha * st[x]["hm"] + st[x]["mod"][5:6, :] * st[x]["f"]
        o_ref[rows, :] = _ln_plain(z) * lng_ref[1:2, :] + lnb_ref[1:2, :]

    a_, b_ = 0, 1
    out_proj(a_)
    out_proj(b_)
    mid_norm(a_)
    gate_up(a_, 0)
    mid_norm(b_)
    gate_up(a_, 1)
    act(a_, 0)
    gate_up(b_, 0)
    act(a_, 1)
    down(a_, 0)
    gate_up(b_, 1)
    act(b_, 0)
    down(a_, 1)
    act(b_, 1)
    down(b_, 0)
    final(a_)
    down(b_, 1)
    final(b_)


def _post_call(o_f, o_b, gla_in, diff_o, swa_o, h, mods, layer, ctx_row, w_out, w_g, w_u, w_d, gng, lng, lnb, alpha):
    bsz, s, d = h.shape
    rows = bsz * s
    assert rows % POST_TOK == 0
    hid = w_g.shape[1]
    flat = lambda a: a.reshape(rows, a.shape[-1])
    tile = lambda w: pl.BlockSpec((POST_TOK, w), lambda t: (t, 0))
    single = pl.Buffered(1)
    const = lambda shape: pl.BlockSpec(shape, lambda t: (0,) * len(shape), pipeline_mode=single)
    out = pl.pallas_call(
        functools.partial(_post_kernel, alpha=alpha, tiles_per_sample=s // TOK, ctx_row=ctx_row, paired=False),
        grid=(rows // POST_TOK,),
        in_specs=[
            tile(GLA_W), tile(GLA_W),
            pl.BlockSpec((POST_TOK, GLA_W), lambda t: (t, 3)),
            tile(DIFF_W), tile(SWA_Q_W), tile(d),
            pl.BlockSpec((None,) + mods.shape[1:], lambda t: (layer, 0, 0, 0)),
            const((d, d)), const((d, hid)), const((d, hid)), const((hid, d)),
            pl.BlockSpec((1, GLA_W), lambda t: (0, 0)),
            pl.BlockSpec((2, d), lambda t: (0, 0)), pl.BlockSpec((2, d), lambda t: (0, 0)),
        ],
        out_specs=tile(d),
        out_shape=jax.ShapeDtypeStruct((rows, d), F32),
        compiler_params=pltpu.CompilerParams(
            dimension_semantics=("arbitrary",), vmem_limit_bytes=POST_VMEM_LIMIT),
        name="post_attention",
    )(flat(o_f), flat(o_b), flat(gla_in), flat(diff_o), flat(swa_o), flat(h), mods,
      w_out, w_g, w_u, w_d, gng, lng, lnb)
    return out.reshape(bsz, s, d)


def _post_latent_call(o_f, o_b, gla_in, diff_o, swa_o, h, mods, layer, w_out, w_g, w_u, w_d, gng, lng, lnb,
                      alpha, n_ctx):
    bsz, s, d = h.shape
    n_lat = s - n_ctx
    assert n_ctx == TOK and n_lat % POST_TOK == 0
    hid = w_g.shape[1]
    halves = POST_TOK // TOK
    first = n_ctx // TOK

    def pair(w, col=0):
        return [pl.BlockSpec((None, TOK, w), functools.partial(lambda b, i, x: (b, first + halves * i + x, col), x=x))
                for x in range(halves)]

    single = pl.Buffered(1)
    const = lambda shape: pl.BlockSpec(shape, lambda b, i: (0,) * len(shape), pipeline_mode=single)
    acts = [o_f, o_b, gla_in, diff_o, swa_o, h]
    specs = pair(GLA_W) + pair(GLA_W) + pair(GLA_W, 3) + pair(DIFF_W) + pair(SWA_Q_W) + pair(d)
    return pl.pallas_call(
        functools.partial(_post_kernel, alpha=alpha, tiles_per_sample=s // TOK, ctx_row=0, paired=True),
        grid=(bsz, n_lat // POST_TOK),
        in_specs=specs + [
            pl.BlockSpec((None,) + mods.shape[1:], lambda b, i: (layer, 0, 0, 0)),
            const((d, d)), const((d, hid)), const((d, hid)), const((hid, d)),
            pl.BlockSpec((1, GLA_W), lambda b, i: (0, 0)),
            pl.BlockSpec((2, d), lambda b, i: (0, 0)), pl.BlockSpec((2, d), lambda b, i: (0, 0)),
        ],
        out_specs=pl.BlockSpec((None, POST_TOK, d), lambda b, i: (b, i, 0)),
        out_shape=jax.ShapeDtypeStruct((bsz, n_lat, d), F32),
        compiler_params=pltpu.CompilerParams(
            dimension_semantics=("arbitrary", "arbitrary"), vmem_limit_bytes=POST_VMEM_LIMIT),
        name="post_attention_latent",
    )(*[a for a in acts for _ in range(halves)], mods, w_out, w_g, w_u, w_d, gng, lng, lnb)


def _rope_tables(rows, n_ctx, dim):
    row = jnp.repeat(jnp.arange(rows, dtype=F32), GRID_W)
    col = jnp.tile(jnp.arange(GRID_W, dtype=F32), rows)
    n_freq = dim // 4
    inv = jnp.power(ROPE_BASE, -jnp.arange(n_freq, dtype=F32) / n_freq)
    ang = jnp.concatenate([row[:, None] * inv, col[:, None] * inv], axis=-1)
    cos, sin = jnp.cos(ang), jnp.sin(ang)
    reps = LANES // dim
    cos_t = jnp.tile(jnp.concatenate([cos, cos], axis=-1), (1, reps))
    sin_t = jnp.tile(jnp.concatenate([-sin, sin], axis=-1), (1, reps))
    cos_t = jnp.concatenate([jnp.ones((n_ctx, LANES), F32), cos_t], axis=0)
    sin_t = jnp.concatenate([jnp.zeros((n_ctx, LANES), F32), sin_t], axis=0)
    return cos_t, sin_t


def _reorder_w_in(w_in):
    o = 0
    parts = {}
    for name, width in (("gq", GLA_W), ("gk", GLA_W), ("gv", GLA_W), ("go", GLA_W),
                        ("zf", GLA_GATE_RANK), ("zb", GLA_GATE_RANK),
                        ("dq", DIFF_W), ("dk", DIFF_W), ("dv", DIFF_W),
                        ("sq", SWA_Q_W), ("sk", SWA_KV_W), ("sv", SWA_KV_W)):
        parts[name] = w_in[:, o:o + width]
        o += width
    cols = [parts[n] for n in ("gq", "gk", "gv", "go", "dq", "dk", "dv", "sq", "sk", "sv", "zf", "zb")]
    pad = IN_PAD_W - sum(c.shape[1] for c in cols)
    cols.append(jnp.zeros((w_in.shape[0], pad), w_in.dtype))
    return jnp.concatenate(cols, axis=1)


def kernel(x, c, ctx, c_ctx, w_ada, b_ada, w_in, w_gla_gate, b_gla_gate, gla_norm_g, diff_lambda,
           diff_norm_g, swa_sink, w_out, ln_g, ln_b, w_ffn_gate, w_ffn_up, w_ffn_down):
    bsz, n_lat, d = x.shape
    n_ctx = ctx.shape[1]
    depth = w_ada.shape[0]
    assert n_ctx == TOK and n_lat % TOK == 0 and n_lat % GRID_W == 0
    alpha = (2.0 * depth) ** 0.25
    wdt = MXU_DTYPE

    cond_rows = -(-(bsz + 1) // 8) * 8
    cond =jnp.concatenate([c, c_ctx[None, :], jnp.zeros((cond_rows - bsz - 1, d), F32)], axis=0)
    mods = _mods_call(cond, w_ada, b_ada).reshape(depth, cond_rows, 6, d)
    ctx_row = bsz

    rows = n_lat // GRID_W
    tabs = _rope_tables(rows, n_ctx, DIFF_DH) + _rope_tables(rows, n_ctx, SWA_DH)

    h = (x, ctx)
    for layer in range(depth):
        last = layer == depth - 1
        lam_init = 0.8 - 0.6 * math.exp(-0.3 * layer)
        w_in_p = _reorder_w_in(w_in[layer]).astype(wdt)
        wg = jnp.zeros((LANES, 2 * GLA_W), F32)
        wg = wg.at[0:GLA_GATE_RANK, 0:GLA_W].set(w_gla_gate[layer, 0])
        wg = wg.at[GLA_GATE_RANK:2 * GLA_GATE_RANK, GLA_W:].set(w_gla_gate[layer, 1])
        bg = b_gla_gate[layer].reshape(1, 2 * GLA_W)

        proj = _inproj_call(h, mods, layer, ctx_row, w_in_p, wg.astype(wdt), bg, tabs)
        if layer == 0:
            h = proj[-1]
        gla_in, gates, dqk, dvt, sq, sk, svt = proj[:7]
        o_f, o_b = _gla_call(gla_in, gates)
        diff_o = _diff_call(dqk, dvt, diff_lambda[layer], jnp.full((1, 1), lam_init, F32),
                            (jnp.tile(diff_norm_g[layer], DIFF_HEADS) * (1.0 - lam_init)).reshape(1, DIFF_W))
        swa_o = _swa_call(sq, sk, svt,
                          jnp.broadcast_to(swa_sink[layer][:, None] * LOG2_E, (SWA_HEADS, LANES)), n_ctx)
        gng = jnp.tile(gla_norm_g[layer], GLA_HEADS).reshape(1, GLA_W)
        w_o, w_g, w_u, w_d = (w_out[layer].astype(wdt), w_ffn_gate[layer].astype(wdt),
                              w_ffn_up[layer].astype(wdt), w_ffn_down[layer].astype(wdt))
        if last:
            h = _post_latent_call(o_f, o_b, gla_in, diff_o, swa_o, h, mods, layer, w_o, w_g, w_u, w_d,
                                  gng, ln_g[layer], ln_b[layer], alpha, n_ctx)
        else:
            h = _post_call(o_f, o_b, gla_in, diff_o, swa_o, h, mods, layer, ctx_row, w_o, w_g, w_u, w_d,
                           gng, ln_g[layer], ln_b[layer], alpha)
    return h
```

```python
import functools
import math

import jax
import jax.numpy as jnp
from jax import lax
from jax.experimental import pallas as pl
from jax.experimental.pallas import tpu as pltpu

F32 = jnp.float32
MXU_DTYPE = jnp.bfloat16

GRID_W = 64
GLA_HEADS, GLA_DK, GLA_DV = 4, 64, 64
GLA_GATE_RANK = 16
GLA_GATE_NORM = 16.0
GLA_CHUNK = 64
GLA_SUB = 16
DIFF_HEADS, DIFF_DH = 4, 32
DIFF_DV = 2 * DIFF_DH
SWA_HEADS, SWA_KV_HEADS, SWA_DH = 8, 2, 64
WINDOW = 128
ROPE_BASE = 10000.0
LN_EPS = 1e-6

GLA_W = GLA_HEADS * GLA_DK
DIFF_W = DIFF_HEADS * 2 * DIFF_DH
SWA_Q_W = SWA_HEADS * SWA_DH
SWA_KV_W = SWA_KV_HEADS * SWA_DH
LANES = 128
MXU_TILE = 256
TOK = 256
POST_TOK = 2 * TOK
SWA_TQ = 128
DIFF_VROWS = DIFF_DV + 16
DIFF_KEY_CUT = (13, 17)
INPROJ_LAG = 4
SWA_VROWS = SWA_DH + 16
LOG2_E = math.log2(math.e)
VMEM_LIMIT = 52 * 1024 * 1024
POST_VMEM_LIMIT = 58 * 1024 * 1024


def _mm(a, b):
    return jnp.dot(a.astype(MXU_DTYPE), b.astype(MXU_DTYPE), preferred_element_type=F32)


def _mm_nt(a, b):
    return lax.dot_general(a.astype(MXU_DTYPE), b.astype(MXU_DTYPE),
                           (((1,), (1,)), ((), ())), preferred_element_type=F32)


def _mm_tn(a, b):
    return lax.dot_general(a.astype(MXU_DTYPE), b.astype(MXU_DTYPE),
                           (((0,), (0,)), ((), ())), preferred_element_type=F32)


def _split_mm(a, b_exact):
    hi = a.astype(jnp.bfloat16)
    r1 = a - hi.astype(F32)
    mid = r1.astype(jnp.bfloat16)
    lo = (r1 - mid.astype(F32)).astype(jnp.bfloat16)
    b = b_exact.astype(jnp.bfloat16)
    return (jnp.dot(hi, b, preferred_element_type=F32)
            + jnp.dot(mid, b, preferred_element_type=F32)
            + jnp.dot(lo, b, preferred_element_type=F32))


def _split_mm_t(b_exact, a):
    hi = a.astype(jnp.bfloat16)
    r1 = a - hi.astype(F32)
    mid = r1.astype(jnp.bfloat16)
    lo = (r1 - mid.astype(F32)).astype(jnp.bfloat16)
    b = b_exact.astype(jnp.bfloat16)
    return (jnp.dot(b, hi, preferred_element_type=F32)
            + jnp.dot(b, mid, preferred_element_type=F32)
            + jnp.dot(b, lo, preferred_element_type=F32))


def _ln_plain(x):
    mu = jnp.mean(x, axis=-1, keepdims=True)
    xc = x - mu
    var = jnp.mean(xc * xc, axis=-1, keepdims=True)
    return xc * lax.rsqrt(var + LN_EPS)


def _silu(x):
    return x / (1.0 + jnp.exp(-x))


def _group_mean_matrix(width, group):
    r = lax.broadcasted_iota(jnp.int32, (width, width), 0) // group
    c = lax.broadcasted_iota(jnp.int32, (width, width), 1) // group
    return jnp.where(r == c, 1.0 / group, 0.0).astype(F32)


def _mods_kernel(a_ref, w_ref, b_ref, o_ref):
    a = a_ref[...]
    o_ref[...] = jnp.dot(_silu(a), w_ref[...], preferred_element_type=F32,
                         precision=lax.Precision.HIGHEST) + b_ref[...]


def _mods_call(cond, w_ada, b_ada):
    depth, d, width = w_ada.shape
    rows = cond.shape[0]
    bn = 1536
    return pl.pallas_call(
        _mods_kernel,
        grid=(depth, width // bn),
        in_specs=[
            pl.BlockSpec((rows, d), lambda l, n: (0, 0)),
            pl.BlockSpec((None, d, bn), lambda l, n: (l, 0, n)),
            pl.BlockSpec((None, 1, bn), lambda l, n: (l, 0, n)),
        ],
        out_specs=pl.BlockSpec((None, rows, bn), lambda l, n: (l, 0, n)),
        out_shape=jax.ShapeDtypeStruct((depth, rows, width), F32),
        compiler_params=pltpu.CompilerParams(vmem_limit_bytes=VMEM_LIMIT),
        name="adaln_mods",
    )(cond, w_ada, b_ada.reshape(depth, 1, width))


C_GQ, C_GK, C_GV, C_GO = 0, 256, 512, 768
C_DQ, C_DK, C_DV = 1024, 1280, 1536
C_SQ, C_SK, C_SV = 1792, 2304, 2432
C_Z = 2560
IN_PAD_W = 2688


def _rope(x, cos, sin_signed, half):
    outs = []
    lane = lax.broadcasted_iota(jnp.int32, (1, LANES), 1)
    first = (lane % (2 * half)) < half
    for s in range(x.shape[1] // LANES):
        xs = x[:, s * LANES:(s + 1) * LANES]
        up = pltpu.roll(xs, LANES - half, axis=1)
        dn = pltpu.roll(xs, half, axis=1)
        outs.append(xs * cos + jnp.where(first, up, dn) * sin_signed)
    return outs[0] if len(outs) == 1 else jnp.concatenate(outs, axis=1)


def _inproj_kernel(*refs, split_input):
    j = pl.program_id(1)
    if split_input:
        x_ref, ctx_ref = refs[0:2]
        refs = refs[2:]
        hout_ref = refs[-1]
        refs = refs[:-1]
        sps = x_ref.shape[0]
        for n in range(sps):
            hout_ref[n] = jnp.where(j == 0, ctx_ref[n], x_ref[n])
        h_ref = hout_ref
    else:
        h_ref = refs[0]
        refs = refs[1:]
        sps = h_ref.shape[0]
    (mod_ref, w_ref, wg_ref, bg_ref, cd_ref, sd_ref, cs_ref, ss_ref,
     gla_ref, gate_ref, dqk_ref, dvt_ref, sq_ref, sk_ref, svt_ref) = refs
    cd, sd, cs, ss = cd_ref[...], sd_ref[...], cs_ref[...], ss_ref[...]
    lane = lax.broadcasted_iota(jnp.int32, (1, LANES), 1)
    low = lane < SWA_DH
    ones_d = jnp.where(lax.broadcasted_iota(jnp.int32, (DIFF_VROWS - DIFF_DV, TOK), 0) == 0, 1.0, 0.0)
    ones_s = jnp.where(lax.broadcasted_iota(jnp.int32, (SWA_VROWS - SWA_DH, SWA_TQ), 0) == 0, 1.0, 0.0)
    st = [dict() for _ in range(sps)]

    def norm(n):
        mod = mod_ref[jnp.where(j == 0, 0, n)]
        st[n]["u"] = (_ln_plain(h_ref[n]) * (1.0 + mod[1:2, :]) + mod[0:1, :]).astype(MXU_DTYPE)

    def project(n, name, c0, c1):
        st[n][name] = jnp.dot(st[n]["u"], w_ref[:, c0:c1], preferred_element_type=F32)

    def gate_out(n):
        gpre = _mm(st[n].pop("z"), wg_ref[...]) + bg_ref[...]
        logsig = jnp.minimum(gpre, 0.0) - jnp.log(1.0 + jnp.exp(-jnp.abs(gpre)))
        gate_ref[n] = logsig * (1.0 / GLA_GATE_NORM)

    def diff_out(n):
        pd = st[n].pop("d")
        dq = _rope(pd[:, 0:DIFF_W], cd, sd, DIFF_DH // 2) * (DIFF_DH ** -0.5 * LOG2_E)
        dk = _rope(pd[:, DIFF_W:2 * DIFF_W], cd, sd, DIFF_DH // 2)
        dqk_ref[n, :, 0:DIFF_W] = dq.astype(dqk_ref.dtype)
        dqk_ref[n, :, DIFF_W:2 * DIFF_W] = dk.astype(dqk_ref.dtype)
        vt = pd[:, 2 * DIFF_W:3 * DIFF_W].T
        for hh in range(DIFF_HEADS):
            dvt_ref[n, hh, 0:DIFF_DV, :] = vt[hh * DIFF_DV:(hh + 1) * DIFF_DV, :].astype(dvt_ref.dtype)
            dvt_ref[n, hh, DIFF_DV:DIFF_VROWS, :] = ones_d.astype(dvt_ref.dtype)

    def swa_out(n):
        ps = st[n].pop("s")
        sq = _rope(ps[:, 0:SWA_Q_W], cs, ss, SWA_DH // 2) * (SWA_DH ** -0.5 * LOG2_E)
        sq_ref[n] = sq.astype(sq_ref.dtype)
        sk = _rope(ps[:, SWA_Q_W:SWA_Q_W + SWA_KV_W], cs, ss, SWA_DH // 2)
        sw = pltpu.roll(sk, SWA_DH, axis=1)
        sk_ref[n, :, 0:LANES] = jnp.where(low, sk, sw).astype(sk_ref.dtype)
        sk_ref[n, :, LANES:2 * LANES] = jnp.where(low, sw, sk).astype(sk_ref.dtype)
        svt = ps[:, SWA_Q_W + SWA_KV_W:SWA_Q_W + 2 * SWA_KV_W].T
        for grp in range(SWA_KV_HEADS):
            for i in range(TOK // SWA_TQ):
                svt_ref[n, grp, i, 0:SWA_DH, :] = svt[grp * SWA_DH:(grp + 1) * SWA_DH,
                                                      i * SWA_TQ:(i + 1) * SWA_TQ].astype(svt_ref.dtype)
                svt_ref[n, grp, i, SWA_DH:SWA_VROWS, :] = ones_s.astype(svt_ref.dtype)

    def gla_out(n):
        pg = st[n].pop("g")
        gla_ref[n, :, 0:GLA_W] = pg[:, 0:GLA_W] * (GLA_DK ** -0.5)
        gla_ref[n, :, GLA_W:4 * GLA_W] = pg[:, GLA_W:4 * GLA_W]

    def steps(n):
        return [
            lambda: norm(n),
            lambda: project(n, "z", C_Z, C_Z + LANES),
            lambda: project(n, "d", C_DQ, C_DV + DIFF_W),
            lambda: gate_out(n),
            lambda: project(n, "s", C_SQ, C_SV + SWA_KV_W),
            lambda: diff_out(n),
            lambda: project(n, "g", C_GQ, C_GO + GLA_W),
            lambda: swa_out(n),
            lambda: gla_out(n),
        ]

    lag = INPROJ_LAG
    plans = [steps(n) for n in range(sps)]
    for tick in range(len(plans[0]) + lag * (sps - 1)):
        for n in range(sps):
            i = tick - lag * n
            if 0 <= i < len(plans[n]):
                plans[n][i]()


def _inproj_call(h, mods, layer, ctx_row, w_in_p, wg, bg, tabs):
    split_input = isinstance(h, tuple)
    if split_input:
        x, ctx = h
        bsz, n_lat, d = x.shape
        assert ctx.shape[1] == TOK
        s = n_lat + TOK
    else:
        bsz, s, d = h.shape
    nt = s // TOK
    act = MXU_DTYPE
    sps = 2 if (bsz % 2 == 0 and ctx_row % 2 == 0) else 1
    tab_spec = pl.BlockSpec((TOK, LANES), lambda b, j: (j, 0))

    def mod_index(b, j):
        return (layer, jnp.where(j == 0, ctx_row // sps, b), 0, 0)

    if split_input:
        stream_in = [x, ctx]
        stream_specs = [pl.BlockSpec((sps, TOK, d), lambda b, j: (b, jnp.maximum(j - 1, 0), 0)),
                        pl.BlockSpec((sps, TOK, d), lambda b, j: (b, 0, 0))]
        extra_specs = [pl.BlockSpec((sps, TOK, d), lambda b, j: (b, j, 0))]
        extra_shapes = [jax.ShapeDtypeStruct((bsz, s, d), F32)]
    else:
        stream_in = [h]
        stream_specs = [pl.BlockSpec((sps, TOK, d), lambda b, j: (b, j, 0))]
        extra_specs, extra_shapes = [], []

    return pl.pallas_call(
        functools.partial(_inproj_kernel, split_input=split_input),
        grid=(bsz // sps, nt),
        in_specs=stream_specs + [
            pl.BlockSpec((None, sps, 6, d), mod_index),
            pl.BlockSpec((d, IN_PAD_W), lambda b, j: (0, 0)),
            pl.BlockSpec((LANES, 2 * GLA_W), lambda b, j: (0, 0)),
            pl.BlockSpec((1, 2 * GLA_W), lambda b, j: (0, 0)),
            tab_spec, tab_spec, tab_spec, tab_spec,
        ],
        out_specs=[
            pl.BlockSpec((sps, TOK, 4 * GLA_W), lambda b, j: (b, j, 0)),
            pl.BlockSpec((sps, TOK, 2 * GLA_W), lambda b, j: (b, j, 0)),
            pl.BlockSpec((sps, TOK, 2 * DIFF_W), lambda b, j: (b, j, 0)),
            pl.BlockSpec((sps, DIFF_HEADS, DIFF_VROWS, TOK), lambda b, j: (b, 0, 0, j)),
            pl.BlockSpec((sps, TOK, SWA_Q_W), lambda b, j: (b, j, 0)),
            pl.BlockSpec((sps, TOK, 2 * LANES), lambda b, j: (b, j, 0)),
            pl.BlockSpec((sps, SWA_KV_HEADS, TOK // SWA_TQ, SWA_VROWS, SWA_TQ), lambda b, j: (b, 0, j, 0, 0)),
        ] + extra_specs,
        out_shape=[
            jax.ShapeDtypeStruct((bsz, s, 4 * GLA_W), F32),
            jax.ShapeDtypeStruct((bsz, s, 2 * GLA_W), F32),
            jax.ShapeDtypeStruct((bsz, s, 2 * DIFF_W), act),
            jax.ShapeDtypeStruct((bsz, DIFF_HEADS, DIFF_VROWS, s), act),
            jax.ShapeDtypeStruct((bsz, s, SWA_Q_W), act),
            jax.ShapeDtypeStruct((bsz, s, 2 * LANES), act),
            jax.ShapeDtypeStruct((bsz, SWA_KV_HEADS, s // SWA_TQ, SWA_VROWS, SWA_TQ), act),
        ] + extra_shapes,
        compiler_params=pltpu.CompilerParams(
            dimension_semantics=("arbitrary", "arbitrary"), vmem_limit_bytes=VMEM_LIMIT),
        name="in_projection",
    )(*stream_in, mods, w_in_p, wg, bg, *tabs)


def _gla_direction(gla_ref, gate_ref, gate_col, o_ref, st_ref, reverse, consts):
    tri, head_lane, blockdiag, sub_masks = consts
    c_, sub = GLA_CHUNK, GLA_SUB
    nsub = c_ // sub
    nchunk = TOK // c_
    g_all = gate_ref[:, gate_col:gate_col + GLA_W]
    b_all = _split_mm_t(tri, g_all)
    order = range(nchunk - 1, -1, -1) if reverse else range(nchunk)
    zero = jnp.zeros((), F32)
    chunks = []
    for c in order:
        r0 = c * c_
        q = gla_ref[r0:r0 + c_, 0:GLA_W]
        k = gla_ref[r0:r0 + c_, GLA_W:2 * GLA_W]
        v = gla_ref[r0:r0 + c_, 2 * GLA_W:3 * GLA_W]
        b = b_all[r0:r0 + c_, :]
        b_end = b[0:1, :] if reverse else b[c_ - 1:c_, :]
        atts = []
        for i in range(nsub):
            t0, t1 = i * sub, (i + 1) * sub
            if reverse:
                ref_b = b[t1:t1 + 1, :] if i < nsub - 1 else jnp.zeros((1, GLA_W), F32)
                k0, k1 = t0, c_
            else:
                ref_b = b[t0 - 1:t0, :] if i > 0 else jnp.zeros((1, GLA_W), F32)
                k0, k1 = 0, t1
            qd = q[t0:t1, :] * jnp.exp(b[t0:t1, :] - ref_b)
            lhs = jnp.concatenate([jnp.where(head_lane[hh], qd, zero) for hh in range(GLA_HEADS)], axis=0)
            kk = k[k0:k1, :] * jnp.exp(ref_b - b[k0:k1, :])
            atts.append((_mm_nt(lhs, kk), k0, k1))
        upd = _mm_tn(v, k * jnp.exp(b_end - b))
        chunks.append(dict(r0=r0, v=v, atts=atts, upd=upd, qs=q * jnp.exp(b), decay=jnp.exp(b_end)))
    for ch in chunks:
        o_parts = []
        for i, (att, k0, k1) in enumerate(ch["atts"]):
            res = _mm(jnp.where(sub_masks[(reverse, i)], att, zero), ch["v"][k0:k1, :])
            o_i = res[(GLA_HEADS - 1) * sub:GLA_HEADS * sub, :]
            for hh in range(GLA_HEADS - 2, -1, -1):
                o_i = jnp.where(head_lane[hh], res[hh * sub:(hh + 1) * sub, :], o_i)
            o_parts.append(o_i)
        ch["o_intra"] = jnp.concatenate(o_parts, axis=0)
    st = st_ref[...]
    for ch in chunks:
        o_ref[ch["r0"]:ch["r0"] + c_, :] = ch["o_intra"] + _mm_nt(ch["qs"], st)
        st = st * ch["decay"] + jnp.where(blockdiag, ch["upd"], zero)
    st_ref[...] = st


def _gla_consts():
    c_, sub = GLA_CHUNK, GLA_SUB
    nsub = c_ // sub
    r = lax.broadcasted_iota(jnp.int32, (TOK, TOK), 0)
    cc = lax.broadcasted_iota(jnp.int32, (TOK, TOK), 1)
    same_chunk = (r // c_) == (cc // c_)
    tri_f = jnp.where(same_chunk & (cc <= r), 1.0, 0.0).astype(F32)
    tri_b = jnp.where(same_chunk & (cc >= r), 1.0, 0.0).astype(F32)
    lane = lax.broadcasted_iota(jnp.int32, (1, GLA_W), 1)
    head_lane = [(lane // GLA_DK) == hh for hh in range(GLA_HEADS)]
    blockdiag = (r // GLA_DV) == (cc // GLA_DK)
    sub_masks = {}
    for reverse in (False, True):
        for i in range(nsub):
            nk = (nsub - i) * sub if reverse else (i + 1) * sub
            rr = lax.broadcasted_iota(jnp.int32, (GLA_HEADS * sub, nk), 0) % sub
            kc = lax.broadcasted_iota(jnp.int32, (GLA_HEADS * sub, nk), 1)
            if reverse:
                sub_masks[(reverse, i)] = kc >= rr
            else:
                sub_masks[(reverse, i)] = (kc - (nk - sub)) <= rr
    return tri_f, tri_b, head_lane, blockdiag, sub_masks


def _gla_kernel(gla_f_ref, gate_f_ref, gla_b_ref, gate_b_ref, of_ref, ob_ref, stf_ref, stb_ref):
    @pl.when(pl.program_id(1) == 0)
    def _():
        stf_ref[...] = jnp.zeros_like(stf_ref)
        stb_ref[...] = jnp.zeros_like(stb_ref)

    tri_f, tri_b, head_lane, blockdiag, sub_masks = _gla_consts()
    for smp in range(gla_f_ref.shape[0]):
        _gla_direction(gla_f_ref.at[smp], gate_f_ref.at[smp], 0, of_ref.at[smp], stf_ref.at[smp], False,
                       (tri_f, head_lane, blockdiag, sub_masks))
        _gla_direction(gla_b_ref.at[smp], gate_b_ref.at[smp], GLA_W, ob_ref.at[smp], stb_ref.at[smp], True,
                       (tri_b, head_lane, blockdiag, sub_masks))


def _gla_call(gla_in, gates):
    bsz, s, _ = gla_in.shape
    nt = s // TOK
    sps = 2 if bsz % 2 == 0 else 1

    def fwd(b, j):
        return (b, j, 0)

    def bwd(b, j):
        return (b, jnp.where(j == 0, 0, nt - j), 0)

    return pl.pallas_call(
        _gla_kernel,
        grid=(bsz // sps, nt),
        in_specs=[
            pl.BlockSpec((sps, TOK, 4 * GLA_W), fwd),
            pl.BlockSpec((sps, TOK, 2 * GLA_W), fwd),
            pl.BlockSpec((sps, TOK, 4 * GLA_W), bwd),
            pl.BlockSpec((sps, TOK, 2 * GLA_W), bwd),
        ],
        out_specs=[
            pl.BlockSpec((sps, TOK, GLA_W), fwd),
            pl.BlockSpec((sps, TOK, GLA_W), bwd),
        ],
        out_shape=[jax.ShapeDtypeStruct((bsz, s, GLA_W), F32)] * 2,
        scratch_shapes=[pltpu.VMEM((sps, GLA_W, GLA_W), F32), pltpu.VMEM((sps, GLA_W, GLA_W), F32)],
        compiler_params=pltpu.CompilerParams(
            dimension_semantics=("arbitrary", "arbitrary"), vmem_limit_bytes=VMEM_LIMIT),
        name="gla_scan",
    )(gla_in, gates, gla_in, gates)


def _diff_kernel(q_ref, k_ref, vt_ref, lam_ref, lam0_ref, g_ref, o_ref, *, n_ctx):
    lam_p = lam_ref[...]
    lam = (jnp.exp(jnp.sum(lam_p[0:1, :] * lam_p[1:2, :], axis=-1, keepdims=True))
           - jnp.exp(jnp.sum(lam_p[2:3, :] * lam_p[3:4, :], axis=-1, keepdims=True))
           + lam0_ref[...])
    lane = lax.broadcasted_iota(jnp.int32, (1, DIFF_W), 1)

    def attend(edges):
        parts = len(edges) - 1
        per_smp = 2 * DIFF_HEADS
        n_str = q_ref.shape[0] * per_smp

        def scores(idx):
            smp, hc = divmod(idx, per_smp)
            q = q_ref[smp]
            lo = hc * DIFF_DH
            qm = jnp.where((lane >= lo) & (lane < lo + DIFF_DH), q, jnp.zeros_like(q))
            return [_mm_nt(k_ref[smp, edges[i]:edges[i + 1], :], qm).astype(MXU_DTYPE)
                    for i in range(parts)]

        def probs(sts):
            m = _col_max(sts[0])
            for st in sts[1:]:
                m = jnp.maximum(m, _col_max(st))
            return [jnp.exp2(st - m) for st in sts]

        def values(idx, ps):
            smp, hc = divmod(idx, per_smp)
            acc = None
            for i, p in enumerate(ps):
                pv = _mm(vt_ref[smp, hc // 2, :, edges[i]:edges[i + 1]], p)
                acc = pv if acc is None else acc + pv
            return acc[0:DIFF_DV, :] / acc[DIFF_DV:DIFF_DV + 1, :]

        def finish(smp, comps):
            heads = []
            for hh in range(DIFF_HEADS):
                o_h = comps[2 * hh] - lam * comps[2 * hh + 1]
                ms = jnp.mean(o_h * o_h, axis=0, keepdims=True)
                heads.append(o_h * lax.rsqrt(ms + LN_EPS))
            out = jnp.concatenate(heads, axis=0).T
            o_ref[smp] = (out * g_ref[...]).astype(o_ref.dtype)

        sts = {0: scores(0), 1: scores(1)}
        ps = {0: probs(sts.pop(0))}
        comps = []
        for idx in range(n_str):
            if idx + 2 < n_str:
                sts[idx + 2] = scores(idx + 2)
            if idx + 1 < n_str:
                ps[idx + 1] = probs(sts.pop(idx + 1))
            comps.append(values(idx, ps.pop(idx)))
            if len(comps) == per_smp:
                finish(idx // per_smp, comps)
                comps = []

    j = pl.program_id(1)

    @pl.when(j == 0)
    def _():
        attend((0, n_ctx))

    @pl.when(j > 0)
    def _():
        n_keys = k_ref.shape[1]
        cut = LANES * ((n_keys // LANES) * DIFF_KEY_CUT[0] // DIFF_KEY_CUT[1])
        attend((0, cut, n_keys) if 0 < cut < n_keys else (0, n_keys))


def _col_max(x):
    r = x.shape[0]
    slab = 16
    while r % (2 * slab) == 0 and r // slab > 32:
        slab *= 2
    acc = x[0:slab, :]
    for i in range(1, r // slab):
        acc = jnp.maximum(acc, x[i * slab:(i + 1) * slab, :])
    return jnp.max(acc, axis=0, keepdims=True)


def _diff_call(dqk, dvt, lam_p, lam0, g_eff):
    bsz, s, _ = dqk.shape
    nt = s // TOK
    sps = 2 if bsz % 2 == 0 else 1
    return pl.pallas_call(
        functools.partial(_diff_kernel, n_ctx=TOK),
        grid=(bsz // sps, nt),
        in_specs=[
            pl.BlockSpec((sps, TOK, DIFF_W), lambda b, j: (b, j, 0)),
            pl.BlockSpec((sps, s, DIFF_W), lambda b, j: (b, 0, 1)),
            pl.BlockSpec((sps, DIFF_HEADS, DIFF_VROWS, s), lambda b, j: (b, 0, 0, 0)),
            pl.BlockSpec((4, DIFF_DH), lambda b, j: (0, 0)),
            pl.BlockSpec((1, 1), lambda b, j: (0, 0)),
            pl.BlockSpec((1, DIFF_W), lambda b, j: (0, 0)),
        ],
        out_specs=pl.BlockSpec((sps, TOK, DIFF_W), lambda b, j: (b, j, 0)),
        out_shape=jax.ShapeDtypeStruct((bsz, s, DIFF_W), MXU_DTYPE),
        compiler_params=pltpu.CompilerParams(
            dimension_semantics=("arbitrary", "arbitrary"), vmem_limit_bytes=VMEM_LIMIT),
        name="diff_attention",
    )(dqk, dqk, dvt, lam_p, lam0, g_eff)


def _swa_kernel(q_ref, k_ref, vt_ref, sink_ref, o_ref, *, n_ctx, n_lat):
    j = pl.program_id(1)
    tq = SWA_TQ
    n_win = 3
    n_ctx_t = n_ctx // tq
    s_rows = n_ctx + n_lat
    lane = lax.broadcasted_iota(jnp.int32, (1, LANES), 1)
    low = lane < SWA_DH
    q_per_kv = SWA_HEADS // SWA_KV_HEADS
    chains = [(smp, sub, grp) for smp in range(q_ref.shape[0]) for sub in range(TOK // tq)
              for grp in range(SWA_KV_HEADS)]

    def window(sub):
        p0 = j * TOK + sub * tq - n_ctx
        start = pl.multiple_of(jnp.clip(n_ctx + p0 - tq, 0, s_rows - n_win * tq), tq)
        return p0, start

    def scores(idx):
        smp, sub, grp = chains[idx]
        _, start = window(sub)
        blocks = []
        for r in range(q_per_kv):
            hh = grp * q_per_kv + r
            qp = q_ref[smp, sub * tq:(sub + 1) * tq, (hh // 2) * LANES:(hh // 2 + 1) * LANES]
            blocks.append(jnp.where(low if hh % 2 == 0 else ~low, qp, jnp.zeros_like(qp)))
        lhs = jnp.concatenate(blocks, axis=0)
        st_c = _mm_nt(k_ref[smp, 0:n_ctx, grp * LANES:(grp + 1) * LANES], lhs)
        st_l = _mm_nt(k_ref[smp, pl.ds(start, n_win * tq), grp * LANES:(grp + 1) * LANES], lhs)
        return st_c, st_l

    def probs(idx, st):
        _, sub, grp = chains[idx]
        st_c, st_l = st
        p0, start = window(sub)
        pos_q = p0 + lax.broadcasted_iota(jnp.int32, (1, tq), 1)
        pos_k = start - n_ctx + lax.broadcasted_iota(jnp.int32, (n_win * tq, 1), 0)
        valid = (jnp.abs(pos_k - pos_q) <= WINDOW) & (pos_k >= 0) & (p0 >= 0)
        bias = jnp.where(valid, 0.0, -jnp.inf).astype(F32)
        st_l = st_l + jnp.concatenate([bias] * q_per_kv, axis=1)
        st_c = st_c.astype(MXU_DTYPE)
        st_l = st_l.astype(MXU_DTYPE)
        snk = jnp.concatenate([sink_ref[grp * q_per_kv + r:grp * q_per_kv + r + 1, :]
                               for r in range(q_per_kv)], axis=1)
        m = jnp.maximum(jnp.maximum(_col_max(st_c), _col_max(st_l)).astype(F32), snk)
        mb = m.astype(MXU_DTYPE)
        return jnp.exp2(st_c - mb), jnp.exp2(st_l - mb), jnp.exp2(snk - mb.astype(F32))

    def values(idx, pr):
        smp, sub, grp = chains[idx]
        p_c, p_l, p_snk = pr
        _, start = window(sub)
        t0 = start // tq
        acc = None
        for i in range(n_ctx_t):
            pv = _mm(vt_ref[smp, grp, i], p_c[i * tq:(i + 1) * tq, :])
            acc = pv if acc is None else acc + pv
        for i in range(n_win):
            acc = acc + _mm(vt_ref[smp, grp, t0 + i], p_l[i * tq:(i + 1) * tq, :])
        o = acc[0:SWA_DH, :] / (acc[SWA_DH:SWA_DH + 1, :] + p_snk)
        for hp in range(q_per_kv // 2):
            pair = grp * (q_per_kv // 2) + hp
            both = jnp.concatenate([o[:, (2 * hp) * tq:(2 * hp + 1) * tq],
                                    o[:, (2 * hp + 1) * tq:(2 * hp + 2) * tq]], axis=0)
            o_ref[smp, sub * tq:(sub + 1) * tq, pair * LANES:(pair + 1) * LANES] = both.T.astype(o_ref.dtype)

    n_ch = len(chains)
    sts = {0: scores(0), 1: scores(1)}
    prs = {0: probs(0, sts.pop(0))}
    for idx in range(n_ch):
        if idx + 2 < n_ch:
            sts[idx + 2] = scores(idx + 2)
        if idx + 1 < n_ch:
            prs[idx + 1] = probs(idx + 1, sts.pop(idx + 1))
        values(idx, prs.pop(idx))


def _swa_call(sq, sk, svt, sink_tile, n_ctx):
    bsz, s, _ = sq.shape
    sps = 2 if bsz % 2 == 0 else 1
    return pl.pallas_call(
        functools.partial(_swa_kernel, n_ctx=n_ctx, n_lat=s - n_ctx),
        grid=(bsz // sps, s // TOK),
        in_specs=[
            pl.BlockSpec((sps, TOK, SWA_Q_W), lambda b, j: (b, j, 0)),
            pl.BlockSpec((sps, s, 2 * LANES), lambda b, j: (b, 0, 0)),
            pl.BlockSpec((sps, SWA_KV_HEADS, s // SWA_TQ, SWA_VROWS, SWA_TQ), lambda b, j: (b, 0, 0, 0, 0)),
            pl.BlockSpec((SWA_HEADS, LANES), lambda b, j: (0, 0)),
        ],
        out_specs=pl.BlockSpec((sps, TOK, SWA_Q_W), lambda b, j: (b, j, 0)),
        out_shape=jax.ShapeDtypeStruct((bsz, s, SWA_Q_W), MXU_DTYPE),
        compiler_params=pltpu.CompilerParams(
            dimension_semantics=("arbitrary", "arbitrary"), vmem_limit_bytes=VMEM_LIMIT),
        name="windowed_gqa",
    )(sq, sk, svt, sink_tile)


def _ffn_chunk_bounds(hid):
    cut = -(-(hid // MXU_TILE) // 2) * MXU_TILE
    return ((0, cut), (cut, hid))


def _post_kernel(*refs, alpha, tiles_per_sample, ctx_row, paired):
    halves = POST_TOK // TOK
    n_act = 6
    if paired:
        acts = [refs[halves * i:halves * (i + 1)] for i in range(n_act)]
        load = lambda i, x: acts[i][x][...]
    else:
        acts = refs[:n_act]
        load = lambda i, x: acts[i][x * TOK:(x + 1) * TOK, :]
    (mod_ref, wo_ref, wg_ref, wu_ref, wd_ref, gng_ref, lng_ref, lnb_ref, o_ref) = refs[len(refs) - 9:]
    hid = wg_ref.shape[1]
    bounds = _ffn_chunk_bounds(hid)
    gm = _group_mean_matrix(GLA_W, GLA_DV)
    st = [dict() for _ in range(halves)]

    def out_proj(x):
        if paired:
            row = pl.program_id(0)
        else:
            g = pl.program_id(0) * halves + x
            row = jnp.where(g % tiles_per_sample == 0, ctx_row, g // tiles_per_sample)
        st[x]["mod"] = mod_ref[row]
        o = load(0, x) + load(1, x)
        ms = _split_mm(o * o, gm)
        gla = o * lax.rsqrt(ms + LN_EPS) * gng_ref[...] * _silu(load(2, x))
        st[x]["y"] = (_mm(gla, wo_ref[0:GLA_W, :])
                      + _mm(load(3, x), wo_ref[GLA_W:GLA_W + DIFF_W, :])
                      + _mm(load(4, x), wo_ref[GLA_W + DIFF_W:, :]))

    def mid_norm(x):
        mod = st[x]["mod"]
        z = alpha * load(5, x) + mod[2:3, :] * st[x].pop("y")
        hm = _ln_plain(z) * lng_ref[0:1, :] + lnb_ref[0:1, :]
        st[x]["hm"] = hm
        st[x]["u"] = (_ln_plain(hm) * (1.0 + mod[4:5, :]) + mod[3:4, :]).astype(MXU_DTYPE)

    def gate_up(x, c):
        u = st[x]["u"]
        c0, c1 = bounds[c]
        st[x]["g", c] = jnp.dot(u, wg_ref[:, c0:c1], preferred_element_type=F32)
        st[x]["p", c] = jnp.dot(u, wu_ref[:, c0:c1], preferred_element_type=F32)

    def act(x, c):
        st[x]["a", c] = (_silu(st[x].pop(("g", c))) * st[x].pop(("p", c))).astype(MXU_DTYPE)

    def down(x, c):
        c0, c1 = bounds[c]
        f = jnp.dot(st[x].pop(("a", c)), wd_ref[c0:c1, :], preferred_element_type=F32)
        st[x]["f"] = f if c == 0 else st[x]["f"] + f

    def final(x):
        rows = slice(x * TOK, (x + 1) * TOK)
        z = alpha * st[x]["hm"] + st[x]["mod"][5:6, :] * st[x]["f"]
        o_ref[rows, :] = _ln_plain(z) * lng_ref[1:2, :] + lnb_ref[1:2, :]

    a_, b_ = 0, 1
    out_proj(a_)
    out_proj(b_)
    mid_norm(a_)
    gate_up(a_, 0)
    mid_norm(b_)
    gate_up(a_, 1)
    act(a_, 0)
    gate_up(b_, 0)
    act(a_, 1)
    down(a_, 0)
    gate_up(b_, 1)
    act(b_, 0)
    down(a_, 1)
    act(b_, 1)
    down(b_, 0)
    final(a_)
    down(b_, 1)
    final(b_)


def _post_call(o_f, o_b, gla_in, diff_o, swa_o, h, mods, layer, ctx_row, w_out, w_g, w_u, w_d, gng, lng, lnb, alpha):
    bsz, s, d = h.shape
    rows = bsz * s
    assert rows % POST_TOK == 0
    hid = w_g.shape[1]
    flat = lambda a: a.reshape(rows, a.shape[-1])
    tile = lambda w: pl.BlockSpec((POST_TOK, w), lambda t: (t, 0))
    single = pl.Buffered(1)
    const = lambda shape: pl.BlockSpec(shape, lambda t: (0,) * len(shape), pipeline_mode=single)
    out = pl.pallas_call(
        functools.partial(_post_kernel, alpha=alpha, tiles_per_sample=s // TOK, ctx_row=ctx_row, paired=False),
        grid=(rows // POST_TOK,),
        in_specs=[
            tile(GLA_W), tile(GLA_W),
            pl.BlockSpec((POST_TOK, GLA_W), lambda t: (t, 3)),
            tile(DIFF_W), tile(SWA_Q_W), tile(d),
            pl.BlockSpec((None,) + mods.shape[1:], lambda t: (layer, 0, 0, 0)),
            const((d, d)), const((d, hid)), const((d, hid)), const((hid, d)),
            pl.BlockSpec((1, GLA_W), lambda t: (0, 0)),
            pl.BlockSpec((2, d), lambda t: (0, 0)), pl.BlockSpec((2, d), lambda t: (0, 0)),
        ],
        out_specs=tile(d),
        out_shape=jax.ShapeDtypeStruct((rows, d), F32),
        compiler_params=pltpu.CompilerParams(
            dimension_semantics=("arbitrary",), vmem_limit_bytes=POST_VMEM_LIMIT),
        name="post_attention",
    )(flat(o_f), flat(o_b), flat(gla_in), flat(diff_o), flat(swa_o), flat(h), mods,
      w_out, w_g, w_u, w_d, gng, lng, lnb)
    return out.reshape(bsz, s, d)


def _post_latent_call(o_f, o_b, gla_in, diff_o, swa_o, h, mods, layer, w_out, w_g, w_u, w_d, gng, lng, lnb,
                      alpha, n_ctx):
    bsz, s, d = h.shape
    n_lat = s - n_ctx
    assert n_ctx == TOK and n_lat % POST_TOK == 0
    hid = w_g.shape[1]
    halves = POST_TOK // TOK
    first = n_ctx // TOK

    def pair(w, col=0):
        return [pl.BlockSpec((None, TOK, w), functools.partial(lambda b, i, x: (b, first + halves * i + x, col), x=x))
                for x in range(halves)]

    single = pl.Buffered(1)
    const = lambda shape: pl.BlockSpec(shape, lambda b, i: (0,) * len(shape), pipeline_mode=single)
    acts = [o_f, o_b, gla_in, diff_o, swa_o, h]
    specs = pair(GLA_W) + pair(GLA_W) + pair(GLA_W, 3) + pair(DIFF_W) + pair(SWA_Q_W) + pair(d)
    return pl.pallas_call(
        functools.partial(_post_kernel, alpha=alpha, tiles_per_sample=s // TOK, ctx_row=0, paired=True),
        grid=(bsz, n_lat // POST_TOK),
        in_specs=specs + [
            pl.BlockSpec((None,) + mods.shape[1:], lambda b, i: (layer, 0, 0, 0)),
            const((d, d)), const((d, hid)), const((d, hid)), const((hid, d)),
            pl.BlockSpec((1, GLA_W), lambda b, i: (0, 0)),
            pl.BlockSpec((2, d), lambda b, i: (0, 0)), pl.BlockSpec((2, d), lambda b, i: (0, 0)),
        ],
        out_specs=pl.BlockSpec((None, POST_TOK, d), lambda b, i: (b, i, 0)),
        out_shape=jax.ShapeDtypeStruct((bsz, n_lat, d), F32),
        compiler_params=pltpu.CompilerParams(
            dimension_semantics=("arbitrary", "arbitrary"), vmem_limit_bytes=POST_VMEM_LIMIT),
        name="post_attention_latent",
    )(*[a for a in acts for _ in range(halves)], mods, w_out, w_g, w_u, w_d, gng, lng, lnb)


def _rope_tables(rows, n_ctx, dim):
    row = jnp.repeat(jnp.arange(rows, dtype=F32), GRID_W)
    col = jnp.tile(jnp.arange(GRID_W, dtype=F32), rows)
    n_freq = dim // 4
    inv = jnp.power(ROPE_BASE, -jnp.arange(n_freq, dtype=F32) / n_freq)
    ang = jnp.concatenate([row[:, None] * inv, col[:, None] * inv], axis=-1)
    cos, sin = jnp.cos(ang), jnp.sin(ang)
    reps = LANES // dim
    cos_t = jnp.tile(jnp.concatenate([cos, cos], axis=-1), (1, reps))
    sin_t = jnp.tile(jnp.concatenate([-sin, sin], axis=-1), (1, reps))
    cos_t = jnp.concatenate([jnp.ones((n_ctx, LANES), F32), cos_t], axis=0)
    sin_t = jnp.concatenate([jnp.zeros((n_ctx, LANES), F32), sin_t], axis=0)
    return cos_t, sin_t


def _reorder_w_in(w_in):
    o = 0
    parts = {}
    for name, width in (("gq", GLA_W), ("gk", GLA_W), ("gv", GLA_W), ("go", GLA_W),
                        ("zf", GLA_GATE_RANK), ("zb", GLA_GATE_RANK),
                        ("dq", DIFF_W), ("dk", DIFF_W), ("dv", DIFF_W),
                        ("sq", SWA_Q_W), ("sk", SWA_KV_W), ("sv", SWA_KV_W)):
        parts[name] = w_in[:, o:o + width]
        o += width
    cols = [parts[n] for n in ("gq", "gk", "gv", "go", "dq", "dk", "dv", "sq", "sk", "sv", "zf", "zb")]
    pad = IN_PAD_W - sum(c.shape[1] for c in cols)
    cols.append(jnp.zeros((w_in.shape[0], pad), w_in.dtype))
    return jnp.concatenate(cols, axis=1)


def kernel(x, c, ctx, c_ctx, w_ada, b_ada, w_in, w_gla_gate, b_gla_gate, gla_norm_g, diff_lambda,
           diff_norm_g, swa_sink, w_out, ln_g, ln_b, w_ffn_gate, w_ffn_up, w_ffn_down):
    bsz, n_lat, d = x.shape
    n_ctx = ctx.shape[1]
    depth = w_ada.shape[0]
    assert n_ctx == TOK and n_lat % TOK == 0 and n_lat % GRID_W == 0
    alpha = (2.0 * depth) ** 0.25
    wdt = MXU_DTYPE

    cond_rows = -(-(bsz + 1) // 8) * 8
    cond =jnp.concatenate([c, c_ctx[None, :], jnp.zeros((cond_rows - bsz - 1, d), F32)], axis=0)
    mods = _mods_call(cond, w_ada, b_ada).reshape(depth, cond_rows, 6, d)
    ctx_row = bsz

    rows = n_lat // GRID_W
    tabs = _rope_tables(rows, n_ctx, DIFF_DH) + _rope_tables(rows, n_ctx, SWA_DH)

    h = (x, ctx)
    for layer in range(depth):
        last = layer == depth - 1
        lam_init = 0.8 - 0.6 * math.exp(-0.3 * layer)
        w_in_p = _reorder_w_in(w_in[layer]).astype(wdt)
        wg = jnp.zeros((LANES, 2 * GLA_W), F32)
        wg = wg.at[0:GLA_GATE_RANK, 0:GLA_W].set(w_gla_gate[layer, 0])
        wg = wg.at[GLA_GATE_RANK:2 * GLA_GATE_RANK, GLA_W:].set(w_gla_gate[layer, 1])
        bg = b_gla_gate[layer].reshape(1, 2 * GLA_W)

        proj = _inproj_call(h, mods, layer, ctx_row, w_in_p, wg.astype(wdt), bg, tabs)
        if layer == 0:
            h = proj[-1]
        gla_in, gates, dqk, dvt, sq, sk, svt = proj[:7]
        o_f, o_b = _gla_call(gla_in, gates)
        diff_o = _diff_call(dqk, dvt, diff_lambda[layer], jnp.full((1, 1), lam_init, F32),
                            (jnp.tile(diff_norm_g[layer], DIFF_HEADS) * (1.0 - lam_init)).reshape(1, DIFF_W))
        swa_o = _swa_call(sq, sk, svt,
                          jnp.broadcast_to(swa_sink[layer][:, None] * LOG2_E, (SWA_HEADS, LANES)), n_ctx)
        gng = jnp.tile(gla_norm_g[layer], GLA_HEADS).reshape(1, GLA_W)
        w_o, w_g, w_u, w_d = (w_out[layer].astype(wdt), w_ffn_gate[layer].astype(wdt),
                              w_ffn_up[layer].astype(wdt), w_ffn_down[layer].astype(wdt))
        if last:
            h = _post_latent_call(o_f, o_b, gla_in, diff_o, swa_o, h, mods, layer, w_o, w_g, w_u, w_d,
                                  gng, ln_g[layer], ln_b[layer], alpha, n_ctx)
        else:
            h = _post_call(o_f, o_b, gla_in, diff_o, swa_o, h, mods, layer, ctx_row, w_o, w_g, w_u, w_d,
                           gng, ln_g[layer], ln_b[layer], alpha)
    return h
```

```python
import functools
import math

import jax
import jax.numpy as jnp
from jax import lax
from jax.experimental import pallas as pl
from jax.experimental.pallas import tpu as pltpu

F32 = jnp.float32
MXU_DTYPE = jnp.bfloat16

GRID_W = 64
GLA_HEADS, GLA_DK, GLA_DV = 4, 64, 64
GLA_GATE_RANK = 16
GLA_GATE_NORM = 16.0
GLA_CHUNK = 64
GLA_SUB = 16
DIFF_HEADS, DIFF_DH = 4, 32
DIFF_DV = 2 * DIFF_DH
SWA_HEADS, SWA_KV_HEADS, SWA_DH = 8, 2, 64
WINDOW = 128
ROPE_BASE = 10000.0
LN_EPS = 1e-6

GLA_W = GLA_HEADS * GLA_DK
DIFF_W = DIFF_HEADS * 2 * DIFF_DH
SWA_Q_W = SWA_HEADS * SWA_DH
SWA_KV_W = SWA_KV_HEADS * SWA_DH
LANES = 128
MXU_TILE = 256
TOK = 256
POST_TOK = 2 * TOK
SWA_TQ = 128
DIFF_VROWS = DIFF_DV + 16
DIFF_KEY_CUT = (13, 17)
INPROJ_LAG = 4
SWA_VROWS = SWA_DH + 16
LOG2_E = math.log2(math.e)
VMEM_LIMIT = 52 * 1024 * 1024
POST_VMEM_LIMIT = 58 * 1024 * 1024


def _mm(a, b):
    return jnp.dot(a.astype(MXU_DTYPE), b.astype(MXU_DTYPE), preferred_element_type=F32)


def _mm_nt(a, b):
    return lax.dot_general(a.astype(MXU_DTYPE), b.astype(MXU_DTYPE),
                           (((1,), (1,)), ((), ())), preferred_element_type=F32)


def _mm_tn(a, b):
    return lax.dot_general(a.astype(MXU_DTYPE), b.astype(MXU_DTYPE),
                           (((0,), (0,)), ((), ())), preferred_element_type=F32)


def _split_mm(a, b_exact):
    hi = a.astype(jnp.bfloat16)
    r1 = a - hi.astype(F32)
    mid = r1.astype(jnp.bfloat16)
    lo = (r1 - mid.astype(F32)).astype(jnp.bfloat16)
    b = b_exact.astype(jnp.bfloat16)
    return (jnp.dot(hi, b, preferred_element_type=F32)
            + jnp.dot(mid, b, preferred_element_type=F32)
            + jnp.dot(lo, b, preferred_element_type=F32))


def _split_mm_t(b_exact, a):
    hi = a.astype(jnp.bfloat16)
    r1 = a - hi.astype(F32)
    mid = r1.astype(jnp.bfloat16)
    lo = (r1 - mid.astype(F32)).astype(jnp.bfloat16)
    b = b_exact.astype(jnp.bfloat16)
    return (jnp.dot(b, hi, preferred_element_type=F32)
            + jnp.dot(b, mid, preferred_element_type=F32)
            + jnp.dot(b, lo, preferred_element_type=F32))


def _ln_plain(x):
    mu = jnp.mean(x, axis=-1, keepdims=True)
    xc = x - mu
    var = jnp.mean(xc * xc, axis=-1, keepdims=True)
    return xc * lax.rsqrt(var + LN_EPS)


def _silu(x):
    return x / (1.0 + jnp.exp(-x))


def _group_mean_matrix(width, group):
    r = lax.broadcasted_iota(jnp.int32, (width, width), 0) // group
    c = lax.broadcasted_iota(jnp.int32, (width, width), 1) // group
    return jnp.where(r == c, 1.0 / group, 0.0).astype(F32)


def _mods_kernel(a_ref, w_ref, b_ref, o_ref):
    a = a_ref[...]
    o_ref[...] = jnp.dot(_silu(a), w_ref[...], preferred_element_type=F32,
                         precision=lax.Precision.HIGHEST) + b_ref[...]


def _mods_call(cond, w_ada, b_ada):
    depth, d, width = w_ada.shape
    rows = cond.shape[0]
    bn = 1536
    return pl.pallas_call(
        _mods_kernel,
        grid=(depth, width // bn),
        in_specs=[
            pl.BlockSpec((rows, d), lambda l, n: (0, 0)),
            pl.BlockSpec((None, d, bn), lambda l, n: (l, 0, n)),
            pl.BlockSpec((None, 1, bn), lambda l, n: (l, 0, n)),
        ],
        out_specs=pl.BlockSpec((None, rows, bn), lambda l, n: (l, 0, n)),
        out_shape=jax.ShapeDtypeStruct((depth, rows, width), F32),
        compiler_params=pltpu.CompilerParams(vmem_limit_bytes=VMEM_LIMIT),
        name="adaln_mods",
    )(cond, w_ada, b_ada.reshape(depth, 1, width))


C_GQ, C_GK, C_GV, C_GO = 0, 256, 512, 768
C_DQ, C_DK, C_DV = 1024, 1280, 1536
C_SQ, C_SK, C_SV = 1792, 2304, 2432
C_Z = 2560
IN_PAD_W = 2688


def _rope(x, cos, sin_signed, half):
    outs = []
    lane = lax.broadcasted_iota(jnp.int32, (1, LANES), 1)
    first = (lane % (2 * half)) < half
    for s in range(x.shape[1] // LANES):
        xs = x[:, s * LANES:(s + 1) * LANES]
        up = pltpu.roll(xs, LANES - half, axis=1)
        dn = pltpu.roll(xs, half, axis=1)
        outs.append(xs * cos + jnp.where(first, up, dn) * sin_signed)
    return outs[0] if len(outs) == 1 else jnp.concatenate(outs, axis=1)


def _inproj_kernel(*refs, split_input):
    j = pl.program_id(1)
    if split_input:
        x_ref, ctx_ref = refs[0:2]
        refs = refs[2:]
        hout_ref = refs[-1]
        refs = refs[:-1]
        sps = x_ref.shape[0]
        for n in range(sps):
            hout_ref[n] = jnp.where(j == 0, ctx_ref[n], x_ref[n])
        h_ref = hout_ref
    else:
        h_ref = refs[0]
        refs = refs[1:]
        sps = h_ref.shape[0]
    (mod_ref, w_ref, wg_ref, bg_ref, cd_ref, sd_ref, cs_ref, ss_ref,
     gla_ref, gate_ref, dqk_ref, dvt_ref, sq_ref, sk_ref, svt_ref) = refs
    cd, sd, cs, ss = cd_ref[...], sd_ref[...], cs_ref[...], ss_ref[...]
    lane = lax.broadcasted_iota(jnp.int32, (1, LANES), 1)
    low = lane < SWA_DH
    ones_d = jnp.where(lax.broadcasted_iota(jnp.int32, (DIFF_VROWS - DIFF_DV, TOK), 0) == 0, 1.0, 0.0)
    ones_s = jnp.where(lax.broadcasted_iota(jnp.int32, (SWA_VROWS - SWA_DH, SWA_TQ), 0) == 0, 1.0, 0.0)
    st = [dict() for _ in range(sps)]

    def norm(n):
        mod = mod_ref[jnp.where(j == 0, 0, n)]
        st[n]["u"] = (_ln_plain(h_ref[n]) * (1.0 + mod[1:2, :]) + mod[0:1, :]).astype(MXU_DTYPE)

    def project(n, name, c0, c1):
        st[n][name] = jnp.dot(st[n]["u"], w_ref[:, c0:c1], preferred_element_type=F32)

    def gate_out(n):
        gpre = _mm(st[n].pop("z"), wg_ref[...]) + bg_ref[...]
        logsig = jnp.minimum(gpre, 0.0) - jnp.log(1.0 + jnp.exp(-jnp.abs(gpre)))
        gate_ref[n] = logsig * (1.0 / GLA_GATE_NORM)

    def diff_out(n):
        pd = st[n].pop("d")
        dq = _rope(pd[:, 0:DIFF_W], cd, sd, DIFF_DH // 2) * (DIFF_DH ** -0.5 * LOG2_E)
        dk = _rope(pd[:, DIFF_W:2 * DIFF_W], cd, sd, DIFF_DH // 2)
        dqk_ref[n, :, 0:DIFF_W] = dq.astype(dqk_ref.dtype)
        dqk_ref[n, :, DIFF_W:2 * DIFF_W] = dk.astype(dqk_ref.dtype)
        vt = pd[:, 2 * DIFF_W:3 * DIFF_W].T
        for hh in range(DIFF_HEADS):
            dvt_ref[n, hh, 0:DIFF_DV, :] = vt[hh * DIFF_DV:(hh + 1) * DIFF_DV, :].astype(dvt_ref.dtype)
            dvt_ref[n, hh, DIFF_DV:DIFF_VROWS, :] = ones_d.astype(dvt_ref.dtype)

    def swa_out(n):
        ps = st[n].pop("s")
        sq = _rope(ps[:, 0:SWA_Q_W], cs, ss, SWA_DH // 2) * (SWA_DH ** -0.5 * LOG2_E)
        sq_ref[n] = sq.astype(sq_ref.dtype)
        sk = _rope(ps[:, SWA_Q_W:SWA_Q_W + SWA_KV_W], cs, ss, SWA_DH // 2)
        sw = pltpu.roll(sk, SWA_DH, axis=1)
        sk_ref[n, :, 0:LANES] = jnp.where(low, sk, sw).astype(sk_ref.dtype)
        sk_ref[n, :, LANES:2 * LANES] = jnp.where(low, sw, sk).astype(sk_ref.dtype)
        svt = ps[:, SWA_Q_W + SWA_KV_W:SWA_Q_W + 2 * SWA_KV_W].T
        for grp in range(SWA_KV_HEADS):
            for i in range(TOK // SWA_TQ):
                svt_ref[n, grp, i, 0:SWA_DH, :] = svt[grp * SWA_DH:(grp + 1) * SWA_DH,
                                                      i * SWA_TQ:(i + 1) * SWA_TQ].astype(svt_ref.dtype)
                svt_ref[n, grp, i, SWA_DH:SWA_VROWS, :] = ones_s.astype(svt_ref.dtype)

    def gla_out(n):
        pg = st[n].pop("g")
        gla_ref[n, :, 0:GLA_W] = pg[:, 0:GLA_W] * (GLA_DK ** -0.5)
        gla_ref[n, :, GLA_W:4 * GLA_W] = pg[:, GLA_W:4 * GLA_W]

    def steps(n):
        return [
            lambda: norm(n),
            lambda: project(n, "z", C_Z, C_Z + LANES),
            lambda: project(n, "d", C_DQ, C_DV + DIFF_W),
            lambda: gate_out(n),
            lambda: project(n, "s", C_SQ, C_SV + SWA_KV_W),
            lambda: diff_out(n),
            lambda: project(n, "g", C_GQ, C_GO + GLA_W),
            lambda: swa_out(n),
            lambda: gla_out(n),
        ]

    lag = INPROJ_LAG
    plans = [steps(n) for n in range(sps)]
    for tick in range(len(plans[0]) + lag * (sps - 1)):
        for n in range(sps):
            i = tick - lag * n
            if 0 <= i < len(plans[n]):
                plans[n][i]()


def _inproj_call(h, mods, layer, ctx_row, w_in_p, wg, bg, tabs):
    split_input = isinstance(h, tuple)
    if split_input:
        x, ctx = h
        bsz, n_lat, d = x.shape
        assert ctx.shape[1] == TOK
        s = n_lat + TOK
    else:
        bsz, s, d = h.shape
    nt = s // TOK
    act = MXU_DTYPE
    sps = 2 if (bsz % 2 == 0 and ctx_row % 2 == 0) else 1
    tab_spec = pl.BlockSpec((TOK, LANES), lambda b, j: (j, 0))

    def mod_index(b, j):
        return (layer, jnp.where(j == 0, ctx_row // sps, b), 0, 0)

    if split_input:
        stream_in = [x, ctx]
        stream_specs = [pl.BlockSpec((sps, TOK, d), lambda b, j: (b, jnp.maximum(j - 1, 0), 0)),
                        pl.BlockSpec((sps, TOK, d), lambda b, j: (b, 0, 0))]
        extra_specs = [pl.BlockSpec((sps, TOK, d), lambda b, j: (b, j, 0))]
        extra_shapes = [jax.ShapeDtypeStruct((bsz, s, d), F32)]
    else:
        stream_in = [h]
        stream_specs = [pl.BlockSpec((sps, TOK, d), lambda b, j: (b, j, 0))]
        extra_specs, extra_shapes = [], []

    return pl.pallas_call(
        functools.partial(_inproj_kernel, split_input=split_input),
        grid=(bsz // sps, nt),
        in_specs=stream_specs + [
            pl.BlockSpec((None, sps, 6, d), mod_index),
            pl.BlockSpec((d, IN_PAD_W), lambda b, j: (0, 0)),
            pl.BlockSpec((LANES, 2 * GLA_W), lambda b, j: (0, 0)),
            pl.BlockSpec((1, 2 * GLA_W), lambda b, j: (0, 0)),
            tab_spec, tab_spec, tab_spec, tab_spec,
        ],
        out_specs=[
            pl.BlockSpec((sps, TOK, 4 * GLA_W), lambda b, j: (b, j, 0)),
            pl.BlockSpec((sps, TOK, 2 * GLA_W), lambda b, j: (b, j, 0)),
            pl.BlockSpec((sps, TOK, 2 * DIFF_W), lambda b, j: (b, j, 0)),
            pl.BlockSpec((sps, DIFF_HEADS, DIFF_VROWS, TOK), lambda b, j: (b, 0, 0, j)),
            pl.BlockSpec((sps, TOK, SWA_Q_W), lambda b, j: (b, j, 0)),
            pl.BlockSpec((sps, TOK, 2 * LANES), lambda b, j: (b, j, 0)),
            pl.BlockSpec((sps, SWA_KV_HEADS, TOK // SWA_TQ, SWA_VROWS, SWA_TQ), lambda b, j: (b, 0, j, 0, 0)),
        ] + extra_specs,
        out_shape=[
            jax.ShapeDtypeStruct((bsz, s, 4 * GLA_W), F32),
            jax.ShapeDtypeStruct((bsz, s, 2 * GLA_W), F32),
            jax.ShapeDtypeStruct((bsz, s, 2 * DIFF_W), act),
            jax.ShapeDtypeStruct((bsz, DIFF_HEADS, DIFF_VROWS, s), act),
            jax.ShapeDtypeStruct((bsz, s, SWA_Q_W), act),
            jax.ShapeDtypeStruct((bsz, s, 2 * LANES), act),
            jax.ShapeDtypeStruct((bsz, SWA_KV_HEADS, s // SWA_TQ, SWA_VROWS, SWA_TQ), act),
        ] + extra_shapes,
        compiler_params=pltpu.CompilerParams(
            dimension_semantics=("arbitrary", "arbitrary"), vmem_limit_bytes=VMEM_LIMIT),
        name="in_projection",
    )(*stream_in, mods, w_in_p, wg, bg, *tabs)


def _gla_direction(gla_ref, gate_ref, gate_col, o_ref, st_ref, reverse, consts):
    tri, head_lane, blockdiag, sub_masks = consts
    c_, sub = GLA_CHUNK, GLA_SUB
    nsub = c_ // sub
    nchunk = TOK // c_
    g_all = gate_ref[:, gate_col:gate_col + GLA_W]
    b_all = _split_mm_t(tri, g_all)
    order = range(nchunk - 1, -1, -1) if reverse else range(nchunk)
    zero = jnp.zeros((), F32)
    chunks = []
    for c in order:
        r0 = c * c_
        q = gla_ref[r0:r0 + c_, 0:GLA_W]
        k = gla_ref[r0:r0 + c_, GLA_W:2 * GLA_W]
        v = gla_ref[r0:r0 + c_, 2 * GLA_W:3 * GLA_W]
        b = b_all[r0:r0 + c_, :]
        b_end = b[0:1, :] if reverse else b[c_ - 1:c_, :]
        atts = []
        for i in range(nsub):
            t0, t1 = i * sub, (i + 1) * sub
            if reverse:
                ref_b = b[t1:t1 + 1, :] if i < nsub - 1 else jnp.zeros((1, GLA_W), F32)
                k0, k1 = t0, c_
            else:
                ref_b = b[t0 - 1:t0, :] if i > 0 else jnp.zeros((1, GLA_W), F32)
                k0, k1 = 0, t1
            qd = q[t0:t1, :] * jnp.exp(b[t0:t1, :] - ref_b)
            lhs = jnp.concatenate([jnp.where(head_lane[hh], qd, zero) for hh in range(GLA_HEADS)], axis=0)
            kk = k[k0:k1, :] * jnp.exp(ref_b - b[k0:k1, :])
            atts.append((_mm_nt(lhs, kk), k0, k1))
        upd = _mm_tn(v, k * jnp.exp(b_end - b))
        chunks.append(dict(r0=r0, v=v, atts=atts, upd=upd, qs=q * jnp.exp(b), decay=jnp.exp(b_end)))
    for ch in chunks:
        o_parts = []
        for i, (att, k0, k1) in enumerate(ch["atts"]):
            res = _mm(jnp.where(sub_masks[(reverse, i)], att, zero), ch["v"][k0:k1, :])
            o_i = res[(GLA_HEADS - 1) * sub:GLA_HEADS * sub, :]
            for hh in range(GLA_HEADS - 2, -1, -1):
                o_i = jnp.where(head_lane[hh], res[hh * sub:(hh + 1) * sub, :], o_i)
            o_parts.append(o_i)
        ch["o_intra"] = jnp.concatenate(o_parts, axis=0)
    st = st_ref[...]
    for ch in chunks:
        o_ref[ch["r0"]:ch["r0"] + c_, :] = ch["o_intra"] + _mm_nt(ch["qs"], st)
        st = st * ch["decay"] + jnp.where(blockdiag, ch["upd"], zero)
    st_ref[...] = st


def _gla_consts():
    c_, sub = GLA_CHUNK, GLA_SUB
    nsub = c_ // sub
    r = lax.broadcasted_iota(jnp.int32, (TOK, TOK), 0)
    cc = lax.broadcasted_iota(jnp.int32, (TOK, TOK), 1)
    same_chunk = (r // c_) == (cc // c_)
    tri_f = jnp.where(same_chunk & (cc <= r), 1.0, 0.0).astype(F32)
    tri_b = jnp.where(same_chunk & (cc >= r), 1.0, 0.0).astype(F32)
    lane = lax.broadcasted_iota(jnp.int32, (1, GLA_W), 1)
    head_lane = [(lane // GLA_DK) == hh for hh in range(GLA_HEADS)]
    blockdiag = (r // GLA_DV) == (cc // GLA_DK)
    sub_masks = {}
    for reverse in (False, True):
        for i in range(nsub):
            nk = (nsub - i) * sub if reverse else (i + 1) * sub
            rr = lax.broadcasted_iota(jnp.int32, (GLA_HEADS * sub, nk), 0) % sub
            kc = lax.broadcasted_iota(jnp.int32, (GLA_HEADS * sub, nk), 1)
            if reverse:
                sub_masks[(reverse, i)] = kc >= rr
            else:
                sub_masks[(reverse, i)] = (kc - (nk - sub)) <= rr
    return tri_f, tri_b, head_lane, blockdiag, sub_masks


def _gla_kernel(gla_f_ref, gate_f_ref, gla_b_ref, gate_b_ref, of_ref, ob_ref, stf_ref, stb_ref):
    @pl.when(pl.program_id(1) == 0)
    def _():
        stf_ref[...] = jnp.zeros_like(stf_ref)
        stb_ref[...] = jnp.zeros_like(stb_ref)

    tri_f, tri_b, head_lane, blockdiag, sub_masks = _gla_consts()
    for smp in range(gla_f_ref.shape[0]):
        _gla_direction(gla_f_ref.at[smp], gate_f_ref.at[smp], 0, of_ref.at[smp], stf_ref.at[smp], False,
                       (tri_f, head_lane, blockdiag, sub_masks))
        _gla_direction(gla_b_ref.at[smp], gate_b_ref.at[smp], GLA_W, ob_ref.at[smp], stb_ref.at[smp], True,
                       (tri_b, head_lane, blockdiag, sub_masks))


def _diff_kernel(q_ref, k_ref, vt_ref, lam_ref, lam0_ref, g_ref, o_ref, *, n_ctx):
    lam_p = lam_ref[...]
    lam = (jnp.exp(jnp.sum(lam_p[0:1, :] * lam_p[1:2, :], axis=-1, keepdims=True))
           - jnp.exp(jnp.sum(lam_p[2:3, :] * lam_p[3:4, :], axis=-1, keepdims=True))
           + lam0_ref[...])
    lane = lax.broadcasted_iota(jnp.int32, (1, DIFF_W), 1)

    def attend(edges):
        parts = len(edges) - 1
        per_smp = 2 * DIFF_HEADS
        n_str = q_ref.shape[0] * per_smp

        def scores(idx):
            smp, hc = divmod(idx, per_smp)
            q = q_ref[smp]
            lo = hc * DIFF_DH
            qm = jnp.where((lane >= lo) & (lane < lo + DIFF_DH), q, jnp.zeros_like(q))
            return [_mm_nt(k_ref[smp, edges[i]:edges[i + 1], :], qm).astype(MXU_DTYPE)
                    for i in range(parts)]

        def probs(sts):
            m = _col_max(sts[0])
            for st in sts[1:]:
                m = jnp.maximum(m, _col_max(st))
            return [jnp.exp2(st - m) for st in sts]

        def values(idx, ps):
            smp, hc = divmod(idx, per_smp)
            acc = None
            for i, p in enumerate(ps):
                pv = _mm(vt_ref[smp, hc // 2, :, edges[i]:edges[i + 1]], p)
                acc = pv if acc is None else acc + pv
            return acc[0:DIFF_DV, :] / acc[DIFF_DV:DIFF_DV + 1, :]

        def finish(smp, comps):
            heads = []
            for hh in range(DIFF_HEADS):
                o_h = comps[2 * hh] - lam * comps[2 * hh + 1]
                ms = jnp.mean(o_h * o_h, axis=0, keepdims=True)
                heads.append(o_h * lax.rsqrt(ms + LN_EPS))
            out = jnp.concatenate(heads, axis=0).T
            o_ref[smp] = (out * g_ref[...]).astype(o_ref.dtype)

        sts = {0: scores(0), 1: scores(1)}
        ps = {0: probs(sts.pop(0))}
        comps = []
        for idx in range(n_str):
            if idx + 2 < n_str:
                sts[idx + 2] = scores(idx + 2)
            if idx + 1 < n_str:
                ps[idx + 1] = probs(sts.pop(idx + 1))
            comps.append(values(idx, ps.pop(idx)))
            if len(comps) == per_smp:
                finish(idx // per_smp, comps)
                comps = []

    j = pl.program_id(1)

    @pl.when(j == 0)
    def _():
        attend((0, n_ctx))

    @pl.when(j > 0)
    def _():
        n_keys = k_ref.shape[1]
        cut = LANES * ((n_keys // LANES) * DIFF_KEY_CUT[0] // DIFF_KEY_CUT[1])
        attend((0, cut, n_keys) if 0 < cut < n_keys else (0, n_keys))


def _col_max(x):
    r = x.shape[0]
    slab = 16
    while r % (2 * slab) == 0 and r // slab > 32:
        slab *= 2
    acc = x[0:slab, :]
    for i in range(1, r // slab):
        acc = jnp.maximum(acc, x[i * slab:(i + 1) * slab, :])
    return jnp.max(acc, axis=0, keepdims=True)


def _diff_call(dqk, dvt, lam_p, lam0, g_eff):
    bsz, s, _ = dqk.shape
    nt = s // TOK
    sps = 2 if bsz % 2 == 0 else 1
    return pl.pallas_call(
        functools.partial(_diff_kernel, n_ctx=TOK),
        grid=(bsz // sps, nt),
        in_specs=[
            pl.BlockSpec((sps, TOK, DIFF_W), lambda b, j: (b, j, 0)),
            pl.BlockSpec((sps, s, DIFF_W), lambda b, j: (b, 0, 1)),
            pl.BlockSpec((sps, DIFF_HEADS, DIFF_VROWS, s), lambda b, j: (b, 0, 0, 0)),
            pl.BlockSpec((4, DIFF_DH), lambda b, j: (0, 0)),
            pl.BlockSpec((1, 1), lambda b, j: (0, 0)),
            pl.BlockSpec((1, DIFF_W), lambda b, j: (0, 0)),
        ],
        out_specs=pl.BlockSpec((sps, TOK, DIFF_W), lambda b, j: (b, j, 0)),
        out_shape=jax.ShapeDtypeStruct((bsz, s, DIFF_W), MXU_DTYPE),
        compiler_params=pltpu.CompilerParams(
            dimension_semantics=("arbitrary", "arbitrary"), vmem_limit_bytes=VMEM_LIMIT),
        name="diff_attention",
    )(dqk, dqk, dvt, lam_p, lam0, g_eff)


def _swa_kernel(q_ref, k_ref, vt_ref, sink_ref, o_ref, *, n_ctx, n_lat):
    j = pl.program_id(1)
    tq = SWA_TQ
    n_win = 3
    n_ctx_t = n_ctx // tq
    s_rows = n_ctx + n_lat
    lane = lax.broadcasted_iota(jnp.int32, (1, LANES), 1)
    low = lane < SWA_DH
    q_per_kv = SWA_HEADS // SWA_KV_HEADS
    chains = [(smp, sub, grp) for smp in range(q_ref.shape[0]) for sub in range(TOK // tq)
              for grp in range(SWA_KV_HEADS)]

    def window(sub):
        p0 = j * TOK + sub * tq - n_ctx
        start = pl.multiple_of(jnp.clip(n_ctx + p0 - tq, 0, s_rows - n_win * tq), tq)
        return p0, start

    def scores(idx):
        smp, sub, grp = chains[idx]
        _, start = window(sub)
        blocks = []
        for r in range(q_per_kv):
            hh = grp * q_per_kv + r
            qp = q_ref[smp, sub * tq:(sub + 1) * tq, (hh // 2) * LANES:(hh // 2 + 1) * LANES]
            blocks.append(jnp.where(low if hh % 2 == 0 else ~low, qp, jnp.zeros_like(qp)))
        lhs = jnp.concatenate(blocks, axis=0)
        st_c = _mm_nt(k_ref[smp, 0:n_ctx, grp * LANES:(grp + 1) * LANES], lhs)
        st_l = _mm_nt(k_ref[smp, pl.ds(start, n_win * tq), grp * LANES:(grp + 1) * LANES], lhs)
        return st_c, st_l

    def probs(idx, st):
        _, sub, grp = chains[idx]
        st_c, st_l = st
        p0, start = window(sub)
        pos_q = p0 + lax.broadcasted_iota(jnp.int32, (1, tq), 1)
        pos_k = start - n_ctx + lax.broadcasted_iota(jnp.int32, (n_win * tq, 1), 0)
        valid = (jnp.abs(pos_k - pos_q) <= WINDOW) & (pos_k >= 0) & (p0 >= 0)
        bias = jnp.where(valid, 0.0, -jnp.inf).astype(F32)
        st_l = st_l + jnp.concatenate([bias] * q_per_kv, axis=1)
        st_c = st_c.astype(MXU_DTYPE)
        st_l = st_l.astype(MXU_DTYPE)
        snk = jnp.concatenate([sink_ref[grp * q_per_kv + r:grp * q_per_kv + r + 1, :]
                               for r in range(q_per_kv)], axis=1)
        m = jnp.maximum(jnp.maximum(_col_max(st_c), _col_max(st_l)).astype(F32), snk)
        mb = m.astype(MXU_DTYPE)
        return jnp.exp2(st_c - mb), jnp.exp2(st_l - mb), jnp.exp2(snk - mb.astype(F32))

    def values(idx, pr):
        smp, sub, grp = chains[idx]
        p_c, p_l, p_snk = pr
        _, start = window(sub)
        t0 = start // tq
        acc = None
        for i in range(n_ctx_t):
            pv = _mm(vt_ref[smp, grp, i], p_c[i * tq:(i + 1) * tq, :])
            acc = pv if acc is None else acc + pv
        for i in range(n_win):
            acc = acc + _mm(vt_ref[smp, grp, t0 + i], p_l[i * tq:(i + 1) * tq, :])
        o = acc[0:SWA_DH, :] / (acc[SWA_DH:SWA_DH + 1, :] + p_snk)
        for hp in range(q_per_kv // 2):
            pair = grp * (q_per_kv // 2) + hp
            both = jnp.concatenate([o[:, (2 * hp) * tq:(2 * hp + 1) * tq],
                                    o[:, (2 * hp + 1) * tq:(2 * hp + 2) * tq]], axis=0)
            o_ref[smp, sub * tq:(sub + 1) * tq, pair * LANES:(pair + 1) * LANES] = both.T.astype(o_ref.dtype)

    n_ch = len(chains)
    sts = {0: scores(0), 1: scores(1)}
    prs = {0: probs(0, sts.pop(0))}
    for idx in range(n_ch):
        if idx + 2 < n_ch:
            sts[idx + 2] = scores(idx + 2)
        if idx + 1 < n_ch:
            prs[idx + 1] = probs(idx + 1, sts.pop(idx + 1))
        values(idx, prs.pop(idx))


def _gla_swa_kernel(gla_f_ref, gate_f_ref, gla_b_ref, gate_b_ref, q_ref, k_ref, vt_ref, sink_ref,
                    of_ref, ob_ref, swa_ref, stf_ref, stb_ref, *, n_ctx, n_lat):
    _gla_kernel(gla_f_ref, gate_f_ref, gla_b_ref, gate_b_ref, of_ref, ob_ref, stf_ref, stb_ref)
    _swa_kernel(q_ref, k_ref, vt_ref, sink_ref, swa_ref, n_ctx=n_ctx, n_lat=n_lat)


def _gla_swa_call(gla_in, gates, sq, sk, svt, sink_tile, n_ctx):
    bsz, s, _ = gla_in.shape
    nt = s // TOK
    sps = 2 if bsz % 2 == 0 else 1

    def fwd(b, j):
        return (b, j, 0)

    def bwd(b, j):
        return (b, jnp.where(j == 0, 0, nt - j), 0)

    return pl.pallas_call(
        functools.partial(_gla_swa_kernel, n_ctx=n_ctx, n_lat=s - n_ctx),
        grid=(bsz // sps, nt),
        in_specs=[
            pl.BlockSpec((sps, TOK, 4 * GLA_W), fwd),
            pl.BlockSpec((sps, TOK, 2 * GLA_W), fwd),
            pl.BlockSpec((sps, TOK, 4 * GLA_W), bwd),
            pl.BlockSpec((sps, TOK, 2 * GLA_W), bwd),
            pl.BlockSpec((sps, TOK, SWA_Q_W), fwd),
            pl.BlockSpec((sps, s, 2 * LANES), lambda b, j: (b, 0, 0)),
            pl.BlockSpec((sps, SWA_KV_HEADS, s // SWA_TQ, SWA_VROWS, SWA_TQ), lambda b, j: (b, 0, 0, 0, 0)),
            pl.BlockSpec((SWA_HEADS, LANES), lambda b, j: (0, 0)),
        ],
        out_specs=[
            pl.BlockSpec((sps, TOK, GLA_W), fwd),
            pl.BlockSpec((sps, TOK, GLA_W), bwd),
            pl.BlockSpec((sps, TOK, SWA_Q_W), fwd),
        ],
        out_shape=[jax.ShapeDtypeStruct((bsz, s, GLA_W), F32)] * 2
                  + [jax.ShapeDtypeStruct((bsz, s, SWA_Q_W), MXU_DTYPE)],
        scratch_shapes=[pltpu.VMEM((sps, GLA_W, GLA_W), F32), pltpu.VMEM((sps, GLA_W, GLA_W), F32)],
        compiler_params=pltpu.CompilerParams(
            dimension_semantics=("arbitrary", "arbitrary"), vmem_limit_bytes=VMEM_LIMIT),
        name="gla_swa",
    )(gla_in, gates, gla_in, gates, sq, sk, svt, sink_tile)


def _ffn_chunk_bounds(hid):
    cut = -(-(hid // MXU_TILE) // 2) * MXU_TILE
    return ((0, cut), (cut, hid))


def _post_kernel(*refs, alpha, tiles_per_sample, ctx_row, paired):
    halves = POST_TOK // TOK
    n_act = 6
    if paired:
        acts = [refs[halves * i:halves * (i + 1)] for i in range(n_act)]
        load = lambda i, x: acts[i][x][...]
    else:
        acts = refs[:n_act]
        load = lambda i, x: acts[i][x * TOK:(x + 1) * TOK, :]
    (mod_ref, wo_ref, wg_ref, wu_ref, wd_ref, gng_ref, lng_ref, lnb_ref, o_ref) = refs[len(refs) - 9:]
    hid = wg_ref.shape[1]
    bounds = _ffn_chunk_bounds(hid)
    gm = _group_mean_matrix(GLA_W, GLA_DV)
    st = [dict() for _ in range(halves)]

    def out_proj(x):
        if paired:
            row = pl.program_id(0)
        else:
            g = pl.program_id(0) * halves + x
            row = jnp.where(g % tiles_per_sample == 0, ctx_row, g // tiles_per_sample)
        st[x]["mod"] = mod_ref[row]
        o = load(0, x) + load(1, x)
        ms = _split_mm(o * o, gm)
        gla = o * lax.rsqrt(ms + LN_EPS) * gng_ref[...] * _silu(load(2, x))
        st[x]["y"] = (_mm(gla, wo_ref[0:GLA_W, :])
                      + _mm(load(3, x), wo_ref[GLA_W:GLA_W + DIFF_W, :])
                      + _mm(load(4, x), wo_ref[GLA_W + DIFF_W:, :]))

    def mid_norm(x):
        mod = st[x]["mod"]
        z = alpha * load(5, x) + mod[2:3, :] * st[x].pop("y")
        hm = _ln_plain(z) * lng_ref[0:1, :] + lnb_ref[0:1, :]
        st[x]["hm"] = hm
        st[x]["u"] = (_ln_plain(hm) * (1.0 + mod[4:5, :]) + mod[3:4, :]).astype(MXU_DTYPE)

    def gate_up(x, c):
        u = st[x]["u"]
        c0, c1 = bounds[c]
        st[x]["g", c] = jnp.dot(u, wg_ref[:, c0:c1], preferred_element_type=F32)
        st[x]["p", c] = jnp.dot(u, wu_ref[:, c0:c1], preferred_element_type=F32)

    def act(x, c):
        st[x]["a", c] = (_silu(st[x].pop(("g", c))) * st[x].pop(("p", c))).astype(MXU_DTYPE)

    def down(x, c):
        c0, c1 = bounds[c]
        f = jnp.dot(st[x].pop(("a", c)), wd_ref[c0:c1, :], preferred_element_type=F32)
        st[x]["f"] = f if c == 0 else st[x]["f"] + f

    def final(x):
        rows = slice(x * TOK, (x + 1) * TOK)
        z = alpha * st[x]["hm"] + st[x]["mod"][5:6, :] * st[x]["f"]
        o_ref[rows, :] = _ln_plain(z) * lng_ref[1:2, :] + lnb_ref[1:2, :]

    a_, b_ = 0, 1
    out_proj(a_)
    out_proj(b_)
    mid_norm(a_)
    gate_up(a_, 0)
    mid_norm(b_)
    gate_up(a_, 1)
    act(a_, 0)
    gate_up(b_, 0)
    act(a_, 1)
    down(a_, 0)
    gate_up(b_, 1)
    act(b_, 0)
    down(a_, 1)
    act(b_, 1)
    down(b_, 0)
    final(a_)
    down(b_, 1)
    final(b_)


def _post_call(o_f, o_b, gla_in, diff_o, swa_o, h, mods, layer, ctx_row, w_out, w_g, w_u, w_d, gng, lng, lnb, alpha):
    bsz, s, d = h.shape
    rows = bsz * s
    assert rows % POST_TOK == 0
    hid = w_g.shape[1]
    flat = lambda a: a.reshape(rows, a.shape[-1])
    tile = lambda w: pl.BlockSpec((POST_TOK, w), lambda t: (t, 0))
    single = pl.Buffered(1)
    const = lambda shape: pl.BlockSpec(shape, lambda t: (0,) * len(shape), pipeline_mode=single)
    out = pl.pallas_call(
        functools.partial(_post_kernel, alpha=alpha, tiles_per_sample=s // TOK, ctx_row=ctx_row, paired=False),
        grid=(rows // POST_TOK,),
        in_specs=[
            tile(GLA_W), tile(GLA_W),
            pl.BlockSpec((POST_TOK, GLA_W), lambda t: (t, 3)),
            tile(DIFF_W), tile(SWA_Q_W), tile(d),
            pl.BlockSpec((None,) + mods.shape[1:], lambda t: (layer, 0, 0, 0)),
            const((d, d)), const((d, hid)), const((d, hid)), const((hid, d)),
            pl.BlockSpec((1, GLA_W), lambda t: (0, 0)),
            pl.BlockSpec((2, d), lambda t: (0, 0)), pl.BlockSpec((2, d), lambda t: (0, 0)),
        ],
        out_specs=tile(d),
        out_shape=jax.ShapeDtypeStruct((rows, d), F32),
        compiler_params=pltpu.CompilerParams(
            dimension_semantics=("arbitrary",), vmem_limit_bytes=POST_VMEM_LIMIT),
        name="post_attention",
    )(flat(o_f), flat(o_b), flat(gla_in), flat(diff_o), flat(swa_o), flat(h), mods,
      w_out, w_g, w_u, w_d, gng, lng, lnb)
    return out.reshape(bsz, s, d)


def _post_latent_call(o_f, o_b, gla_in, diff_o, swa_o, h, mods, layer, w_out, w_g, w_u, w_d, gng, lng, lnb,
                      alpha, n_ctx):
    bsz, s, d = h.shape
    n_lat = s - n_ctx
    assert n_ctx == TOK and n_lat % POST_TOK == 0
    hid = w_g.shape[1]
    halves = POST_TOK // TOK
    first = n_ctx // TOK

    def pair(w, col=0):
        return [pl.BlockSpec((None, TOK, w), functools.partial(lambda b, i, x: (b, first + halves * i + x, col), x=x))
                for x in range(halves)]

    single = pl.Buffered(1)
    const = lambda shape: pl.BlockSpec(shape, lambda b, i: (0,) * len(shape), pipeline_mode=single)
    acts = [o_f, o_b, gla_in, diff_o, swa_o, h]
    specs = pair(GLA_W) + pair(GLA_W) + pair(GLA_W, 3) + pair(DIFF_W) + pair(SWA_Q_W) + pair(d)
    return pl.pallas_call(
        functools.partial(_post_kernel, alpha=alpha, tiles_per_sample=s // TOK, ctx_row=0, paired=True),
        grid=(bsz, n_lat // POST_TOK),
        in_specs=specs + [
            pl.BlockSpec((None,) + mods.shape[1:], lambda b, i: (layer, 0, 0, 0)),
            const((d, d)), const((d, hid)), const((d, hid)), const((hid, d)),
            pl.BlockSpec((1, GLA_W), lambda b, i: (0, 0)),
            pl.BlockSpec((2, d), lambda b, i: (0, 0)), pl.BlockSpec((2, d), lambda b, i: (0, 0)),
        ],
        out_specs=pl.BlockSpec((None, POST_TOK, d), lambda b, i: (b, i, 0)),
        out_shape=jax.ShapeDtypeStruct((bsz, n_lat, d), F32),
        compiler_params=pltpu.CompilerParams(
            dimension_semantics=("arbitrary", "arbitrary"), vmem_limit_bytes=POST_VMEM_LIMIT),
        name="post_attention_latent",
    )(*[a for a in acts for _ in range(halves)], mods, w_out, w_g, w_u, w_d, gng, lng, lnb)


def _rope_tables(rows, n_ctx, dim):
    row = jnp.repeat(jnp.arange(rows, dtype=F32), GRID_W)
    col = jnp.tile(jnp.arange(GRID_W, dtype=F32), rows)
    n_freq = dim // 4
    inv = jnp.power(ROPE_BASE, -jnp.arange(n_freq, dtype=F32) / n_freq)
    ang = jnp.concatenate([row[:, None] * inv, col[:, None] * inv], axis=-1)
    cos, sin = jnp.cos(ang), jnp.sin(ang)
    reps = LANES // dim
    cos_t = jnp.tile(jnp.concatenate([cos, cos], axis=-1), (1, reps))
    sin_t = jnp.tile(jnp.concatenate([-sin, sin], axis=-1), (1, reps))
    cos_t = jnp.concatenate([jnp.ones((n_ctx, LANES), F32), cos_t], axis=0)
    sin_t = jnp.concatenate([jnp.zeros((n_ctx, LANES), F32), sin_t], axis=0)
    return cos_t, sin_t


def _reorder_w_in(w_in):
    o = 0
    parts = {}
    for name, width in (("gq", GLA_W), ("gk", GLA_W), ("gv", GLA_W), ("go", GLA_W),
                        ("zf", GLA_GATE_RANK), ("zb", GLA_GATE_RANK),
                        ("dq", DIFF_W), ("dk", DIFF_W), ("dv", DIFF_W),
                        ("sq", SWA_Q_W), ("sk", SWA_KV_W), ("sv", SWA_KV_W)):
        parts[name] = w_in[:, o:o + width]
        o += width
    cols = [parts[n] for n in ("gq", "gk", "gv", "go", "dq", "dk", "dv", "sq", "sk", "sv", "zf", "zb")]
    pad = IN_PAD_W - sum(c.shape[1] for c in cols)
    cols.append(jnp.zeros((w_in.shape[0], pad), w_in.dtype))
    return jnp.concatenate(cols, axis=1)


def kernel(x, c, ctx, c_ctx, w_ada, b_ada, w_in, w_gla_gate, b_gla_gate, gla_norm_g, diff_lambda,
           diff_norm_g, swa_sink, w_out, ln_g, ln_b, w_ffn_gate, w_ffn_up, w_ffn_down):
    bsz, n_lat, d = x.shape
    n_ctx = ctx.shape[1]
    depth = w_ada.shape[0]
    assert n_ctx == TOK and n_lat % TOK == 0 and n_lat % GRID_W == 0
    alpha = (2.0 * depth) ** 0.25
    wdt = MXU_DTYPE

    cond_rows = -(-(bsz + 1) // 8) * 8
    cond =jnp.concatenate([c, c_ctx[None, :], jnp.zeros((cond_rows - bsz - 1, d), F32)], axis=0)
    mods = _mods_call(cond, w_ada, b_ada).reshape(depth, cond_rows, 6, d)
    ctx_row = bsz

    rows = n_lat // GRID_W
    tabs = _rope_tables(rows, n_ctx, DIFF_DH) + _rope_tables(rows, n_ctx, SWA_DH)

    h = (x, ctx)
    for layer in range(depth):
        last = layer == depth - 1
        lam_init = 0.8 - 0.6 * math.exp(-0.3 * layer)
        w_in_p = _reorder_w_in(w_in[layer]).astype(wdt)
        wg = jnp.zeros((LANES, 2 * GLA_W), F32)
        wg = wg.at[0:GLA_GATE_RANK, 0:GLA_W].set(w_gla_gate[layer, 0])
        wg = wg.at[GLA_GATE_RANK:2 * GLA_GATE_RANK, GLA_W:].set(w_gla_gate[layer, 1])
        bg = b_gla_gate[layer].reshape(1, 2 * GLA_W)

        proj = _inproj_call(h, mods, layer, ctx_row, w_in_p, wg.astype(wdt), bg, tabs)
        if layer == 0:
            h = proj[-1]
        gla_in, gates, dqk, dvt, sq, sk, svt = proj[:7]
        o_f, o_b, swa_o = _gla_swa_call(
            gla_in, gates, sq, sk, svt,
            jnp.broadcast_to(swa_sink[layer][:, None] * LOG2_E, (SWA_HEADS, LANES)), n_ctx)
        diff_o = _diff_call(dqk, dvt, diff_lambda[layer], jnp.full((1, 1), lam_init, F32),
                            (jnp.tile(diff_norm_g[layer], DIFF_HEADS) * (1.0 - lam_init)).reshape(1, DIFF_W))
        gng = jnp.tile(gla_norm_g[layer], GLA_HEADS).reshape(1, GLA_W)
        w_o, w_g, w_u, w_d = (w_out[layer].astype(wdt), w_ffn_gate[layer].astype(wdt),
                              w_ffn_up[layer].astype(wdt), w_ffn_down[layer].astype(wdt))
        if last:
            h = _post_latent_call(o_f, o_b, gla_in, diff_o, swa_o, h, mods, layer, w_o, w_g, w_u, w_d,
                                  gng, ln_g[layer], ln_b[layer], alpha, n_ctx)
        else:
            h = _post_call(o_f, o_b, gla_in, diff_o, swa_o, h, mods, layer, ctx_row, w_o, w_g, w_u, w_d,
                           gng, ln_g[layer], ln_b[layer], alpha)
    return h
```
